```python
import math
import jax, jax.numpy as jnp
from jax import lax
import numpy as np

D_MODEL = 1024
BATCH = 8
SEQ = 4096
DEPTH = 2

CHUNK = 64
Q_BLOCK = 128
NORM_EPS = 1e-5
NEG_INF = -1e30
MAX_POS_OFFSET = 4096

MLA_HEADS = 6
MLA_NOPE = 64
MLA_ROPE = 32
MLA_V = 64
MLA_Q_RANK = 256
MLA_KV_RANK = 128
MLA_WIDTH = MLA_HEADS * MLA_V
ROPE_THETA = 10000.0

SSM_GROUPS = 24
SSM_CH = 16
SSM_STATE = 64
SSM_WIDTH = SSM_GROUPS * SSM_CH
STEP_MIN = 1e-3
STEP_MAX = 1e-1

SB_HEADS = 4
SB_DIM = 64
SB_WIDTH = SB_HEADS * SB_DIM

MIX_WIDTH = MLA_WIDTH + SSM_WIDTH + SB_WIDTH
IN_WIDTHS = (MLA_Q_RANK, MLA_KV_RANK, MLA_ROPE, SSM_WIDTH, SB_WIDTH, SB_WIDTH, SB_WIDTH)
IN_WIDTH = MLA_Q_RANK + MLA_KV_RANK + MLA_ROPE + SSM_WIDTH + 3 * SB_WIDTH

MEM_LEN = 256
MEM_HEADS = 4
MEM_HEAD_DIM = D_MODEL // MEM_HEADS

PEER_HEADS = 8
PEER_N_KEYS = 128
PEER_EXPERTS = PEER_N_KEYS * PEER_N_KEYS
PEER_TOPK = 16
PEER_KEY_DIM = 128
PEER_BLOCK = 128

DN_ALPHA = (2 * DEPTH) ** 0.25
DN_BETA = (8 * DEPTH) ** -0.25

kernel_name = 'hybrid_mla_s5_stickbreak_peer_deepnorm'


def layer_norm(x, g, b):
    xf = x.astype(jnp.float32)
    mu = jnp.mean(xf, axis=-1, keepdims=True)
    var = jnp.mean(jnp.square(xf - mu), axis=-1, keepdims=True)
    return ((xf - mu) * lax.rsqrt(var + NORM_EPS) * g + b).astype(x.dtype)


def group_rms(y):
    yf = y.astype(jnp.float32)
    return (yf * lax.rsqrt(jnp.mean(yf * yf, axis=-1, keepdims=True) + NORM_EPS)).astype(y.dtype)


def rms_norm(x, g):
    return group_rms(x) * g


def split_columns(t, widths):
    outs, start = [], 0
    for w in widths:
        outs.append(t[..., start:start + w])
        start += w
    return outs


def rope(x, pos):
    half = x.shape[-1] // 2
    freq = ROPE_THETA ** (-jnp.arange(half, dtype=jnp.float32) / half)
    ang = pos.astype(jnp.float32)[..., None] * freq
    cos = jnp.cos(ang)[:, :, None, :]
    sin = jnp.sin(ang)[:, :, None, :]
    x1 = x[..., :half].astype(jnp.float32)
    x2 = x[..., half:].astype(jnp.float32)
    return jnp.concatenate([x1 * cos - x2 * sin, x1 * sin + x2 * cos], axis=-1).astype(x.dtype)


def sweep_query_blocks(fn, q):
    bsz, seq = q.shape[0], q.shape[1]
    nblk = seq // Q_BLOCK
    qb = jnp.moveaxis(q.reshape(bsz, nblk, Q_BLOCK, q.shape[2], q.shape[3]), 1, 0)
    out = lax.map(lambda a: fn(a[0], a[1] * Q_BLOCK + jnp.arange(Q_BLOCK)), (qb, jnp.arange(nblk)))
    out = jnp.moveaxis(out, 0, 1)
    return out.reshape(bsz, seq, out.shape[3], out.shape[4])


def chunk_causal_softmax_attention(q, k, v):
    seq = k.shape[1]
    key_chunk = jnp.arange(seq) // CHUNK
    scale = q.shape[-1] ** -0.5

    def block(qi, t):
        allowed = key_chunk[None, :] <= (t // CHUNK)[:, None]
        s = jnp.einsum('bqhd,bshd->bhqs', qi, k).astype(jnp.float32) * scale
        p = jax.nn.softmax(jnp.where(allowed, s, NEG_INF), axis=-1)
        return jnp.einsum('bhqs,bshd->bqhd', p.astype(v.dtype), v)

    return sweep_query_blocks(block, q)


def stick_breaking_attention(q, k, v):
    seq = k.shape[1]
    key_pos = jnp.arange(seq)
    scale = q.shape[-1] ** -0.5

    def block(qi, t):
        earlier = key_pos[None, :] < t[:, None]
        z = jnp.einsum('bqhd,bshd->bhqs', qi, k).astype(jnp.float32) * scale
        log_fail = jnp.where(earlier, jax.nn.log_sigmoid(-z), 0.0)
        log_between = lax.cumsum(log_fail, axis=3, reverse=True) - log_fail
        w = jnp.where(earlier, jnp.exp(jax.nn.log_sigmoid(z) + log_between), 0.0)
        return jnp.einsum('bhqs,bshd->bqhd', w.astype(v.dtype), v)

    return sweep_query_blocks(block, q)


def complex_linear_combine(e1, e2):
    a1r, a1i, b1r, b1i = e1
    a2r, a2i, b2r, b2i = e2
    return (a2r * a1r - a2i * a1i,
            a2r * a1i + a2i * a1r,
            a2r * b1r - a2i * b1i + b2r,
            a2r * b1i + a2i * b1r + b2i)


def s5_glu(u, lam_re, lam_im, log_step, b_re, b_im, c_re, c_im, d_skip, w_glu, b_glu):
    f32 = jnp.float32
    bsz, seq, _ = u.shape
    ug = u.reshape(bsz, seq, SSM_GROUPS, SSM_CH).astype(f32)
    lr, li = lam_re.astype(f32), lam_im.astype(f32)
    step = jnp.exp(log_step.astype(f32))[:, None]
    decay = jnp.exp(lr * step)
    ab_re, ab_im = decay * jnp.cos(li * step), decay * jnp.sin(li * step)
    inv = 1.0 / (lr * lr + li * li)
    f_re = ((ab_re - 1.0) * lr + ab_im * li) * inv
    f_im = (ab_im * lr - (ab_re - 1.0) * li) * inv
    br, bi = b_re.astype(f32), b_im.astype(f32)
    bb_re = f_re[..., None] * br - f_im[..., None] * bi
    bb_im = f_re[..., None] * bi + f_im[..., None] * br
    bu_re = jnp.einsum('bsgh,gph->bsgp', ug, bb_re)
    bu_im = jnp.einsum('bsgh,gph->bsgp', ug, bb_im)
    a_re = jnp.broadcast_to(ab_re, (1, seq) + ab_re.shape)
    a_im = jnp.broadcast_to(ab_im, (1, seq) + ab_im.shape)
    _, _, h_re, h_im = lax.associative_scan(complex_linear_combine, (a_re, a_im, bu_re, bu_im), axis=1)
    y = (jnp.einsum('bsgp,ghp->bsgh', h_re, c_re.astype(f32))
         - jnp.einsum('bsgp,ghp->bsgh', h_im, c_im.astype(f32))
         + d_skip.astype(f32) * ug)
    y = jax.nn.gelu(y.reshape(bsz, seq, SSM_WIDTH))
    y = y * jax.nn.sigmoid(y @ w_glu.astype(f32) + b_glu.astype(f32))
    return y.astype(u.dtype)


def hybrid_mixer(x, positions, w_in, g_cq, g_ckv, w_uq, w_ukv,
                 lam_re, lam_im, log_step, b_re, b_im, c_re, c_im, d_skip,
                 w_glu, b_glu, g_mix, w_out):
    bsz, seq, _ = x.shape
    c_q, c_kv, k_r, u_ssm, q_sb, k_sb, v_sb = split_columns(x @ w_in, IN_WIDTHS)
    q = (rms_norm(c_q, g_cq) @ w_uq).reshape(bsz, seq, MLA_HEADS, MLA_NOPE + MLA_ROPE)
    kv = (rms_norm(c_kv, g_ckv) @ w_ukv).reshape(bsz, seq, MLA_HEADS, MLA_NOPE + MLA_V)
    q = jnp.concatenate([q[..., :MLA_NOPE], rope(q[..., MLA_NOPE:], positions)], axis=-1)
    k_rope = jnp.broadcast_to(rope(k_r[:, :, None, :], positions), (bsz, seq, MLA_HEADS, MLA_ROPE))
    k = jnp.concatenate([kv[..., :MLA_NOPE], k_rope], axis=-1)
    y_mla = chunk_causal_softmax_attention(q, k, kv[..., MLA_NOPE:]).reshape(bsz, seq, MLA_WIDTH)
    y_ssm = s5_glu(u_ssm, lam_re, lam_im, log_step, b_re, b_im, c_re, c_im, d_skip, w_glu, b_glu)
    y_sb = stick_breaking_attention(q_sb.reshape(bsz, seq, SB_HEADS, SB_DIM),
                                    k_sb.reshape(bsz, seq, SB_HEADS, SB_DIM),
                                    v_sb.reshape(bsz, seq, SB_HEADS, SB_DIM)).reshape(bsz, seq, SB_WIDTH)
    y = jnp.concatenate([group_rms(y_mla), group_rms(y_ssm), group_rms(y_sb)], axis=-1) * g_mix
    return y @ w_out


def memory_cross_attention(x, mem, w_mq, w_mkv, w_mo):
    bsz, seq, _ = x.shape
    q = (x @ w_mq).reshape(bsz, seq, MEM_HEADS, MEM_HEAD_DIM)
    k, v = jnp.split((mem @ w_mkv).reshape(mem.shape[0], mem.shape[1], 2, MEM_HEADS, MEM_HEAD_DIM), 2, axis=2)
    k, v = k[:, :, 0], v[:, :, 0]
    s = jnp.einsum('bqhd,bmhd->bhqm', q, k).astype(jnp.float32) * MEM_HEAD_DIM ** -0.5
    p = jax.nn.softmax(s, axis=-1).astype(v.dtype)
    o = jnp.einsum('bhqm,bmhd->bqhd', p, v).reshape(bsz, seq, D_MODEL)
    return o @ w_mo


def peer_ffn(x, w_pq, sub_keys, expert_u, expert_v):
    bsz, seq, d = x.shape
    q = (x @ w_pq).reshape(bsz, seq, PEER_HEADS, 2, PEER_KEY_DIM)
    scores = jnp.einsum('bshcd,hcnd->bshcn', q, sub_keys).astype(jnp.float32)
    top_v, top_i = lax.top_k(scores, PEER_TOPK)
    cand = (top_v[..., 0, :, None] + top_v[..., 1, None, :]).reshape(bsz, seq, PEER_HEADS, PEER_TOPK * PEER_TOPK)
    cand_idx = (top_i[..., 0, :, None] * PEER_N_KEYS + top_i[..., 1, None, :]).reshape(
        bsz, seq, PEER_HEADS, PEER_TOPK * PEER_TOPK)
    best_v, best_pos = lax.top_k(cand, PEER_TOPK)
    idx = jnp.take_along_axis(cand_idx, best_pos, axis=-1)
    gate = jax.nn.softmax(best_v, axis=-1)
    n_tok = bsz * seq
    nblk = n_tok // PEER_BLOCK
    n_sel = PEER_HEADS * PEER_TOPK
    xb = x.reshape(nblk, PEER_BLOCK, d)
    ib = idx.reshape(nblk, PEER_BLOCK, n_sel)
    gb = gate.reshape(nblk, PEER_BLOCK, n_sel).astype(x.dtype)

    def block(args):
        xt, it, gt = args
        h = jax.nn.gelu(jnp.einsum('ted,td->te', expert_u[it], xt))
        return jnp.einsum('te,ted->td', gt * h, expert_v[it])

    return lax.map(block, (xb, ib, gb)).reshape(bsz, seq, d)


def setup_inputs(seed: int = 0) -> dict:
    key = jax.random.key(seed)
    ks = iter(jax.random.split(key, 48))
    f32 = jnp.float32
    L = DEPTH

    def nrm(shape, scale):
        return jax.random.normal(next(ks), shape, f32) * scale

    x = nrm((BATCH, SEQ, D_MODEL), 1.0)
    mem = nrm((BATCH, MEM_LEN, D_MODEL), 1.0)
    offset = jax.random.randint(next(ks), (BATCH, 1), 0, MAX_POS_OFFSET, dtype=jnp.int32)
    positions = offset + jnp.arange(SEQ, dtype=jnp.int32)[None, :]
    n_idx = jnp.arange(SSM_STATE, dtype=f32)
    return {
        'x': x,
        'mem': mem,
        'positions': positions,
        'w_in': nrm((L, D_MODEL, IN_WIDTH), D_MODEL ** -0.5),
        'g_cq': 1.0 + nrm((L, MLA_Q_RANK), 0.02),
        'g_ckv': 1.0 + nrm((L, MLA_KV_RANK), 0.02),
        'w_uq': nrm((L, MLA_Q_RANK, MLA_HEADS * (MLA_NOPE + MLA_ROPE)), MLA_Q_RANK ** -0.5),
        'w_ukv': nrm((L, MLA_KV_RANK, MLA_HEADS * (MLA_NOPE + MLA_V)), MLA_KV_RANK ** -0.5),
        'ssm_lam_re': -0.5 + nrm((L, SSM_GROUPS, SSM_STATE), 0.01),
        'ssm_lam_im': math.pi * n_idx + nrm((L, SSM_GROUPS, SSM_STATE), 0.01),
        'ssm_log_step': jax.random.uniform(next(ks), (L, SSM_GROUPS), f32, math.log(STEP_MIN), math.log(STEP_MAX)),
        'ssm_b_re': nrm((L, SSM_GROUPS, SSM_STATE, SSM_CH), (2 * SSM_CH) ** -0.5),
        'ssm_b_im': nrm((L, SSM_GROUPS, SSM_STATE, SSM_CH), (2 * SSM_CH) ** -0.5),
        'ssm_c_re': nrm((L, SSM_GROUPS, SSM_CH, SSM_STATE), SSM_STATE ** -0.5),
        'ssm_c_im': nrm((L, SSM_GROUPS, SSM_CH, SSM_STATE), SSM_STATE ** -0.5),
        'ssm_d': nrm((L, SSM_GROUPS, SSM_CH), 1.0),
        'w_glu': nrm((L, SSM_WIDTH, SSM_WIDTH), SSM_WIDTH ** -0.5),
        'b_glu': nrm((L, SSM_WIDTH), 0.01),
        'g_mix': 1.0 + nrm((L, MIX_WIDTH), 0.02),
        'w_out': nrm((L, MIX_WIDTH, D_MODEL), MIX_WIDTH ** -0.5 * DN_BETA),
        'ln_mix_g': 1.0 + nrm((L, D_MODEL), 0.02),
        'ln_mix_b': nrm((L, D_MODEL), 0.01),
        'w_mq': nrm((L, D_MODEL, D_MODEL), D_MODEL ** -0.5),
        'w_mkv': nrm((L, D_MODEL, 2 * D_MODEL), D_MODEL ** -0.5),
        'w_mo': nrm((L, D_MODEL, D_MODEL), D_MODEL ** -0.5 * DN_BETA),
        'ln_mem_g': 1.0 + nrm((L, D_MODEL), 0.02),
        'ln_mem_b': nrm((L, D_MODEL), 0.01),
        'w_pq': nrm((L, D_MODEL, PEER_HEADS * 2 * PEER_KEY_DIM), D_MODEL ** -0.5),
        'peer_sub_keys': nrm((L, PEER_HEADS, 2, PEER_N_KEYS, PEER_KEY_DIM), PEER_KEY_DIM ** -0.5),
        'peer_u': nrm((L, PEER_EXPERTS, D_MODEL), D_MODEL ** -0.5),
        'peer_v': nrm((L, PEER_EXPERTS, D_MODEL), PEER_HEADS ** -0.5 * DN_BETA),
        'ln_ffn_g': 1.0 + nrm((L, D_MODEL), 0.02),
        'ln_ffn_b': nrm((L, D_MODEL), 0.01),
    }


def reference(x, mem, positions, w_in, g_cq, g_ckv, w_uq, w_ukv,
              ssm_lam_re, ssm_lam_im, ssm_log_step, ssm_b_re, ssm_b_im, ssm_c_re, ssm_c_im, ssm_d,
              w_glu, b_glu, g_mix, w_out, ln_mix_g, ln_mix_b,
              w_mq, w_mkv, w_mo, ln_mem_g, ln_mem_b,
              w_pq, peer_sub_keys, peer_u, peer_v, ln_ffn_g, ln_ffn_b):
    for l in range(DEPTH):
        mix = hybrid_mixer(x, positions, w_in[l], g_cq[l], g_ckv[l], w_uq[l], w_ukv[l],
                           ssm_lam_re[l], ssm_lam_im[l], ssm_log_step[l], ssm_b_re[l], ssm_b_im[l],
                           ssm_c_re[l], ssm_c_im[l], ssm_d[l], w_glu[l], b_glu[l], g_mix[l], w_out[l])
        x = layer_norm(DN_ALPHA * x + mix, ln_mix_g[l], ln_mix_b[l])
        x = layer_norm(DN_ALPHA * x + memory_cross_attention(x, mem, w_mq[l], w_mkv[l], w_mo[l]),
                       ln_mem_g[l], ln_mem_b[l])
        x = layer_norm(DN_ALPHA * x + peer_ffn(x, w_pq[l], peer_sub_keys[l], peer_u[l], peer_v[l]),
                       ln_ffn_g[l], ln_ffn_b[l])
    return x
```

```python
import functools
import math

import jax
import jax.numpy as jnp
from jax import lax
from jax.experimental import pallas as pl
from jax.experimental.pallas import tpu as pltpu

F32 = jnp.float32
BF16 = jnp.bfloat16

D_MODEL = 1024
CHUNK = 64
NORM_EPS = 1e-5
NEG_BIG = -1e30

MLA_HEADS = 6
MLA_NOPE = 64
MLA_ROPE = 32
MLA_V = 64
MLA_Q_RANK = 256
MLA_KV_RANK = 128
ROPE_THETA = 10000.0

SSM_GROUPS = 24
SSM_CH = 16
SSM_STATE = 64
SSM_WIDTH = SSM_GROUPS * SSM_CH
SSM_STATES = SSM_GROUPS * SSM_STATE

SB_HEADS = 4
SB_DIM = 64
SB_WIDTH = SB_HEADS * SB_DIM

MEM_HEADS = 4
MEM_HEAD_DIM = D_MODEL // MEM_HEADS

PEER_HEADS = 8
PEER_N_KEYS = 128
PEER_TOPK = 16
PEER_KEY_DIM = 128
PEER_EXPERTS = PEER_N_KEYS * PEER_N_KEYS

LANE = 128
HEAD_PAD = 128

VMEM_LIMIT = 56 * 1024 * 1024

_C_CQ = 0
_C_CKV = _C_CQ + MLA_Q_RANK
_C_KRP = _C_CKV + MLA_KV_RANK
_C_KRS = _C_KRP + HEAD_PAD
_C_SSM = _C_KRS + HEAD_PAD
_C_QSB = _C_SSM + SSM_WIDTH
_C_KSB = _C_QSB + SB_HEADS * HEAD_PAD
_C_VSB = _C_KSB + SB_HEADS * HEAD_PAD
_C_END = _C_VSB + SB_WIDTH
MLA_QK_W = MLA_HEADS * HEAD_PAD
MLA_V_W = MLA_HEADS * MLA_V


def _params(*sem):
    return pltpu.CompilerParams(dimension_semantics=sem, vmem_limit_bytes=VMEM_LIMIT)


def _dot(a, b):
    return jnp.dot(a, b, preferred_element_type=F32)


def _dot_nt(a, b):
    return lax.dot_general(a, b, (((1,), (1,)), ((), ())), preferred_element_type=F32)


def _layer_norm_rows(v, g, b):
    mu = jnp.mean(v, axis=-1, keepdims=True)
    c = v - mu
    var = jnp.mean(c * c, axis=-1, keepdims=True)
    return c * lax.rsqrt(var + NORM_EPS) * g + b


def _rms_rows(v, width):
    return v * lax.rsqrt(jnp.sum(v * v, axis=-1, keepdims=True) * (1.0 / width) + NORM_EPS)


def _rope_kernel(pos_ref, freq_ref, cos_ref, sin_ref):
    ang = pos_ref[0].astype(F32) * freq_ref[...]
    lane = lax.broadcasted_iota(jnp.int32, ang.shape, 1)
    rot = (lane >= MLA_NOPE) & (lane < MLA_NOPE + MLA_ROPE)
    cos_ref[0] = jnp.where(rot, jnp.cos(ang), jnp.where(lane < MLA_NOPE, 1.0, 0.0))
    sin_ref[0] = jnp.where(rot, jnp.sin(ang), 0.0)


def _rope_tables(positions):
    bsz, seq = positions.shape
    ts = min(seq, 512)
    half = MLA_ROPE // 2
    freq = ROPE_THETA ** (-jnp.arange(half, dtype=F32) / half)
    freq_row = jnp.zeros((1, HEAD_PAD), F32).at[0, MLA_NOPE:MLA_NOPE + MLA_ROPE].set(jnp.tile(freq, 2))
    return pl.pallas_call(
        _rope_kernel,
        grid=(bsz, seq // ts),
        in_specs=[pl.BlockSpec((1, ts, 1), lambda b, s: (b, s, 0)),
                  pl.BlockSpec((1, HEAD_PAD), lambda b, s: (0, 0))],
        out_specs=[pl.BlockSpec((1, ts, HEAD_PAD), lambda b, s: (b, s, 0))] * 2,
        out_shape=[jax.ShapeDtypeStruct((bsz, seq, HEAD_PAD), F32)] * 2,
        compiler_params=_params("parallel", "parallel"),
        name="rope_tables",
    )(positions.reshape(bsz, seq, 1), freq_row)


def _inproj_kernel(x_ref, win_ref, wuq_ref, wukv_ref, cos_ref, sin_ref,
                   q_ref, k_ref, v_ref, u_ref, qs_ref, ks_ref, vs_ref):
    xb = x_ref[0].astype(BF16)
    acc = _dot(xb, win_ref[...])
    cos = cos_ref[0]
    sin = sin_ref[0]
    cqn = _rms_rows(acc[:, _C_CQ:_C_CKV], MLA_Q_RANK).astype(BF16)
    qq = _dot(cqn, wuq_ref[...])
    ckvn = _rms_rows(acc[:, _C_CKV:_C_KRP], MLA_KV_RANK).astype(BF16)
    kv = _dot(ckvn, wukv_ref[...])
    k_rope = acc[:, _C_KRP:_C_KRS] * cos + acc[:, _C_KRS:_C_SSM] * sin
    for h in range(MLA_HEADS):
        lo, hi = h * HEAD_PAD, (h + 1) * HEAD_PAD
        q_ref[0, :, lo:hi] = (qq[:, lo:hi] * cos + qq[:, MLA_QK_W + lo:MLA_QK_W + hi] * sin).astype(BF16)
        k_ref[0, :, lo:hi] = (kv[:, lo:hi] + k_rope).astype(BF16)
    v_ref[0] = kv[:, MLA_QK_W:].astype(BF16)
    u_ref[...] = acc[:, _C_SSM:_C_QSB].astype(BF16)
    qs_ref[0] = acc[:, _C_QSB:_C_KSB].astype(BF16)
    ks_ref[0] = acc[:, _C_KSB:_C_VSB].astype(BF16)
    vs_ref[0] = acc[:, _C_VSB:_C_END].astype(BF16)


def _inproj(x, win, wuq, wukv, cos_t, sin_t):
    bsz, seq, _ = x.shape
    ts = min(seq, 512)
    tok = lambda w: pl.BlockSpec((1, ts, w), lambda b, s: (b, s, 0))
    full = lambda a: pl.BlockSpec(a.shape, lambda b, s: (0,) * a.ndim)
    sb_w = SB_HEADS * HEAD_PAD
    return pl.pallas_call(
        _inproj_kernel,
        grid=(bsz, seq // ts),
        in_specs=[tok(D_MODEL), full(win), full(wuq), full(wukv), tok(HEAD_PAD), tok(HEAD_PAD)],
        out_specs=[tok(MLA_QK_W), tok(MLA_QK_W), tok(MLA_V_W),
                   pl.BlockSpec((ts, SSM_WIDTH), lambda b, s: (s, b)),
                   tok(sb_w), tok(sb_w), tok(SB_WIDTH)],
        out_shape=[jax.ShapeDtypeStruct((bsz, seq, MLA_QK_W), BF16),
                   jax.ShapeDtypeStruct((bsz, seq, MLA_QK_W), BF16),
                   jax.ShapeDtypeStruct((bsz, seq, MLA_V_W), BF16),
                   jax.ShapeDtypeStruct((seq, bsz * SSM_WIDTH), BF16),
                   jax.ShapeDtypeStruct((bsz, seq, sb_w), BF16),
                   jax.ShapeDtypeStruct((bsz, seq, sb_w), BF16),
                   jax.ShapeDtypeStruct((bsz, seq, SB_WIDTH), BF16)],
        compiler_params=_params("parallel", "parallel"),
        name="inproj",
    )(x, win, wuq, wukv, cos_t, sin_t)


def _mla_kernel(q_ref, k_ref, v_ref, o_ref, *, tq):
    qi = pl.program_id(1)
    row_chunk = lax.broadcasted_iota(jnp.int32, (tq, tq), 0) // CHUNK
    col_chunk = lax.broadcasted_iota(jnp.int32, (tq, tq), 1) // CHUNK
    allowed = col_chunk <= row_chunk
    lane = lax.broadcasted_iota(jnp.int32, (tq, HEAD_PAD), 1)

    def head(h, pair):
        q = q_ref[0, :, h * HEAD_PAD:(h + 1) * HEAD_PAD]

        def block(kb, carry, masked):
            m, l, acc = carry
            ks = pl.multiple_of(kb * tq, tq)
            k = k_ref[0, pl.ds(ks, tq), h * HEAD_PAD:(h + 1) * HEAD_PAD]
            v = v_ref[0, pl.ds(ks, tq), pair * HEAD_PAD:(pair + 1) * HEAD_PAD]
            s = _dot_nt(q, k)
            if masked:
                s = jnp.where(allowed, s, NEG_BIG)
            m_new = jnp.maximum(m, jnp.max(s, axis=-1, keepdims=True))
            alpha = jnp.exp(m - m_new)
            p = jnp.exp(s - m_new)
            l = alpha * l + jnp.sum(p, axis=-1, keepdims=True)
            acc = alpha * acc + _dot(p.astype(BF16), v)
            return m_new, l, acc

        init = (jnp.full((tq, 1), NEG_BIG, F32), jnp.zeros((tq, 1), F32), jnp.zeros((tq, HEAD_PAD), F32))
        carry = lax.fori_loop(0, qi, lambda kb, c: block(kb, c, False), init)
        _, l, acc = block(qi, carry, True)
        return acc / l

    for pair in range(MLA_HEADS // 2):
        o_ref[0, :, pair * HEAD_PAD:(pair + 1) * HEAD_PAD] = jnp.where(
            lane < MLA_V, head(2 * pair, pair), head(2 * pair + 1, pair)).astype(BF16)


def _mla_attention(q, k, v):
    bsz, seq, _ = q.shape
    tq = min(seq, 256)
    return pl.pallas_call(
        functools.partial(_mla_kernel, tq=tq),
        grid=(bsz, seq // tq),
        in_specs=[pl.BlockSpec((1, tq, MLA_QK_W), lambda b, i: (b, i, 0)),
                  pl.BlockSpec((1, seq, MLA_QK_W), lambda b, i: (b, 0, 0)),
                  pl.BlockSpec((1, seq, MLA_V_W), lambda b, i: (b, 0, 0))],
        out_specs=pl.BlockSpec((1, tq, MLA_V_W), lambda b, i: (b, i, 0)),
        out_shape=jax.ShapeDtypeStruct((bsz, seq, MLA_V_W), BF16),
        compiler_params=_params("parallel", "arbitrary"),
        name="mla_attention",
    )(q, k, v)


def _sb_kernel(q_ref, k_ref, v_ref, tri_ref, o_ref, *, tq):
    qi = pl.program_id(1)
    earlier = lax.broadcasted_iota(jnp.int32, (tq, tq), 1) < lax.broadcasted_iota(jnp.int32, (tq, tq), 0)
    lane = lax.broadcasted_iota(jnp.int32, (tq, HEAD_PAD), 1)
    tri = tri_ref[...]

    def head(h, pair):
        q = q_ref[0, :, h * HEAD_PAD:(h + 1) * HEAD_PAD]

        def block(kb, carry, masked):
            right, acc = carry
            ks = pl.multiple_of(kb * tq, tq)
            k = k_ref[0, pl.ds(ks, tq), h * HEAD_PAD:(h + 1) * HEAD_PAD]
            v = v_ref[0, pl.ds(ks, tq), pair * HEAD_PAD:(pair + 1) * HEAD_PAD]
            z = _dot_nt(q, k)
            t = jnp.log(1.0 + jnp.exp(-jnp.abs(z)))
            log_fail = -(jnp.maximum(z, 0.0) + t)
            log_hit = jnp.minimum(z, 0.0) - t
            if masked:
                log_fail = jnp.where(earlier, log_fail, 0.0)
            hi = log_fail.astype(BF16)
            lo = (log_fail - hi.astype(F32)).astype(BF16)
            between = _dot(hi, tri) + _dot(lo, tri) + right
            w = jnp.exp(log_hit + between)
            if masked:
                w = jnp.where(earlier, w, 0.0)
            acc = acc + _dot(w.astype(BF16), v)
            right = right + jnp.sum(log_fail, axis=-1, keepdims=True)
            return right, acc

        init = (jnp.zeros((tq, 1), F32), jnp.zeros((tq, HEAD_PAD), F32))
        carry = block(qi, init, True)
        _, acc = lax.fori_loop(0, qi, lambda i, c: block(qi - 1 - i, c, False), carry)
        return acc

    for pair in range(SB_HEADS // 2):
        o_ref[0, :, pair * HEAD_PAD:(pair + 1) * HEAD_PAD] = jnp.where(
            lane < SB_DIM, head(2 * pair, pair), head(2 * pair + 1, pair)).astype(BF16)


def _sb_attention(q, k, v):
    bsz, seq, _ = q.shape
    tq = min(seq, 256)
    sb_w = SB_HEADS * HEAD_PAD
    tri = (lax.broadcasted_iota(jnp.int32, (tq, tq), 0) > lax.broadcasted_iota(jnp.int32, (tq, tq), 1)).astype(BF16)
    return pl.pallas_call(
        functools.partial(_sb_kernel, tq=tq),
        grid=(bsz, seq // tq),
        in_specs=[pl.BlockSpec((1, tq, sb_w), lambda b, i: (b, i, 0)),
                  pl.BlockSpec((1, seq, sb_w), lambda b, i: (b, 0, 0)),
                  pl.BlockSpec((1, seq, SB_WIDTH), lambda b, i: (b, 0, 0)),
                  pl.BlockSpec((tq, tq), lambda b, i: (0, 0))],
        out_specs=pl.BlockSpec((1, tq, SB_WIDTH), lambda b, i: (b, i, 0)),
        out_shape=jax.ShapeDtypeStruct((bsz, seq, SB_WIDTH), BF16),
        compiler_params=_params("parallel", "arbitrary"),
        name="sb_attention",
    )(q, k, v, tri)


def _ssm_kernel(u_ref, bbd_ref, lam_ref, cbd_ref, d_ref, wglu_ref, bglu_ref, y_ref, h_ref, hs_ref, *, tt, nb):
    @pl.when(pl.program_id(0) == 0)
    def _():
        h_ref[...] = jnp.zeros_like(h_ref)

    u = u_ref[...]
    hs_ref[...] = _dot(u, bbd_ref[...])
    lam_re = jnp.broadcast_to(lam_ref[0:1, :], (nb, SSM_STATES))
    lam_im = jnp.broadcast_to(lam_ref[1:2, :], (nb, SSM_STATES))

    def step(t, carry):
        h_re, h_im = carry
        r0 = pl.multiple_of(t * nb, nb)
        n_re = lam_re * h_re - lam_im * h_im + hs_ref[pl.ds(r0, nb), 0:SSM_STATES]
        n_im = lam_re * h_im + lam_im * h_re + hs_ref[pl.ds(r0, nb), SSM_STATES:]
        hs_ref[pl.ds(r0, nb), 0:SSM_STATES] = n_re
        hs_ref[pl.ds(r0, nb), SSM_STATES:] = n_im
        return n_re, n_im

    h_re, h_im = lax.fori_loop(0, tt, step, (h_ref[:, 0:SSM_STATES], h_ref[:, SSM_STATES:]))
    h_ref[:, 0:SSM_STATES] = h_re
    h_ref[:, SSM_STATES:] = h_im

    y = _dot(hs_ref[...].astype(BF16), cbd_ref[...]) + d_ref[...] * u.astype(F32)
    y = jax.nn.gelu(y)
    y = y * jax.nn.sigmoid(_dot(y.astype(BF16), wglu_ref[...]) + bglu_ref[...])
    y_ref[...] = y.astype(BF16)


def _ssm(u_tm, bbd, lam, cbd, d_row, wglu, bglu_row, nb):
    rows = u_tm.shape[0]
    seq = rows // nb
    tt = min(seq, 128)
    full = lambda a: pl.BlockSpec(a.shape, lambda i: (0,) * a.ndim)
    return pl.pallas_call(
        functools.partial(_ssm_kernel, tt=tt, nb=nb),
        grid=(seq // tt,),
        in_specs=[pl.BlockSpec((tt * nb, SSM_WIDTH), lambda i: (i, 0)),
                  full(bbd), full(lam), full(cbd), full(d_row), full(wglu), full(bglu_row)],
        out_specs=pl.BlockSpec((tt * nb, SSM_WIDTH), lambda i: (i, 0)),
        out_shape=jax.ShapeDtypeStruct((rows, SSM_WIDTH), BF16),
        scratch_shapes=[pltpu.VMEM((nb, 2 * SSM_STATES), F32),
                        pltpu.VMEM((tt * nb, 2 * SSM_STATES), F32)],
        compiler_params=_params("arbitrary"),
        name="ssm",
    )(u_tm, bbd, lam, cbd, d_row, wglu, bglu_row)


def _memkv_kernel(mem_ref, w_ref, k_ref, v_ref):
    kv = _dot(mem_ref[0].astype(BF16), w_ref[...])
    k_ref[0] = kv[:, :D_MODEL].astype(BF16)
    v_ref[0] = kv[:, D_MODEL:].astype(BF16)


def _memkv(mem, w_mkv):
    bsz, mlen, _ = mem.shape
    return pl.pallas_call(
        _memkv_kernel,
        grid=(bsz,),
        in_specs=[pl.BlockSpec((1, mlen, D_MODEL), lambda b: (b, 0, 0)),
                  pl.BlockSpec(w_mkv.shape, lambda b: (0, 0))],
        out_specs=[pl.BlockSpec((1, mlen, D_MODEL), lambda b: (b, 0, 0))] * 2,
        out_shape=[jax.ShapeDtypeStruct((bsz, mlen, D_MODEL), BF16)] * 2,
        compiler_params=_params("parallel"),
        name="mem_kv",
    )(mem, w_mkv)


def _mix_mem_kernel(ya_ref, ys_ref, yb_ref, x_ref, gmix_ref, wout_ref, ln1_ref,
                    mk_ref, mv_ref, wmq_ref, wmo_ref, ln2_ref, xt_ref, *, alpha):
    gm = gmix_ref[...]
    ya = (_rms_rows(ya_ref[0].astype(F32), MLA_V_W) * gm[:, 0:MLA_V_W]).astype(BF16)
    ys = (_rms_rows(ys_ref[...].astype(F32), SSM_WIDTH) * gm[:, MLA_V_W:MLA_V_W + SSM_WIDTH]).astype(BF16)
    yb = (_rms_rows(yb_ref[0].astype(F32), SB_WIDTH) * gm[:, MLA_V_W + SSM_WIDTH:]).astype(BF16)
    mix = (_dot(ya, wout_ref[0:MLA_V_W, :]) + _dot(ys, wout_ref[MLA_V_W:MLA_V_W + SSM_WIDTH, :])
           + _dot(yb, wout_ref[MLA_V_W + SSM_WIDTH:, :]))
    x1 = _layer_norm_rows(alpha * x_ref[0] + mix, ln1_ref[0:1, :], ln1_ref[1:2, :])

    q = _dot(x1.astype(BF16), wmq_ref[...]).astype(BF16)
    heads = []
    for h in range(MEM_HEADS):
        lo, hi = h * MEM_HEAD_DIM, (h + 1) * MEM_HEAD_DIM
        s = _dot_nt(q[:, lo:hi], mk_ref[0, :, lo:hi])
        p = jnp.exp(s - jnp.max(s, axis=-1, keepdims=True))
        p = p / jnp.sum(p, axis=-1, keepdims=True)
        heads.append(_dot(p.astype(BF16), mv_ref[0, :, lo:hi]).astype(BF16))
    o = jnp.concatenate(heads, axis=-1)
    x2 = _layer_norm_rows(alpha * x1 + _dot(o, wmo_ref[...]), ln2_ref[0:1, :], ln2_ref[1:2, :])
    xt_ref[...] = x2.T


def _mix_mem(ya, ys_tm, yb, x, gmix, wout, ln1, mk, mv, wmq, wmo, ln2, alpha):
    bsz, seq, _ = x.shape
    ts = min(seq, 256)
    nst = seq // ts
    mlen = mk.shape[1]
    tok = lambda w: pl.BlockSpec((1, ts, w), lambda b, s: (b, s, 0))
    full = lambda a: pl.BlockSpec(a.shape, lambda b, s: (0,) * a.ndim)
    return pl.pallas_call(
        functools.partial(_mix_mem_kernel, alpha=alpha),
        grid=(bsz, nst),
        in_specs=[tok(MLA_V_W), pl.BlockSpec((ts, SSM_WIDTH), lambda b, s: (s, b)), tok(SB_WIDTH), tok(D_MODEL),
                  full(gmix), full(wout), full(ln1),
                  pl.BlockSpec((1, mlen, D_MODEL), lambda b, s: (b, 0, 0)),
                  pl.BlockSpec((1, mlen, D_MODEL), lambda b, s: (b, 0, 0)),
                  full(wmq), full(wmo), full(ln2)],
        out_specs=pl.BlockSpec((D_MODEL, ts), lambda b, s: (0, b * nst + s)),
        out_shape=jax.ShapeDtypeStruct((D_MODEL, bsz * seq), F32),
        compiler_params=_params("parallel", "parallel"),
        name="mix_mem",
    )(ya, ys_tm, yb, x, gmix, wout, ln1, mk, mv, wmq, wmo, ln2)


def _top16(s, vals_ref):
    rank = jnp.full(s.shape, float(PEER_TOPK), F32)
    work = s
    for k in range(PEER_TOPK):
        m = jnp.max(work, axis=0, keepdims=True)
        hit = work == m
        rank = jnp.where(hit, float(k), rank)
        work = jnp.where(hit, -jnp.inf, work)
        vals_ref[k:k + 1, :] = m
    return rank


def _peer_route_kernel(xt_ref, wpq_ref, keys_ref, a_ref, n_ref, b_ref, r_ref, v1_ref, v2_ref, cand_ref):
    xb = xt_ref[...].astype(BF16)
    q = _dot(wpq_ref[...], xb).astype(BF16)
    s1 = _dot(keys_ref[0, 0], q[0:PEER_KEY_DIM, :])
    s2 = _dot(keys_ref[0, 1], q[PEER_KEY_DIM:, :])
    rank1 = _top16(s1, v1_ref)
    rank2 = _top16(s2, v2_ref)
    v2 = v2_ref[...]
    for k in range(PEER_TOPK):
        cand_ref[k * PEER_TOPK:(k + 1) * PEER_TOPK, :] = v1_ref[k:k + 1, :] + v2
    cand = cand_ref[...]
    work = cand
    for _ in range(PEER_TOPK - 1):
        m = jnp.max(work, axis=0, keepdims=True)
        work = jnp.where(work == m, -jnp.inf, work)
    thr = jnp.max(work, axis=0, keepdims=True)
    keep = cand >= thr
    cmax = v1_ref[0:1, :] + v2_ref[0:1, :]
    z = jnp.sum(jnp.where(keep, jnp.exp(cand - cmax), 0.0), axis=0, keepdims=True)
    keep_f = jnp.where(keep, 1.0, 0.0)
    n = jnp.zeros(rank1.shape, F32)
    for k in range(PEER_TOPK):
        l_k = jnp.sum(keep_f[k * PEER_TOPK:(k + 1) * PEER_TOPK, :], axis=0, keepdims=True)
        n = jnp.where(rank1 == float(k), l_k, n)
    a = jnp.where(rank1 < float(PEER_TOPK), jnp.exp(s1 - v1_ref[0:1, :]), 0.0)
    b = jnp.where(rank2 < float(PEER_TOPK), jnp.exp(s2 - v2_ref[0:1, :]), 0.0) / z
    tt = a.shape[-1]
    a_ref[:, 0] = a.reshape(PEER_N_KEYS // 8, 8, tt)
    n_ref[:, 0] = n.reshape(PEER_N_KEYS // 8, 8, tt)
    b_ref[0] = b.astype(BF16)
    r_ref[0] = rank2.astype(BF16)


def _peer_route(xt, wpq_t, keys):
    ntok = xt.shape[1]
    tt = min(ntok, 512)
    nblk = PEER_N_KEYS // 8
    return pl.pallas_call(
        _peer_route_kernel,
        grid=(ntok // tt, PEER_HEADS),
        in_specs=[pl.BlockSpec((D_MODEL, tt), lambda i, h: (0, i)),
                  pl.BlockSpec((2 * PEER_KEY_DIM, D_MODEL), lambda i, h: (h, 0)),
                  pl.BlockSpec((1, 2, PEER_N_KEYS, PEER_KEY_DIM), lambda i, h: (h, 0, 0, 0))],
        out_specs=[pl.BlockSpec((nblk, 1, 8, tt), lambda i, h: (0, h, 0, i)),
                   pl.BlockSpec((nblk, 1, 8, tt), lambda i, h: (0, h, 0, i)),
                   pl.BlockSpec((1, PEER_N_KEYS, tt), lambda i, h: (h, 0, i)),
                   pl.BlockSpec((1, PEER_N_KEYS, tt), lambda i, h: (h, 0, i))],
        out_shape=[jax.ShapeDtypeStruct((nblk, PEER_HEADS, 8, ntok), F32),
                   jax.ShapeDtypeStruct((nblk, PEER_HEADS, 8, ntok), F32),
                   jax.ShapeDtypeStruct((PEER_HEADS, PEER_N_KEYS, ntok), BF16),
                   jax.ShapeDtypeStruct((PEER_HEADS, PEER_N_KEYS, ntok), BF16)],
        scratch_shapes=[pltpu.VMEM((PEER_TOPK, tt), F32), pltpu.VMEM((PEER_TOPK, tt), F32),
                        pltpu.VMEM((PEER_TOPK * PEER_TOPK, tt), F32)],
        compiler_params=_params("parallel", "arbitrary"),
        name="peer_route",
    )(xt, wpq_t, keys)


PEER_I1_PER_TILE = 8
PEER_TILE = PEER_I1_PER_TILE * PEER_N_KEYS


def _peer_dense_kernel(xt_ref, a_ref, n_ref, b_ref, r_ref, u_ref, vt_ref, ln_ref, o_ref,
                       acc_ref, xb_ref, gh_ref, *, alpha):
    e = pl.program_id(1)

    @pl.when(e == 0)
    def _():
        acc_ref[...] = jnp.zeros_like(acc_ref)
        xb_ref[...] = xt_ref[...].astype(BF16)

    tt = xb_ref.shape[1]
    hidden = _dot(u_ref[...], xb_ref[...])
    for j in range(PEER_I1_PER_TILE):
        gate = jnp.zeros((PEER_N_KEYS, tt), BF16)
        for h in range(PEER_HEADS):
            a_row = jnp.broadcast_to(a_ref[0, h, j:j + 1, :].astype(BF16), (PEER_N_KEYS, tt))
            n_row = jnp.broadcast_to(n_ref[0, h, j:j + 1, :].astype(BF16), (PEER_N_KEYS, tt))
            gate = gate + jnp.where(r_ref[h] < n_row, a_row * b_ref[h], jnp.zeros((), BF16))
        act = jax.nn.gelu(hidden[j * PEER_N_KEYS:(j + 1) * PEER_N_KEYS, :])
        gh_ref[j * PEER_N_KEYS:(j + 1) * PEER_N_KEYS, :] = act.astype(BF16) * gate
    acc_ref[...] += _dot(vt_ref[...], gh_ref[...])

    @pl.when(e == pl.num_programs(1) - 1)
    def _():
        y = alpha * xt_ref[...] + acc_ref[...]
        mu = jnp.mean(y, axis=0, keepdims=True)
        c = y - mu
        var = jnp.mean(c * c, axis=0, keepdims=True)
        o_ref[...] = (c * lax.rsqrt(var + NORM_EPS)).T * ln_ref[0:1, :] + ln_ref[1:2, :]


def _peer_dense(xt, a, n, b, r, u, vt, ln, alpha):
    ntok = xt.shape[1]
    tt = min(ntok, 1024)
    ntile = PEER_EXPERTS // PEER_TILE
    return pl.pallas_call(
        functools.partial(_peer_dense_kernel, alpha=alpha),
        grid=(ntok // tt, ntile),
        in_specs=[pl.BlockSpec((D_MODEL, tt), lambda i, e: (0, i)),
                  pl.BlockSpec((1, PEER_HEADS, PEER_I1_PER_TILE, tt), lambda i, e: (e, 0, 0, i)),
                  pl.BlockSpec((1, PEER_HEADS, PEER_I1_PER_TILE, tt), lambda i, e: (e, 0, 0, i)),
                  pl.BlockSpec((PEER_HEADS, PEER_N_KEYS, tt), lambda i, e: (0, 0, i)),
                  pl.BlockSpec((PEER_HEADS, PEER_N_KEYS, tt), lambda i, e: (0, 0, i)),
                  pl.BlockSpec((PEER_TILE, D_MODEL), lambda i, e: (e, 0)),
                  pl.BlockSpec((D_MODEL, PEER_TILE), lambda i, e: (0, e)),
                  pl.BlockSpec(ln.shape, lambda i, e: (0, 0))],
        out_specs=pl.BlockSpec((tt, D_MODEL), lambda i, e: (i, 0)),
        out_shape=jax.ShapeDtypeStruct((ntok, D_MODEL), F32),
        scratch_shapes=[pltpu.VMEM((D_MODEL, tt), F32), pltpu.VMEM((D_MODEL, tt), BF16),
                        pltpu.VMEM((PEER_TILE, tt), BF16)],
        compiler_params=_params("parallel", "arbitrary"),
        name="peer_dense",
    )(xt, a, n, b, r, u, vt, ln)


def _pad_heads(w, heads, width):
    rows = w.shape[0]
    w = w.reshape(rows, heads, width)
    return jnp.pad(w, ((0, 0), (0, 0), (0, HEAD_PAD - width))).reshape(rows, heads * HEAD_PAD)


def _rotate_half_cols(w, heads, width, nope):
    rows = w.shape[0]
    w = w.reshape(rows, heads, width)
    half = (width - nope) // 2
    x1, x2 = w[..., nope:nope + half], w[..., nope + half:]
    out = jnp.concatenate([jnp.zeros_like(w[..., :nope]), -x2, x1], axis=-1)
    return out.reshape(rows, heads * width)


def _pack_inproj(w_in, g_cq, g_ckv, w_uq, w_ukv):
    c = 0
    cols = {}
    for name, width in (("cq", MLA_Q_RANK), ("ckv", MLA_KV_RANK), ("kr", MLA_ROPE), ("ssm", SSM_WIDTH),
                        ("qsb", SB_WIDTH), ("ksb", SB_WIDTH), ("vsb", SB_WIDTH)):
        cols[name] = w_in[:, c:c + width]
        c += width
    rows = w_in.shape[0]
    zeros = lambda n: jnp.zeros((rows, n), F32)
    kr = cols["kr"]
    half = MLA_ROPE // 2
    kr_plain = jnp.concatenate([zeros(MLA_NOPE), kr, zeros(HEAD_PAD - MLA_NOPE - MLA_ROPE)], axis=1)
    kr_swap = jnp.concatenate([zeros(MLA_NOPE), -kr[:, half:], kr[:, :half],
                               zeros(HEAD_PAD - MLA_NOPE - MLA_ROPE)], axis=1)
    sb_scale = SB_DIM ** -0.5
    win = jnp.concatenate([cols["cq"], cols["ckv"], kr_plain, kr_swap, cols["ssm"],
                           _pad_heads(cols["qsb"] * sb_scale, SB_HEADS, SB_DIM),
                           _pad_heads(cols["ksb"], SB_HEADS, SB_DIM), cols["vsb"]], axis=1).astype(BF16)

    qk_dim = MLA_NOPE + MLA_ROPE
    wq = w_uq * (g_cq[:, None] * qk_dim ** -0.5)
    wuq = jnp.concatenate([_pad_heads(wq, MLA_HEADS, qk_dim),
                           _pad_heads(_rotate_half_cols(wq, MLA_HEADS, qk_dim, MLA_NOPE), MLA_HEADS, qk_dim)],
                          axis=1).astype(BF16)
    wkv = (w_ukv * g_ckv[:, None]).reshape(MLA_KV_RANK, MLA_HEADS, MLA_NOPE + MLA_V)
    wukv = jnp.concatenate([_pad_heads(wkv[..., :MLA_NOPE].reshape(MLA_KV_RANK, -1), MLA_HEADS, MLA_NOPE),
                            wkv[..., MLA_NOPE:].reshape(MLA_KV_RANK, -1)], axis=1).astype(BF16)
    return win, wuq, wukv


def _pack_ssm(lam_re, lam_im, log_step, b_re, b_im, c_re, c_im, d_skip):
    step = jnp.exp(log_step)[:, None]
    decay = jnp.exp(lam_re * step)
    ab_re, ab_im = decay * jnp.cos(lam_im * step), decay * jnp.sin(lam_im * step)
    inv = 1.0 / (lam_re * lam_re + lam_im * lam_im)
    f_re = ((ab_re - 1.0) * lam_re + ab_im * lam_im) * inv
    f_im = (ab_im * lam_re - (ab_re - 1.0) * lam_im) * inv
    bb_re = f_re[..., None] * b_re - f_im[..., None] * b_im
    bb_im = f_re[..., None] * b_im + f_im[..., None] * b_re
    eye = jnp.eye(SSM_GROUPS, dtype=F32)
    blk = lambda w: jnp.einsum("gph,gk->ghkp", w, eye).reshape(SSM_WIDTH, SSM_STATES)
    bbd = jnp.concatenate([blk(bb_re), blk(bb_im)], axis=1).astype(BF16)
    blk_c = lambda w: jnp.einsum("ghp,gk->gpkh", w, eye).reshape(SSM_STATES, SSM_WIDTH)
    cbd = jnp.concatenate([blk_c(c_re), blk_c(-c_im)], axis=0).astype(BF16)
    lam = jnp.stack([ab_re.reshape(-1), ab_im.reshape(-1)])
    return bbd, lam, cbd, d_skip.reshape(1, SSM_WIDTH)


def kernel(x, mem, positions, w_in, g_cq, g_ckv, w_uq, w_ukv, ssm_lam_re, ssm_lam_im, ssm_log_step, ssm_b_re, ssm_b_im, ssm_c_re, ssm_c_im, ssm_d, w_glu, b_glu, g_mix, w_out, ln_mix_g, ln_mix_b, w_mq, w_mkv, w_mo, ln_mem_g, ln_mem_b, w_pq, peer_sub_keys, peer_u, peer_v, ln_ffn_g, ln_ffn_b):
    bsz, seq, _ = x.shape
    depth = w_in.shape[0]
    alpha = (2 * depth) ** 0.25
    cos_t, sin_t = _rope_tables(positions)
    for l in range(depth):
        win, wuq, wukv = _pack_inproj(w_in[l], g_cq[l], g_ckv[l], w_uq[l], w_ukv[l])
        q, k, v, u_tm, q_sb, k_sb, v_sb = _inproj(x, win, wuq, wukv, cos_t, sin_t)
        y_mla = _mla_attention(q, k, v)
        y_sb = _sb_attention(q_sb, k_sb, v_sb)
        bbd, lam, cbd, d_row = _pack_ssm(ssm_lam_re[l], ssm_lam_im[l], ssm_log_step[l], ssm_b_re[l], ssm_b_im[l],
                                         ssm_c_re[l], ssm_c_im[l], ssm_d[l])
        y_ssm = _ssm(u_tm.reshape(seq * bsz, SSM_WIDTH), bbd, lam, cbd, d_row,
                     w_glu[l].astype(BF16), b_glu[l].reshape(1, SSM_WIDTH), bsz)
        mk, mv = _memkv(mem, w_mkv[l].astype(BF16))
        xt = _mix_mem(y_mla, y_ssm.reshape(seq, bsz * SSM_WIDTH), y_sb, x,
                      g_mix[l].reshape(1, -1), w_out[l].astype(BF16), jnp.stack([ln_mix_g[l], ln_mix_b[l]]),
                      mk, mv, (w_mq[l] * MEM_HEAD_DIM ** -0.5).astype(BF16), w_mo[l].astype(BF16),
                      jnp.stack([ln_mem_g[l], ln_mem_b[l]]), alpha)
        a, n, b, r = _peer_route(xt, w_pq[l].T.astype(BF16), peer_sub_keys[l].astype(BF16))
        x = _peer_dense(xt, a, n, b, r, peer_u[l].astype(BF16), peer_v[l].T.astype(BF16),
                        jnp.stack([ln_ffn_g[l], ln_ffn_b[l]]), alpha).reshape(bsz, seq, D_MODEL)
    return x
```

```python
import functools
import math

import jax
import jax.numpy as jnp
from jax import lax
from jax.experimental import pallas as pl
from jax.experimental.pallas import tpu as pltpu

F32 = jnp.float32
BF16 = jnp.bfloat16

D_MODEL = 1024
CHUNK = 64
NORM_EPS = 1e-5
NEG_BIG = -1e30
LOG2E = math.log2(math.e)

MLA_HEADS = 6
MLA_NOPE = 64
MLA_ROPE = 32
MLA_V = 64
MLA_Q_RANK = 256
MLA_KV_RANK = 128
ROPE_THETA = 10000.0

SSM_GROUPS = 24
SSM_CH = 16
SSM_STATE = 64
SSM_WIDTH = SSM_GROUPS * SSM_CH
SSM_STATES = SSM_GROUPS * SSM_STATE

SB_HEADS = 4
SB_DIM = 64
SB_WIDTH = SB_HEADS * SB_DIM

MEM_HEADS = 4
MEM_HEAD_DIM = D_MODEL // MEM_HEADS

PEER_HEADS = 8
PEER_N_KEYS = 128
PEER_TOPK = 16
PEER_KEY_DIM = 128
PEER_EXPERTS = PEER_N_KEYS * PEER_N_KEYS

LANE = 128
BF16_SUBLANES = 16
HEAD_PAD = 128

VMEM_LIMIT = 56 * 1024 * 1024

_C_CQ = 0
_C_CKV = _C_CQ + MLA_Q_RANK
_C_KRP = _C_CKV + MLA_KV_RANK
_C_KRS = _C_KRP + HEAD_PAD
_C_SSM = _C_KRS + HEAD_PAD
_C_QSB = _C_SSM + SSM_WIDTH
_C_KSB = _C_QSB + SB_HEADS * HEAD_PAD
_C_VSB = _C_KSB + SB_HEADS * HEAD_PAD
_C_END = _C_VSB + SB_WIDTH
MLA_QK_W = MLA_HEADS * HEAD_PAD
MLA_V_W = MLA_HEADS * MLA_V


def _params(*sem):
    return pltpu.CompilerParams(dimension_semantics=sem, vmem_limit_bytes=VMEM_LIMIT)


def _dot(a, b):
    return jnp.dot(a, b, preferred_element_type=F32)


def _dot_nt(a, b):
    return lax.dot_general(a, b, (((1,), (1,)), ((), ())), preferred_element_type=F32)


def _gelu_tanh(x):
    c = math.sqrt(2.0 / math.pi)
    inner = x * (c + (c * 0.044715) * (x * x))
    return (0.5 * x) * (1.0 + jnp.tanh(inner))


def _layer_norm_rows(v, g, b):
    mu = jnp.mean(v, axis=-1, keepdims=True)
    c = v - mu
    var = jnp.mean(c * c, axis=-1, keepdims=True)
    return c * lax.rsqrt(var + NORM_EPS) * g + b


def _rms_rows(v, width):
    return v * lax.rsqrt(jnp.sum(v * v, axis=-1, keepdims=True) * (1.0 / width) + NORM_EPS)


def _rope_kernel(pos_ref, freq_ref, cos_ref, sin_ref):
    ang = pos_ref[0].astype(F32) * freq_ref[...]
    lane = lax.broadcasted_iota(jnp.int32, ang.shape, 1)
    rot = (lane >= MLA_NOPE) & (lane < MLA_NOPE + MLA_ROPE)
    cos_ref[0] = jnp.where(rot, jnp.cos(ang), jnp.where(lane < MLA_NOPE, 1.0, 0.0))
    sin_ref[0] = jnp.where(rot, jnp.sin(ang), 0.0)


def _rope_tables(positions):
    bsz, seq = positions.shape
    ts = min(seq, 512)
    half = MLA_ROPE // 2
    freq = ROPE_THETA ** (-jnp.arange(half, dtype=F32) / half)
    freq_row = jnp.zeros((1, HEAD_PAD), F32).at[0, MLA_NOPE:MLA_NOPE + MLA_ROPE].set(jnp.tile(freq, 2))
    return pl.pallas_call(
        _rope_kernel,
        grid=(bsz, seq // ts),
        in_specs=[pl.BlockSpec((1, ts, 1), lambda b, s: (b, s, 0)),
                  pl.BlockSpec((1, HEAD_PAD), lambda b, s: (0, 0))],
        out_specs=[pl.BlockSpec((1, ts, HEAD_PAD), lambda b, s: (b, s, 0))] * 2,
        out_shape=[jax.ShapeDtypeStruct((bsz, seq, HEAD_PAD), F32)] * 2,
        compiler_params=_params("parallel", "parallel"),
        name="rope_tables",
    )(positions.reshape(bsz, seq, 1), freq_row)


def _inproj_kernel(x_ref, win_ref, wuq_ref, wukv_ref, cos_ref, sin_ref, ones_ref,
                   q_ref, k_ref, v_ref, u_ref, qs_ref, ks_ref, vs_ref):
    xb = x_ref[0].astype(BF16)
    acc = _dot(xb, win_ref[...])
    cos = cos_ref[0]
    sin = sin_ref[0]
    cqn = _rms_rows(acc[:, _C_CQ:_C_CKV], MLA_Q_RANK).astype(BF16)
    qq = _dot(cqn, wuq_ref[...])
    ckvn = _rms_rows(acc[:, _C_CKV:_C_KRP], MLA_KV_RANK).astype(BF16)
    kv = _dot(ckvn, wukv_ref[...])
    k_rope = acc[:, _C_KRP:_C_KRS] * cos + acc[:, _C_KRS:_C_SSM] * sin
    for h in range(MLA_HEADS):
        lo, hi = h * HEAD_PAD, (h + 1) * HEAD_PAD
        q_ref[0, :, lo:hi] = (qq[:, lo:hi] * cos + qq[:, MLA_QK_W + lo:MLA_QK_W + hi] * sin).astype(BF16)
        k_ref[0, :, lo:hi] = (kv[:, lo:hi] + k_rope).astype(BF16)
    v_ref[0] = (kv[:, MLA_QK_W:] + ones_ref[...]).T.astype(BF16)
    u_ref[...] = acc[:, _C_SSM:_C_QSB].astype(BF16)
    qs_ref[0] = acc[:, _C_QSB:_C_KSB].astype(BF16)
    ks_ref[0] = acc[:, _C_KSB:_C_VSB].astype(BF16)
    vs_ref[0] = acc[:, _C_VSB:_C_END].T.astype(BF16)


def _inproj(x, win, wuq, wukv, cos_t, sin_t):
    bsz, seq, _ = x.shape
    ts = min(seq, 512)
    tok = lambda w: pl.BlockSpec((1, ts, w), lambda b, s: (b, s, 0))
    tok_t = lambda w: pl.BlockSpec((1, w, ts), lambda b, s: (b, 0, s))
    full = lambda a: pl.BlockSpec(a.shape, lambda b, s: (0,) * a.ndim)
    sb_w = SB_HEADS * HEAD_PAD
    lane = jnp.arange(MLA_QK_W) % HEAD_PAD
    ones_row = (lane == MLA_V).astype(F32).reshape(1, MLA_QK_W)
    return pl.pallas_call(
        _inproj_kernel,
        grid=(bsz, seq // ts),
        in_specs=[tok(D_MODEL), full(win), full(wuq), full(wukv), tok(HEAD_PAD), tok(HEAD_PAD), full(ones_row)],
        out_specs=[tok(MLA_QK_W), tok(MLA_QK_W), tok_t(MLA_QK_W),
                   pl.BlockSpec((ts, SSM_WIDTH), lambda b, s: (s, b)),
                   tok(sb_w), tok(sb_w), tok_t(SB_WIDTH)],
        out_shape=[jax.ShapeDtypeStruct((bsz, seq, MLA_QK_W), BF16),
                   jax.ShapeDtypeStruct((bsz, seq, MLA_QK_W), BF16),
                   jax.ShapeDtypeStruct((bsz, MLA_QK_W, seq), BF16),
                   jax.ShapeDtypeStruct((seq, bsz * SSM_WIDTH), BF16),
                   jax.ShapeDtypeStruct((bsz, seq, sb_w), BF16),
                   jax.ShapeDtypeStruct((bsz, seq, sb_w), BF16),
                   jax.ShapeDtypeStruct((bsz, SB_WIDTH, seq), BF16)],
        compiler_params=_params("parallel", "parallel"),
        name="inproj",
    )(x, win, wuq, wukv, cos_t, sin_t, ones_row)


def _mla_kernel(q_ref, k_ref, vt_ref, o_ref, m_ref, acc_ref, s_ref, p_ref, *, tq):
    qi = pl.program_id(1)
    key_chunk = lax.broadcasted_iota(jnp.int32, (tq, tq), 0) // CHUNK
    query_chunk = lax.broadcasted_iota(jnp.int32, (tq, tq), 1) // CHUNK
    allowed = key_chunk <= query_chunk
    m_ref[...] = jnp.full(m_ref.shape, NEG_BIG, F32)
    acc_ref[...] = jnp.zeros(acc_ref.shape, F32)

    def block(kb, masked):
        ks = pl.multiple_of(kb * tq, tq)
        for h in range(MLA_HEADS):
            tile = slice(h * HEAD_PAD, (h + 1) * HEAD_PAD)
            s_ref[h] = _dot_nt(k_ref[0, pl.ds(ks, tq), tile], q_ref[0, :, tile])
        alphas = []
        for h in range(MLA_HEADS):
            s = s_ref[h]
            if masked:
                s = jnp.where(allowed, s, NEG_BIG)
            m_old = m_ref[h]
            m_new = jnp.maximum(m_old, jnp.max(s, axis=0, keepdims=True))
            p_ref[h] = jnp.exp2(s - m_new).astype(BF16)
            alphas.append(jnp.exp2(m_old - m_new))
            m_ref[h] = m_new
        for h in range(MLA_HEADS):
            tile = slice(h * HEAD_PAD, (h + 1) * HEAD_PAD)
            acc_ref[h] = alphas[h] * acc_ref[h] + _dot(vt_ref[0, tile, pl.ds(ks, tq)], p_ref[h])

    def body(kb, carry):
        block(kb, False)
        return carry

    lax.fori_loop(0, qi, body, 0)
    block(qi, True)
    row = lax.broadcasted_iota(jnp.int32, (HEAD_PAD, tq), 0)
    for h in range(MLA_HEADS):
        acc = acc_ref[h]
        out_t = jnp.where(row < MLA_V, acc / acc[MLA_V:MLA_V + 1, :], 0.0)
        o_ref[0, :, h * HEAD_PAD:(h + 1) * HEAD_PAD] = out_t.T.astype(BF16)


def _mla_attention(q, k, vt):
    bsz, seq, _ = q.shape
    tq = min(seq, 256)
    return pl.pallas_call(
        functools.partial(_mla_kernel, tq=tq),
        grid=(bsz, seq // tq),
        in_specs=[pl.BlockSpec((1, tq, MLA_QK_W), lambda b, i: (b, i, 0)),
                  pl.BlockSpec((1, seq, MLA_QK_W), lambda b, i: (b, 0, 0)),
                  pl.BlockSpec((1, MLA_QK_W, seq), lambda b, i: (b, 0, 0))],
        out_specs=pl.BlockSpec((1, tq, MLA_QK_W), lambda b, i: (b, i, 0)),
        out_shape=jax.ShapeDtypeStruct((bsz, seq, MLA_QK_W), BF16),
        scratch_shapes=[pltpu.VMEM((MLA_HEADS, 1, tq), F32), pltpu.VMEM((MLA_HEADS, HEAD_PAD, tq), F32),
                        pltpu.VMEM((MLA_HEADS, tq, tq), F32), pltpu.VMEM((MLA_HEADS, tq, tq), BF16)],
        compiler_params=_params("parallel", "arbitrary"),
        name="mla_attention",
    )(q, k, vt)


def _sb_kernel(q_ref, k_ref, vt_ref, tri_ref, o_ref, right_ref, acc_ref, z_ref, split_ref, sum_ref, w_ref, *, tq):
    qi = pl.program_id(1)
    earlier = lax.broadcasted_iota(jnp.int32, (tq, tq), 0) < lax.broadcasted_iota(jnp.int32, (tq, tq), 1)
    right_ref[...] = jnp.zeros(right_ref.shape, F32)
    acc_ref[...] = jnp.zeros(acc_ref.shape, F32)

    def block(kb, masked):
        ks = pl.multiple_of(kb * tq, tq)
        for h in range(SB_HEADS):
            tile = slice(h * HEAD_PAD, (h + 1) * HEAD_PAD)
            z_ref[h] = _dot_nt(k_ref[0, pl.ds(ks, tq), tile], q_ref[0, :, tile])
        for h in range(SB_HEADS):
            z = z_ref[h]
            fail = jnp.maximum(z, 0.0) + jnp.log2(1.0 + jnp.exp2(-jnp.abs(z)))
            z_ref[h] = z - fail
            if masked:
                fail = jnp.where(earlier, fail, 0.0)
            hi = fail.astype(BF16)
            split_ref[h, 0:tq, :] = hi
            split_ref[h, tq:, :] = (fail - hi.astype(F32)).astype(BF16)
            sum_ref[h] = jnp.sum(fail, axis=0, keepdims=True)
        for h in range(SB_HEADS):
            between = _dot(tri_ref[...], split_ref[h]) + right_ref[h]
            w = jnp.exp2(z_ref[h] - between)
            if masked:
                w = jnp.where(earlier, w, 0.0)
            w_ref[h] = w.astype(BF16)
            right_ref[h] += sum_ref[h]
        for h in range(SB_HEADS):
            pair = slice((h // 2) * HEAD_PAD, (h // 2 + 1) * HEAD_PAD)
            acc_ref[h] += _dot(vt_ref[0, pair, pl.ds(ks, tq)], w_ref[h])

    def body(i, carry):
        block(qi - 1 - i, False)
        return carry

    block(qi, True)
    lax.fori_loop(0, qi, body, 0)
    row = lax.broadcasted_iota(jnp.int32, (HEAD_PAD, tq), 0)
    for pair in range(SB_HEADS // 2):
        out_t = jnp.where(row < SB_DIM, acc_ref[2 * pair], acc_ref[2 * pair + 1])
        o_ref[0, :, pair * HEAD_PAD:(pair + 1) * HEAD_PAD] = out_t.T.astype(BF16)


def _sb_attention(q, k, vt):
    bsz, seq, _ = q.shape
    tq = min(seq, 256)
    sb_w = SB_HEADS * HEAD_PAD
    tri = (lax.broadcasted_iota(jnp.int32, (tq, tq), 1) > lax.broadcasted_iota(jnp.int32, (tq, tq), 0)).astype(BF16)
    tri2 = jnp.concatenate([tri, tri], axis=1)
    return pl.pallas_call(
        functools.partial(_sb_kernel, tq=tq),
        grid=(bsz, seq // tq),
        in_specs=[pl.BlockSpec((1, tq, sb_w), lambda b, i: (b, i, 0)),
                  pl.BlockSpec((1, seq, sb_w), lambda b, i: (b, 0, 0)),
                  pl.BlockSpec((1, SB_WIDTH, seq), lambda b, i: (b, 0, 0)),
                  pl.BlockSpec((tq, 2 * tq), lambda b, i: (0, 0))],
        out_specs=pl.BlockSpec((1, tq, SB_WIDTH), lambda b, i: (b, i, 0)),
        out_shape=jax.ShapeDtypeStruct((bsz, seq, SB_WIDTH), BF16),
        scratch_shapes=[pltpu.VMEM((SB_HEADS, 1, tq), F32), pltpu.VMEM((SB_HEADS, HEAD_PAD, tq), F32),
                        pltpu.VMEM((SB_HEADS, tq, tq), F32), pltpu.VMEM((SB_HEADS, 2 * tq, tq), BF16),
                        pltpu.VMEM((SB_HEADS, 1, tq), F32), pltpu.VMEM((SB_HEADS, tq, tq), BF16)],
        compiler_params=_params("parallel", "arbitrary"),
        name="sb_attention",
    )(q, k, vt, tri2)


def _ssm_kernel(u_ref, bbd_ref, lam_ref, cbd_ref, d_ref, wglu_ref, bglu_ref, y_ref, h_ref, hs_ref, *, tt, nb):
    @pl.when(pl.program_id(0) == 0)
    def _():
        h_ref[...] = jnp.zeros_like(h_ref)

    u = u_ref[...]
    hs_ref[...] = _dot(u, bbd_ref[...])
    lam_re = jnp.broadcast_to(lam_ref[0:1, :], (nb, SSM_STATES))
    lam_im = jnp.broadcast_to(lam_ref[1:2, :], (nb, SSM_STATES))

    def step(t, carry):
        h_re, h_im = carry
        r0 = pl.multiple_of(t * nb, nb)
        n_re = lam_re * h_re - lam_im * h_im + hs_ref[pl.ds(r0, nb), 0:SSM_STATES]
        n_im = lam_re * h_im + lam_im * h_re + hs_ref[pl.ds(r0, nb), SSM_STATES:]
        hs_ref[pl.ds(r0, nb), 0:SSM_STATES] = n_re
        hs_ref[pl.ds(r0, nb), SSM_STATES:] = n_im
        return n_re, n_im

    h_re, h_im = lax.fori_loop(0, tt, step, (h_ref[:, 0:SSM_STATES], h_ref[:, SSM_STATES:]))
    h_ref[:, 0:SSM_STATES] = h_re
    h_ref[:, SSM_STATES:] = h_im

    y = _dot(hs_ref[...].astype(BF16), cbd_ref[...]) + d_ref[...] * u.astype(F32)
    y = jax.nn.gelu(y)
    y = y * jax.nn.sigmoid(_dot(y.astype(BF16), wglu_ref[...]) + bglu_ref[...])
    y_ref[...] = y.astype(BF16)


def _ssm(u_tm, bbd, lam, cbd, d_row, wglu, bglu_row, nb):
    rows = u_tm.shape[0]
    seq = rows // nb
    tt = min(seq, 128)
    full = lambda a: pl.BlockSpec(a.shape, lambda i: (0,) * a.ndim)
    return pl.pallas_call(
        functools.partial(_ssm_kernel, tt=tt, nb=nb),
        grid=(seq // tt,),
        in_specs=[pl.BlockSpec((tt * nb, SSM_WIDTH), lambda i: (i, 0)),
                  full(bbd), full(lam), full(cbd), full(d_row), full(wglu), full(bglu_row)],
        out_specs=pl.BlockSpec((tt * nb, SSM_WIDTH), lambda i: (i, 0)),
        out_shape=jax.ShapeDtypeStruct((rows, SSM_WIDTH), BF16),
        scratch_shapes=[pltpu.VMEM((nb, 2 * SSM_STATES), F32),
                        pltpu.VMEM((tt * nb, 2 * SSM_STATES), F32)],
        compiler_params=_params("arbitrary"),
        name="ssm",
    )(u_tm, bbd, lam, cbd, d_row, wglu, bglu_row)


def _memkv_kernel(mem_ref, w_ref, k_ref, v_ref):
    kv = _dot(mem_ref[0].astype(BF16), w_ref[...])
    k_ref[0] = kv[:, :D_MODEL].astype(BF16)
    v_ref[0] = kv[:, D_MODEL:].astype(BF16)


def _memkv(mem, w_mkv):
    bsz, mlen, _ = mem.shape
    return pl.pallas_call(
        _memkv_kernel,
        grid=(bsz,),
        in_specs=[pl.BlockSpec((1, mlen, D_MODEL), lambda b: (b, 0, 0)),
                  pl.BlockSpec(w_mkv.shape, lambda b: (0, 0))],
        out_specs=[pl.BlockSpec((1, mlen, D_MODEL), lambda b: (b, 0, 0))] * 2,
        out_shape=[jax.ShapeDtypeStruct((bsz, mlen, D_MODEL), BF16)] * 2,
        compiler_params=_params("parallel"),
        name="mem_kv",
    )(mem, w_mkv)


def _mix_mem_kernel(ya_ref, ys_ref, yb_ref, x_ref, gmix_ref, wout_ref, ln1_ref,
                    mk_ref, mv_ref, wmq_ref, wmo_ref, ln2_ref, xt_ref, *, alpha):
    gm = gmix_ref[...]
    c1, c2 = MLA_QK_W, MLA_QK_W + SSM_WIDTH
    ya = (_rms_rows(ya_ref[0].astype(F32), MLA_V_W) * gm[:, 0:c1]).astype(BF16)
    ys = (_rms_rows(ys_ref[...].astype(F32), SSM_WIDTH) * gm[:, c1:c2]).astype(BF16)
    yb = (_rms_rows(yb_ref[0].astype(F32), SB_WIDTH) * gm[:, c2:]).astype(BF16)
    mix = _dot(ya, wout_ref[0:c1, :]) + _dot(ys, wout_ref[c1:c2, :]) + _dot(yb, wout_ref[c2:, :])
    x1 = _layer_norm_rows(alpha * x_ref[0] + mix, ln1_ref[0:1, :], ln1_ref[1:2, :])

    q = _dot(x1.astype(BF16), wmq_ref[...]).astype(BF16)
    heads = []
    for h in range(MEM_HEADS):
        lo, hi = h * MEM_HEAD_DIM, (h + 1) * MEM_HEAD_DIM
        s = _dot_nt(q[:, lo:hi], mk_ref[0, :, lo:hi])
        p = jnp.exp(s - jnp.max(s, axis=-1, keepdims=True))
        p = p / jnp.sum(p, axis=-1, keepdims=True)
        heads.append(_dot(p.astype(BF16), mv_ref[0, :, lo:hi]).astype(BF16))
    o = jnp.concatenate(heads, axis=-1)
    x2 = _layer_norm_rows(alpha * x1 + _dot(o, wmo_ref[...]), ln2_ref[0:1, :], ln2_ref[1:2, :])
    xt_ref[...] = x2.T


def _mix_mem(ya, ys_tm, yb, x, gmix, wout, ln1, mk, mv, wmq, wmo, ln2, alpha):
    bsz, seq, _ = x.shape
    ts = min(seq, 256)
    nst = seq // ts
    mlen = mk.shape[1]
    tok = lambda w: pl.BlockSpec((1, ts, w), lambda b, s: (b, s, 0))
    full = lambda a: pl.BlockSpec(a.shape, lambda b, s: (0,) * a.ndim)
    return pl.pallas_call(
        functools.partial(_mix_mem_kernel, alpha=alpha),
        grid=(bsz, nst),
        in_specs=[tok(MLA_QK_W), pl.BlockSpec((ts, SSM_WIDTH), lambda b, s: (s, b)), tok(SB_WIDTH), tok(D_MODEL),
                  full(gmix), full(wout), full(ln1),
                  pl.BlockSpec((1, mlen, D_MODEL), lambda b, s: (b, 0, 0)),
                  pl.BlockSpec((1, mlen, D_MODEL), lambda b, s: (b, 0, 0)),
                  full(wmq), full(wmo), full(ln2)],
        out_specs=pl.BlockSpec((D_MODEL, ts), lambda b, s: (0, b * nst + s)),
        out_shape=jax.ShapeDtypeStruct((D_MODEL, bsz * seq), F32),
        compiler_params=_params("parallel", "parallel"),
        name="mix_mem",
    )(ya, ys_tm, yb, x, gmix, wout, ln1, mk, mv, wmq, wmo, ln2)


def _top16(s, vals_ref):
    rank = jnp.full(s.shape, float(PEER_TOPK), F32)
    work = s
    for k in range(PEER_TOPK):
        m = jnp.max(work, axis=0, keepdims=True)
        hit = work == m
        rank = jnp.where(hit, float(k), rank)
        work = jnp.where(hit, -jnp.inf, work)
        vals_ref[k:k + 1, :] = m
    return rank


def _peer_route_kernel(xt_ref, wpq_ref, keys_ref, a_ref, n_ref, b_ref, r_ref, v1_ref, v2_ref, cand_ref):
    xb = xt_ref[...].astype(BF16)
    q = _dot(wpq_ref[...], xb).astype(BF16)
    s1 = _dot(keys_ref[0, 0], q[0:PEER_KEY_DIM, :])
    s2 = _dot(keys_ref[0, 1], q[PEER_KEY_DIM:, :])
    rank1 = _top16(s1, v1_ref)
    rank2 = _top16(s2, v2_ref)
    v2 = v2_ref[...]
    for k in range(PEER_TOPK):
        cand_ref[k * PEER_TOPK:(k + 1) * PEER_TOPK, :] = v1_ref[k:k + 1, :] + v2
    cand = cand_ref[...]
    work = cand
    for _ in range(PEER_TOPK - 1):
        m = jnp.max(work, axis=0, keepdims=True)
        work = jnp.where(work == m, -jnp.inf, work)
    thr = jnp.max(work, axis=0, keepdims=True)
    keep = cand >= thr
    cmax = v1_ref[0:1, :] + v2_ref[0:1, :]
    z = jnp.sum(jnp.where(keep, jnp.exp(cand - cmax), 0.0), axis=0, keepdims=True)
    keep_f = jnp.where(keep, 1.0, 0.0)
    n = jnp.zeros(rank1.shape, F32)
    for k in range(PEER_TOPK):
        l_k = jnp.sum(keep_f[k * PEER_TOPK:(k + 1) * PEER_TOPK, :], axis=0, keepdims=True)
        n = jnp.where(rank1 == float(k), l_k, n)
    a = jnp.where(rank1 < float(PEER_TOPK), jnp.exp(s1 - v1_ref[0:1, :]), 0.0)
    b = jnp.where(rank2 < float(PEER_TOPK), jnp.exp(s2 - v2_ref[0:1, :]), 0.0) / z
    tt = a.shape[-1]
    a_ref[:, 0] = a.reshape(PEER_N_KEYS // 8, 8, tt)
    n_ref[:, 0] = n.reshape(PEER_N_KEYS // 8, 8, tt)
    b_ref[0] = b.astype(BF16)
    r_ref[0] = rank2.astype(BF16)


def _peer_route(xt, wpq_t, keys):
    ntok = xt.shape[1]
    tt = min(ntok, 512)
    nblk = PEER_N_KEYS // 8
    return pl.pallas_call(
        _peer_route_kernel,
        grid=(ntok // tt, PEER_HEADS),
        in_specs=[pl.BlockSpec((D_MODEL, tt), lambda i, h: (0, i)),
                  pl.BlockSpec((2 * PEER_KEY_DIM, D_MODEL), lambda i, h: (h, 0)),
                  pl.BlockSpec((1, 2, PEER_N_KEYS, PEER_KEY_DIM), lambda i, h: (h, 0, 0, 0))],
        out_specs=[pl.BlockSpec((nblk, 1, 8, tt), lambda i, h: (0, h, 0, i)),
                   pl.BlockSpec((nblk, 1, 8, tt), lambda i, h: (0, h, 0, i)),
                   pl.BlockSpec((1, PEER_N_KEYS, tt), lambda i, h: (h, 0, i)),
                   pl.BlockSpec((1, PEER_N_KEYS, tt), lambda i, h: (h, 0, i))],
        out_shape=[jax.ShapeDtypeStruct((nblk, PEER_HEADS, 8, ntok), F32),
                   jax.ShapeDtypeStruct((nblk, PEER_HEADS, 8, ntok), F32),
                   jax.ShapeDtypeStruct((PEER_HEADS, PEER_N_KEYS, ntok), BF16),
                   jax.ShapeDtypeStruct((PEER_HEADS, PEER_N_KEYS, ntok), BF16)],
        scratch_shapes=[pltpu.VMEM((PEER_TOPK, tt), F32), pltpu.VMEM((PEER_TOPK, tt), F32),
                        pltpu.VMEM((PEER_TOPK * PEER_TOPK, tt), F32)],
        compiler_params=_params("parallel", "arbitrary"),
        name="peer_route",
    )(xt, wpq_t, keys)


PEER_I1_PER_TILE = 8
PEER_TILE = PEER_I1_PER_TILE * PEER_N_KEYS


def _peer_dense_kernel(xt_ref, a_ref, n_ref, b_ref, r_ref, u_ref, vt_ref, ln_ref, o_ref,
                       acc_ref, xb_ref, gh_ref, *, alpha):
    e = pl.program_id(1)

    @pl.when(e == 0)
    def _():
        acc_ref[...] = jnp.zeros_like(acc_ref)
        xb_ref[...] = xt_ref[...].astype(BF16)

    tt = xb_ref.shape[1]
    rows = BF16_SUBLANES
    gh_ref[...] = _dot(u_ref[...], xb_ref[...]).astype(BF16)
    cw = min(tt, 2 * LANE)
    for j in range(PEER_I1_PER_TILE):
        for c in range(tt // cw):
            ln = slice(c * cw, (c + 1) * cw)
            a_rows = [jnp.broadcast_to(a_ref[0, h, j:j + 1, ln], (rows, cw)).astype(BF16) for h in range(PEER_HEADS)]
            n_rows = [jnp.broadcast_to(n_ref[0, h, j:j + 1, ln], (rows, cw)).astype(BF16) for h in range(PEER_HEADS)]
            for g in range(PEER_N_KEYS // rows):
                i2 = slice(g * rows, (g + 1) * rows)
                gate = None
                for h in range(PEER_HEADS):
                    term = jnp.where(r_ref[h, i2, ln] < n_rows[h], b_ref[h, i2, ln] * a_rows[h], jnp.zeros((), BF16))
                    gate = term if gate is None else gate + term
                e0 = j * PEER_N_KEYS + g * rows
                gh_ref[e0:e0 + rows, ln] = gate * _gelu_tanh(gh_ref[e0:e0 + rows, ln])
    acc_ref[...] += _dot(vt_ref[...], gh_ref[...])

    @pl.when(e == pl.num_programs(1) - 1)
    def _():
        y = alpha * xt_ref[...] + acc_ref[...]
        mu = jnp.mean(y, axis=0, keepdims=True)
        c = y - mu
        var = jnp.mean(c * c, axis=0, keepdims=True)
        o_ref[...] = (c * lax.rsqrt(var + NORM_EPS)).T * ln_ref[0:1, :] + ln_ref[1:2, :]


def _peer_dense(xt, a, n, b, r, u, vt, ln, alpha):
    ntok = xt.shape[1]
    tt = min(ntok, 1024)
    ntile = PEER_EXPERTS // PEER_TILE
    return pl.pallas_call(
        functools.partial(_peer_dense_kernel, alpha=alpha),
        grid=(ntok // tt, ntile),
        in_specs=[pl.BlockSpec((D_MODEL, tt), lambda i, e: (0, i)),
                  pl.BlockSpec((1, PEER_HEADS, PEER_I1_PER_TILE, tt), lambda i, e: (e, 0, 0, i)),
                  pl.BlockSpec((1, PEER_HEADS, PEER_I1_PER_TILE, tt), lambda i, e: (e, 0, 0, i)),
                  pl.BlockSpec((PEER_HEADS, PEER_N_KEYS, tt), lambda i, e: (0, 0, i)),
                  pl.BlockSpec((PEER_HEADS, PEER_N_KEYS, tt), lambda i, e: (0, 0, i)),
                  pl.BlockSpec((PEER_TILE, D_MODEL), lambda i, e: (e, 0)),
                  pl.BlockSpec((D_MODEL, PEER_TILE), lambda i, e: (0, e)),
                  pl.BlockSpec(ln.shape, lambda i, e: (0, 0))],
        out_specs=pl.BlockSpec((tt, D_MODEL), lambda i, e: (i, 0)),
        out_shape=jax.ShapeDtypeStruct((ntok, D_MODEL), F32),
        scratch_shapes=[pltpu.VMEM((D_MODEL, tt), F32), pltpu.VMEM((D_MODEL, tt), BF16),
                        pltpu.VMEM((PEER_TILE, tt), BF16)],
        compiler_params=_params("parallel", "arbitrary"),
        name="peer_dense",
    )(xt, a, n, b, r, u, vt, ln)


def _pad_heads(w, heads, width):
    rows = w.shape[0]
    w = w.reshape(rows, heads, width)
    return jnp.pad(w, ((0, 0), (0, 0), (0, HEAD_PAD - width))).reshape(rows, heads * HEAD_PAD)


def _rotate_half_cols(w, heads, width, nope):
    rows = w.shape[0]
    w = w.reshape(rows, heads, width)
    half = (width - nope) // 2
    x1, x2 = w[..., nope:nope + half], w[..., nope + half:]
    out = jnp.concatenate([jnp.zeros_like(w[..., :nope]), -x2, x1], axis=-1)
    return out.reshape(rows, heads * width)


def _pack_inproj(w_in, g_cq, g_ckv, w_uq, w_ukv):
    c = 0
    cols = {}
    for name, width in (("cq", MLA_Q_RANK), ("ckv", MLA_KV_RANK), ("kr", MLA_ROPE), ("ssm", SSM_WIDTH),
                        ("qsb", SB_WIDTH), ("ksb", SB_WIDTH), ("vsb", SB_WIDTH)):
        cols[name] = w_in[:, c:c + width]
        c += width
    rows = w_in.shape[0]
    zeros = lambda n: jnp.zeros((rows, n), F32)
    kr = cols["kr"]
    half = MLA_ROPE // 2
    kr_plain = jnp.concatenate([zeros(MLA_NOPE), kr, zeros(HEAD_PAD - MLA_NOPE - MLA_ROPE)], axis=1)
    kr_swap = jnp.concatenate([zeros(MLA_NOPE), -kr[:, half:], kr[:, :half],
                               zeros(HEAD_PAD - MLA_NOPE - MLA_ROPE)], axis=1)
    sb_scale = SB_DIM ** -0.5 * LOG2E
    win = jnp.concatenate([cols["cq"], cols["ckv"], kr_plain, kr_swap, cols["ssm"],
                           _pad_heads(cols["qsb"] * sb_scale, SB_HEADS, SB_DIM),
                           _pad_heads(cols["ksb"], SB_HEADS, SB_DIM), cols["vsb"]], axis=1).astype(BF16)

    qk_dim = MLA_NOPE + MLA_ROPE
    wq = w_uq * (g_cq[:, None] * (qk_dim ** -0.5 * LOG2E))
    wuq = jnp.concatenate([_pad_heads(wq, MLA_HEADS, qk_dim),
                           _pad_heads(_rotate_half_cols(wq, MLA_HEADS, qk_dim, MLA_NOPE), MLA_HEADS, qk_dim)],
                          axis=1).astype(BF16)
    wkv = (w_ukv * g_ckv[:, None]).reshape(MLA_KV_RANK, MLA_HEADS, MLA_NOPE + MLA_V)
    wukv = jnp.concatenate([_pad_heads(wkv[..., :MLA_NOPE].reshape(MLA_KV_RANK, -1), MLA_HEADS, MLA_NOPE),
                            _pad_heads(wkv[..., MLA_NOPE:].reshape(MLA_KV_RANK, -1), MLA_HEADS, MLA_V)],
                           axis=1).astype(BF16)
    return win, wuq, wukv


def _pad_mla_rows(w):
    cols = w.shape[1]
    head_rows = jnp.pad(w[:MLA_V_W].reshape(MLA_HEADS, MLA_V, cols), ((0, 0), (0, HEAD_PAD - MLA_V), (0, 0)))
    return jnp.concatenate([head_rows.reshape(MLA_QK_W, cols), w[MLA_V_W:]], axis=0)


def _pack_ssm(lam_re, lam_im, log_step, b_re, b_im, c_re, c_im, d_skip):
    step = jnp.exp(log_step)[:, None]
    decay = jnp.exp(lam_re * step)
    ab_re, ab_im = decay * jnp.cos(lam_im * step), decay * jnp.sin(lam_im * step)
    inv = 1.0 / (lam_re * lam_re + lam_im * lam_im)
    f_re = ((ab_re - 1.0) * lam_re + ab_im * lam_im) * inv
    f_im = (ab_im * lam_re - (ab_re - 1.0) * lam_im) * inv
    bb_re = f_re[..., None] * b_re - f_im[..., None] * b_im
    bb_im = f_re[..., None] * b_im + f_im[..., None] * b_re
    eye = jnp.eye(SSM_GROUPS, dtype=F32)
    blk = lambda w: jnp.einsum("gph,gk->ghkp", w, eye).reshape(SSM_WIDTH, SSM_STATES)
    bbd = jnp.concatenate([blk(bb_re), blk(bb_im)], axis=1).astype(BF16)
    blk_c = lambda w: jnp.einsum("ghp,gk->gpkh", w, eye).reshape(SSM_STATES, SSM_WIDTH)
    cbd = jnp.concatenate([blk_c(c_re), blk_c(-c_im)], axis=0).astype(BF16)
    lam = jnp.stack([ab_re.reshape(-1), ab_im.reshape(-1)])
    return bbd, lam, cbd, d_skip.reshape(1, SSM_WIDTH)


def kernel(x, mem, positions, w_in, g_cq, g_ckv, w_uq, w_ukv, ssm_lam_re, ssm_lam_im, ssm_log_step, ssm_b_re, ssm_b_im, ssm_c_re, ssm_c_im, ssm_d, w_glu, b_glu, g_mix, w_out, ln_mix_g, ln_mix_b, w_mq, w_mkv, w_mo, ln_mem_g, ln_mem_b, w_pq, peer_sub_keys, peer_u, peer_v, ln_ffn_g, ln_ffn_b):
    bsz, seq, _ = x.shape
    depth = w_in.shape[0]
    alpha = (2 * depth) ** 0.25
    cos_t, sin_t = _rope_tables(positions)
    for l in range(depth):
        win, wuq, wukv = _pack_inproj(w_in[l], g_cq[l], g_ckv[l], w_uq[l], w_ukv[l])
        q, k, v, u_tm, q_sb, k_sb, v_sb = _inproj(x, win, wuq, wukv, cos_t, sin_t)
        y_mla = _mla_attention(q, k, v)
        y_sb = _sb_attention(q_sb, k_sb, v_sb)
        bbd, lam, cbd, d_row = _pack_ssm(ssm_lam_re[l], ssm_lam_im[l], ssm_log_step[l], ssm_b_re[l], ssm_b_im[l],
                                         ssm_c_re[l], ssm_c_im[l], ssm_d[l])
        y_ssm = _ssm(u_tm.reshape(seq * bsz, SSM_WIDTH), bbd, lam, cbd, d_row,
                     w_glu[l].astype(BF16), b_glu[l].reshape(1, SSM_WIDTH), bsz)
        mk, mv = _memkv(mem, w_mkv[l].astype(BF16))
        xt = _mix_mem(y_mla, y_ssm.reshape(seq, bsz * SSM_WIDTH), y_sb, x,
                      _pad_mla_rows(g_mix[l][:, None]).reshape(1, -1), _pad_mla_rows(w_out[l]).astype(BF16),
                      jnp.stack([ln_mix_g[l], ln_mix_b[l]]),
                      mk, mv, (w_mq[l] * MEM_HEAD_DIM ** -0.5).astype(BF16), w_mo[l].astype(BF16),
                      jnp.stack([ln_mem_g[l], ln_mem_b[l]]), alpha)
        a, n, b, r = _peer_route(xt, w_pq[l].T.astype(BF16), peer_sub_keys[l].astype(BF16))
        x = _peer_dense(xt, a, n, b, r, peer_u[l].astype(BF16), peer_v[l].T.astype(BF16),
                        jnp.stack([ln_ffn_g[l], ln_ffn_b[l]]), alpha).reshape(bsz, seq, D_MODEL)
    return x
```

```python
import functools
import math

import jax
import jax.numpy as jnp
from jax import lax
from jax.experimental import pallas as pl
from jax.experimental.pallas import tpu as pltpu

F32 = jnp.float32
BF16 = jnp.bfloat16

D_MODEL = 1024
CHUNK = 64
NORM_EPS = 1e-5
NEG_BIG = -1e30
LOG2E = math.log2(math.e)

MLA_HEADS = 6
MLA_NOPE = 64
MLA_ROPE = 32
MLA_V = 64
MLA_Q_RANK = 256
MLA_KV_RANK = 128
ROPE_THETA = 10000.0

SSM_GROUPS = 24
SSM_CH = 16
SSM_STATE = 64
SSM_WIDTH = SSM_GROUPS * SSM_CH
SSM_STATES = SSM_GROUPS * SSM_STATE

SB_HEADS = 4
SB_DIM = 64
SB_WIDTH = SB_HEADS * SB_DIM

MEM_HEADS = 4
MEM_HEAD_DIM = D_MODEL // MEM_HEADS

PEER_HEADS = 8
PEER_N_KEYS = 128
PEER_TOPK = 16
PEER_KEY_DIM = 128
PEER_EXPERTS = PEER_N_KEYS * PEER_N_KEYS

LANE = 128
BF16_SUBLANES = 16
HEAD_PAD = 128

VMEM_LIMIT = 56 * 1024 * 1024

_C_CQ = 0
_C_CKV = _C_CQ + MLA_Q_RANK
_C_KRP = _C_CKV + MLA_KV_RANK
_C_KRS = _C_KRP + HEAD_PAD
_C_SSM = _C_KRS + HEAD_PAD
_C_QSB = _C_SSM + SSM_WIDTH
_C_KSB = _C_QSB + SB_HEADS * HEAD_PAD
_C_VSB = _C_KSB + SB_HEADS * HEAD_PAD
_C_END = _C_VSB + SB_WIDTH
MLA_QK_W = MLA_HEADS * HEAD_PAD
MLA_V_W = MLA_HEADS * MLA_V


def _params(*sem):
    return pltpu.CompilerParams(dimension_semantics=sem, vmem_limit_bytes=VMEM_LIMIT)


def _dot(a, b):
    return jnp.dot(a, b, preferred_element_type=F32)


def _dot_nt(a, b):
    return lax.dot_general(a, b, (((1,), (1,)), ((), ())), preferred_element_type=F32)


def _gelu_tanh(x):
    c = math.sqrt(2.0 / math.pi)
    inner = x * (c + (c * 0.044715) * (x * x))
    return (0.5 * x) * (1.0 + jnp.tanh(inner))


def _layer_norm_rows(v, g, b):
    mu = jnp.mean(v, axis=-1, keepdims=True)
    c = v - mu
    var = jnp.mean(c * c, axis=-1, keepdims=True)
    return c * lax.rsqrt(var + NORM_EPS) * g + b


def _rms_rows(v, width):
    return v * lax.rsqrt(jnp.sum(v * v, axis=-1, keepdims=True) * (1.0 / width) + NORM_EPS)


def _rope_kernel(pos_ref, freq_ref, cos_ref, sin_ref):
    ang = pos_ref[0].astype(F32) * freq_ref[...]
    lane = lax.broadcasted_iota(jnp.int32, ang.shape, 1)
    rot = (lane >= MLA_NOPE) & (lane < MLA_NOPE + MLA_ROPE)
    cos_ref[0] = jnp.where(rot, jnp.cos(ang), jnp.where(lane < MLA_NOPE, 1.0, 0.0))
    sin_ref[0] = jnp.where(rot, jnp.sin(ang), 0.0)


def _rope_tables(positions):
    bsz, seq = positions.shape
    ts = min(seq, 512)
    half = MLA_ROPE // 2
    freq = ROPE_THETA ** (-jnp.arange(half, dtype=F32) / half)
    freq_row = jnp.zeros((1, HEAD_PAD), F32).at[0, MLA_NOPE:MLA_NOPE + MLA_ROPE].set(jnp.tile(freq, 2))
    return pl.pallas_call(
        _rope_kernel,
        grid=(bsz, seq // ts),
        in_specs=[pl.BlockSpec((1, ts, 1), lambda b, s: (b, s, 0)),
                  pl.BlockSpec((1, HEAD_PAD), lambda b, s: (0, 0))],
        out_specs=[pl.BlockSpec((1, ts, HEAD_PAD), lambda b, s: (b, s, 0))] * 2,
        out_shape=[jax.ShapeDtypeStruct((bsz, seq, HEAD_PAD), F32)] * 2,
        compiler_params=_params("parallel", "parallel"),
        name="rope_tables",
    )(positions.reshape(bsz, seq, 1), freq_row)


def _inproj_kernel(x_ref, win_ref, wuq_ref, wukv_ref, cos_ref, sin_ref, ones_ref,
                   q_ref, k_ref, v_ref, u_ref, qs_ref, ks_ref, vs_ref):
    xb = x_ref[0].astype(BF16)
    acc = _dot(xb, win_ref[...])
    cos = cos_ref[0]
    sin = sin_ref[0]
    cqn = _rms_rows(acc[:, _C_CQ:_C_CKV], MLA_Q_RANK).astype(BF16)
    qq = _dot(cqn, wuq_ref[...])
    ckvn = _rms_rows(acc[:, _C_CKV:_C_KRP], MLA_KV_RANK).astype(BF16)
    kv = _dot(ckvn, wukv_ref[...])
    k_rope = acc[:, _C_KRP:_C_KRS] * cos + acc[:, _C_KRS:_C_SSM] * sin
    for h in range(MLA_HEADS):
        lo, hi = h * HEAD_PAD, (h + 1) * HEAD_PAD
        q_ref[0, :, lo:hi] = (qq[:, lo:hi] * cos + qq[:, MLA_QK_W + lo:MLA_QK_W + hi] * sin).astype(BF16)
        k_ref[0, :, lo:hi] = (kv[:, lo:hi] + k_rope).astype(BF16)
    v_ref[0] = (kv[:, MLA_QK_W:] + ones_ref[...]).T.astype(BF16)
    u_ref[...] = acc[:, _C_SSM:_C_QSB].astype(BF16)
    qs_ref[0] = acc[:, _C_QSB:_C_KSB].astype(BF16)
    ks_ref[0] = acc[:, _C_KSB:_C_VSB].astype(BF16)
    vs_ref[0] = acc[:, _C_VSB:_C_END].T.astype(BF16)


def _inproj(x, win, wuq, wukv, cos_t, sin_t):
    bsz, seq, _ = x.shape
    ts = min(seq, 512)
    tok = lambda w: pl.BlockSpec((1, ts, w), lambda b, s: (b, s, 0))
    tok_t = lambda w: pl.BlockSpec((1, w, ts), lambda b, s: (b, 0, s))
    full = lambda a: pl.BlockSpec(a.shape, lambda b, s: (0,) * a.ndim)
    sb_w = SB_HEADS * HEAD_PAD
    lane = jnp.arange(MLA_QK_W) % HEAD_PAD
    ones_row = (lane == MLA_V).astype(F32).reshape(1, MLA_QK_W)
    return pl.pallas_call(
        _inproj_kernel,
        grid=(bsz, seq // ts),
        in_specs=[tok(D_MODEL), full(win), full(wuq), full(wukv), tok(HEAD_PAD), tok(HEAD_PAD), full(ones_row)],
        out_specs=[tok(MLA_QK_W), tok(MLA_QK_W), tok_t(MLA_QK_W),
                   pl.BlockSpec((ts, SSM_WIDTH), lambda b, s: (s, b)),
                   tok(sb_w), tok(sb_w), tok_t(SB_WIDTH)],
        out_shape=[jax.ShapeDtypeStruct((bsz, seq, MLA_QK_W), BF16),
                   jax.ShapeDtypeStruct((bsz, seq, MLA_QK_W), BF16),
                   jax.ShapeDtypeStruct((bsz, MLA_QK_W, seq), BF16),
                   jax.ShapeDtypeStruct((seq, bsz * SSM_WIDTH), BF16),
                   jax.ShapeDtypeStruct((bsz, seq, sb_w), BF16),
                   jax.ShapeDtypeStruct((bsz, seq, sb_w), BF16),
                   jax.ShapeDtypeStruct((bsz, SB_WIDTH, seq), BF16)],
        compiler_params=_params("parallel", "parallel"),
        name="inproj",
    )(x, win, wuq, wukv, cos_t, sin_t, ones_row)


def _mla_kernel(q_ref, k_ref, vt_ref, o_ref, m_ref, acc_ref, s_ref, p_ref, *, tq):
    qi = pl.program_id(1)
    key_chunk = lax.broadcasted_iota(jnp.int32, (tq, tq), 0) // CHUNK
    query_chunk = lax.broadcasted_iota(jnp.int32, (tq, tq), 1) // CHUNK
    allowed = key_chunk <= query_chunk
    m_ref[...] = jnp.full(m_ref.shape, NEG_BIG, F32)
    acc_ref[...] = jnp.zeros(acc_ref.shape, F32)

    def block(kb, masked):
        ks = pl.multiple_of(kb * tq, tq)
        for h in range(MLA_HEADS):
            tile = slice(h * HEAD_PAD, (h + 1) * HEAD_PAD)
            s_ref[h] = _dot_nt(k_ref[0, pl.ds(ks, tq), tile], q_ref[0, :, tile])
        alphas = []
        for h in range(MLA_HEADS):
            s = s_ref[h]
            if masked:
                s = jnp.where(allowed, s, NEG_BIG)
            m_old = m_ref[h]
            m_new = jnp.maximum(m_old, jnp.max(s, axis=0, keepdims=True))
            p_ref[h] = jnp.exp2(s - m_new).astype(BF16)
            alphas.append(jnp.exp2(m_old - m_new))
            m_ref[h] = m_new
        for h in range(MLA_HEADS):
            tile = slice(h * HEAD_PAD, (h + 1) * HEAD_PAD)
            acc_ref[h] = alphas[h] * acc_ref[h] + _dot(vt_ref[0, tile, pl.ds(ks, tq)], p_ref[h])

    def body(kb, carry):
        block(kb, False)
        return carry

    lax.fori_loop(0, qi, body, 0)
    block(qi, True)
    row = lax.broadcasted_iota(jnp.int32, (HEAD_PAD, tq), 0)
    for h in range(MLA_HEADS):
        acc = acc_ref[h]
        out_t = jnp.where(row < MLA_V, acc / acc[MLA_V:MLA_V + 1, :], 0.0)
        o_ref[0, :, h * HEAD_PAD:(h + 1) * HEAD_PAD] = out_t.T.astype(BF16)


def _mla_attention(q, k, vt):
    bsz, seq, _ = q.shape
    tq = min(seq, 256)
    return pl.pallas_call(
        functools.partial(_mla_kernel, tq=tq),
        grid=(bsz, seq // tq),
        in_specs=[pl.BlockSpec((1, tq, MLA_QK_W), lambda b, i: (b, i, 0)),
                  pl.BlockSpec((1, seq, MLA_QK_W), lambda b, i: (b, 0, 0)),
                  pl.BlockSpec((1, MLA_QK_W, seq), lambda b, i: (b, 0, 0))],
        out_specs=pl.BlockSpec((1, tq, MLA_QK_W), lambda b, i: (b, i, 0)),
        out_shape=jax.ShapeDtypeStruct((bsz, seq, MLA_QK_W), BF16),
        scratch_shapes=[pltpu.VMEM((MLA_HEADS, 1, tq), F32), pltpu.VMEM((MLA_HEADS, HEAD_PAD, tq), F32),
                        pltpu.VMEM((MLA_HEADS, tq, tq), F32), pltpu.VMEM((MLA_HEADS, tq, tq), BF16)],
        compiler_params=_params("parallel", "arbitrary"),
        name="mla_attention",
    )(q, k, vt)


def _sb_kernel(q_ref, k_ref, vt_ref, tri_ref, o_ref, right_ref, acc_ref, z_ref, split_ref, sum_ref, w_ref, *, tq):
    qi = pl.program_id(1)
    earlier = lax.broadcasted_iota(jnp.int32, (tq, tq), 0) < lax.broadcasted_iota(jnp.int32, (tq, tq), 1)
    right_ref[...] = jnp.zeros(right_ref.shape, F32)
    acc_ref[...] = jnp.zeros(acc_ref.shape, F32)

    def block(kb, masked):
        ks = pl.multiple_of(kb * tq, tq)
        for h in range(SB_HEADS):
            tile = slice(h * HEAD_PAD, (h + 1) * HEAD_PAD)
            z_ref[h] = _dot_nt(k_ref[0, pl.ds(ks, tq), tile], q_ref[0, :, tile])
        for h in range(SB_HEADS):
            z = z_ref[h]
            fail = jnp.maximum(z, 0.0) + jnp.log2(1.0 + jnp.exp2(-jnp.abs(z)))
            z_ref[h] = z - fail
            if masked:
                fail = jnp.where(earlier, fail, 0.0)
            hi = fail.astype(BF16)
            split_ref[h, 0:tq, :] = hi
            split_ref[h, tq:, :] = (fail - hi.astype(F32)).astype(BF16)
            sum_ref[h] = jnp.sum(fail, axis=0, keepdims=True)
        for h in range(SB_HEADS):
            between = _dot(tri_ref[...], split_ref[h]) + right_ref[h]
            w = jnp.exp2(z_ref[h] - between)
            if masked:
                w = jnp.where(earlier, w, 0.0)
            w_ref[h] = w.astype(BF16)
            right_ref[h] += sum_ref[h]
        for h in range(SB_HEADS):
            pair = slice((h // 2) * HEAD_PAD, (h // 2 + 1) * HEAD_PAD)
            acc_ref[h] += _dot(vt_ref[0, pair, pl.ds(ks, tq)], w_ref[h])

    def body(i, carry):
        block(qi - 1 - i, False)
        return carry

    block(qi, True)
    lax.fori_loop(0, qi, body, 0)
    row = lax.broadcasted_iota(jnp.int32, (HEAD_PAD, tq), 0)
    for pair in range(SB_HEADS // 2):
        out_t = jnp.where(row < SB_DIM, acc_ref[2 * pair], acc_ref[2 * pair + 1])
        o_ref[0, :, pair * HEAD_PAD:(pair + 1) * HEAD_PAD] = out_t.T.astype(BF16)


def _sb_attention(q, k, vt):
    bsz, seq, _ = q.shape
    tq = min(seq, 256)
    sb_w = SB_HEADS * HEAD_PAD
    tri = (lax.broadcasted_iota(jnp.int32, (tq, tq), 1) > lax.broadcasted_iota(jnp.int32, (tq, tq), 0)).astype(BF16)
    tri2 = jnp.concatenate([tri, tri], axis=1)
    return pl.pallas_call(
        functools.partial(_sb_kernel, tq=tq),
        grid=(bsz, seq // tq),
        in_specs=[pl.BlockSpec((1, tq, sb_w), lambda b, i: (b, i, 0)),
                  pl.BlockSpec((1, seq, sb_w), lambda b, i: (b, 0, 0)),
                  pl.BlockSpec((1, SB_WIDTH, seq), lambda b, i: (b, 0, 0)),
                  pl.BlockSpec((tq, 2 * tq), lambda b, i: (0, 0))],
        out_specs=pl.BlockSpec((1, tq, SB_WIDTH), lambda b, i: (b, i, 0)),
        out_shape=jax.ShapeDtypeStruct((bsz, seq, SB_WIDTH), BF16),
        scratch_shapes=[pltpu.VMEM((SB_HEADS, 1, tq), F32), pltpu.VMEM((SB_HEADS, HEAD_PAD, tq), F32),
                        pltpu.VMEM((SB_HEADS, tq, tq), F32), pltpu.VMEM((SB_HEADS, 2 * tq, tq), BF16),
                        pltpu.VMEM((SB_HEADS, 1, tq), F32), pltpu.VMEM((SB_HEADS, tq, tq), BF16)],
        compiler_params=_params("parallel", "arbitrary"),
        name="sb_attention",
    )(q, k, vt, tri2)


def _ssm_kernel(u_ref, bbd_ref, lam_ref, cbd_ref, d_ref, wglu_ref, bglu_ref, y_ref, h_ref, hs_ref, *, tt, nb):
    @pl.when(pl.program_id(0) == 0)
    def _():
        h_ref[...] = jnp.zeros_like(h_ref)

    u = u_ref[...]
    hs_ref[...] = _dot(u, bbd_ref[...])
    lam_re = jnp.broadcast_to(lam_ref[0:1, :], (nb, SSM_STATES))
    lam_im = jnp.broadcast_to(lam_ref[1:2, :], (nb, SSM_STATES))

    def step(t, carry):
        h_re, h_im = carry
        r0 = pl.multiple_of(t * nb, nb)
        n_re = lam_re * h_re - lam_im * h_im + hs_ref[pl.ds(r0, nb), 0:SSM_STATES]
        n_im = lam_re * h_im + lam_im * h_re + hs_ref[pl.ds(r0, nb), SSM_STATES:]
        hs_ref[pl.ds(r0, nb), 0:SSM_STATES] = n_re
        hs_ref[pl.ds(r0, nb), SSM_STATES:] = n_im
        return n_re, n_im

    h_re, h_im = lax.fori_loop(0, tt, step, (h_ref[:, 0:SSM_STATES], h_ref[:, SSM_STATES:]))
    h_ref[:, 0:SSM_STATES] = h_re
    h_ref[:, SSM_STATES:] = h_im

    y = _dot(hs_ref[...].astype(BF16), cbd_ref[...]) + d_ref[...] * u.astype(F32)
    y = jax.nn.gelu(y)
    y = y * jax.nn.sigmoid(_dot(y.astype(BF16), wglu_ref[...]) + bglu_ref[...])
    y_ref[...] = y.astype(BF16)


def _ssm(u_tm, bbd, lam, cbd, d_row, wglu, bglu_row, nb):
    rows = u_tm.shape[0]
    seq = rows // nb
    tt = min(seq, 128)
    full = lambda a: pl.BlockSpec(a.shape, lambda i: (0,) * a.ndim)
    return pl.pallas_call(
        functools.partial(_ssm_kernel, tt=tt, nb=nb),
        grid=(seq // tt,),
        in_specs=[pl.BlockSpec((tt * nb, SSM_WIDTH), lambda i: (i, 0)),
                  full(bbd), full(lam), full(cbd), full(d_row), full(wglu), full(bglu_row)],
        out_specs=pl.BlockSpec((tt * nb, SSM_WIDTH), lambda i: (i, 0)),
        out_shape=jax.ShapeDtypeStruct((rows, SSM_WIDTH), BF16),
        scratch_shapes=[pltpu.VMEM((nb, 2 * SSM_STATES), F32),
                        pltpu.VMEM((tt * nb, 2 * SSM_STATES), F32)],
        compiler_params=_params("arbitrary"),
        name="ssm",
    )(u_tm, bbd, lam, cbd, d_row, wglu, bglu_row)


def _memkv_kernel(mem_ref, w_ref, k_ref, v_ref):
    kv = _dot(mem_ref[0].astype(BF16), w_ref[...])
    k_ref[0] = kv[:, :D_MODEL].astype(BF16)
    v_ref[0] = kv[:, D_MODEL:].astype(BF16)


def _memkv(mem, w_mkv):
    bsz, mlen, _ = mem.shape
    return pl.pallas_call(
        _memkv_kernel,
        grid=(bsz,),
        in_specs=[pl.BlockSpec((1, mlen, D_MODEL), lambda b: (b, 0, 0)),
                  pl.BlockSpec(w_mkv.shape, lambda b: (0, 0))],
        out_specs=[pl.BlockSpec((1, mlen, D_MODEL), lambda b: (b, 0, 0))] * 2,
        out_shape=[jax.ShapeDtypeStruct((bsz, mlen, D_MODEL), BF16)] * 2,
        compiler_params=_params("parallel"),
        name="mem_kv",
    )(mem, w_mkv)


def _mix_mem_kernel(ya_ref, ys_ref, yb_ref, x_ref, gmix_ref, wout_ref, ln1_ref,
                    mk_ref, mv_ref, wmq_ref, wmo_ref, ln2_ref, xt_ref, *, alpha):
    gm = gmix_ref[...]
    c1, c2 = MLA_QK_W, MLA_QK_W + SSM_WIDTH
    ya = (_rms_rows(ya_ref[0].astype(F32), MLA_V_W) * gm[:, 0:c1]).astype(BF16)
    ys = (_rms_rows(ys_ref[...].astype(F32), SSM_WIDTH) * gm[:, c1:c2]).astype(BF16)
    yb = (_rms_rows(yb_ref[0].astype(F32), SB_WIDTH) * gm[:, c2:]).astype(BF16)
    mix = _dot(ya, wout_ref[0:c1, :]) + _dot(ys, wout_ref[c1:c2, :]) + _dot(yb, wout_ref[c2:, :])
    x1 = _layer_norm_rows(alpha * x_ref[0] + mix, ln1_ref[0:1, :], ln1_ref[1:2, :])

    q = _dot(x1.astype(BF16), wmq_ref[...]).astype(BF16)
    heads = []
    for h in range(MEM_HEADS):
        lo, hi = h * MEM_HEAD_DIM, (h + 1) * MEM_HEAD_DIM
        s = _dot_nt(q[:, lo:hi], mk_ref[0, :, lo:hi])
        p = jnp.exp(s - jnp.max(s, axis=-1, keepdims=True))
        p = p / jnp.sum(p, axis=-1, keepdims=True)
        heads.append(_dot(p.astype(BF16), mv_ref[0, :, lo:hi]).astype(BF16))
    o = jnp.concatenate(heads, axis=-1)
    x2 = _layer_norm_rows(alpha * x1 + _dot(o, wmo_ref[...]), ln2_ref[0:1, :], ln2_ref[1:2, :])
    xt_ref[...] = x2.T


def _mix_mem(ya, ys_tm, yb, x, gmix, wout, ln1, mk, mv, wmq, wmo, ln2, alpha):
    bsz, seq, _ = x.shape
    ts = min(seq, 256)
    nst = seq // ts
    mlen = mk.shape[1]
    tok = lambda w: pl.BlockSpec((1, ts, w), lambda b, s: (b, s, 0))
    full = lambda a: pl.BlockSpec(a.shape, lambda b, s: (0,) * a.ndim)
    return pl.pallas_call(
        functools.partial(_mix_mem_kernel, alpha=alpha),
        grid=(bsz, nst),
        in_specs=[tok(MLA_QK_W), pl.BlockSpec((ts, SSM_WIDTH), lambda b, s: (s, b)), tok(SB_WIDTH), tok(D_MODEL),
                  full(gmix), full(wout), full(ln1),
                  pl.BlockSpec((1, mlen, D_MODEL), lambda b, s: (b, 0, 0)),
                  pl.BlockSpec((1, mlen, D_MODEL), lambda b, s: (b, 0, 0)),
                  full(wmq), full(wmo), full(ln2)],
        out_specs=pl.BlockSpec((D_MODEL, ts), lambda b, s: (0, b * nst + s)),
        out_shape=jax.ShapeDtypeStruct((D_MODEL, bsz * seq), F32),
        compiler_params=_params("parallel", "parallel"),
        name="mix_mem",
    )(ya, ys_tm, yb, x, gmix, wout, ln1, mk, mv, wmq, wmo, ln2)


def _sort16_desc(v):
    v = list(v)
    for k in (2, 4, 8, 16):
        j = k // 2
        while j >= 1:
            for i in range(16):
                l = i ^ j
                if l > i:
                    hi, lo = jnp.maximum(v[i], v[l]), jnp.minimum(v[i], v[l])
                    v[i], v[l] = (hi, lo) if (i & k) == 0 else (lo, hi)
            j //= 2
    return v


def _merge_bitonic_desc(v):
    v = list(v)
    for j in (8, 4, 2, 1):
        for i in range(16):
            l = i ^ j
            if l > i:
                v[i], v[l] = jnp.maximum(v[i], v[l]), jnp.minimum(v[i], v[l])
    return v


def _top16_sorted(rows):
    v = _sort16_desc(rows)
    for shift in (4, 2, 1):
        partner = [pltpu.roll(x, shift, axis=0) for x in v]
        v = _merge_bitonic_desc([jnp.maximum(v[i], partner[15 - i]) for i in range(16)])
    return v


def _count_prefix(pred, v):
    p8 = pred(v[7])
    p4 = pred(jnp.where(p8, v[11], v[3]))
    p2 = pred(jnp.where(p8, jnp.where(p4, v[13], v[9]), jnp.where(p4, v[5], v[1])))
    p1 = pred(jnp.where(p8, jnp.where(p4, jnp.where(p2, v[14], v[12]), jnp.where(p2, v[10], v[8])),
                        jnp.where(p4, jnp.where(p2, v[6], v[4]), jnp.where(p2, v[2], v[0]))))
    count = (jnp.where(p8, 8.0, 0.0) + jnp.where(p4, 4.0, 0.0)) + (jnp.where(p2, 2.0, 0.0) + jnp.where(p1, 1.0, 0.0))
    return jnp.where(pred(v[15]), 16.0, count)


def _peer_route_kernel(xt_ref, wpq_ref, keys_ref, a_ref, n_ref, b_ref, r_ref, s1_ref, s2_ref):
    xb = xt_ref[...].astype(BF16)
    q = _dot(wpq_ref[...], xb).astype(BF16)
    s1_ref[...] = _dot(keys_ref[0, 0], q[0:PEER_KEY_DIM, :])
    s2_ref[...] = _dot(keys_ref[0, 1], q[PEER_KEY_DIM:, :])
    sub = lax.broadcasted_iota(jnp.int32, (8, LANE), 0)
    groups = PEER_N_KEYS // 8

    def spread(vals):
        out = vals[7]
        for j in range(6, -1, -1):
            out = jnp.where(sub == j, vals[j], out)
        return out

    def chunk(c, carry):
        ln = pl.ds(pl.multiple_of(c * LANE, LANE), LANE)
        rows1 = [s1_ref[8 * i:8 * i + 8, ln] for i in range(groups)]
        rows2 = [s2_ref[8 * i:8 * i + 8, ln] for i in range(groups)]
        v1 = _top16_sorted(rows1)
        v2 = _top16_sorted(rows2)
        v2lo, v2hi, v1hi = spread(v2[:8]), spread(v2[8:]), spread(v1[8:])
        cands = ([v1[0] + v2lo, v1[0] + v2hi] + [v1[r] + v2lo for r in range(1, 8)] + [v1hi + v2[0]]
                 + [v1[r] + v2hi for r in range(1, 7)])
        top = _top16_sorted(cands)
        thr = top[15]
        z = jnp.exp(top[0] - top[0])
        for k in range(1, PEER_TOPK):
            z = z + jnp.exp(top[k] - top[0])
        inv_z = 1.0 / z
        for i in range(groups):
            s = rows1[i]
            kept = _count_prefix(lambda t, s=s: s + t >= thr, v2)
            in_top = s >= v1[15]
            a_ref[i, 0, :, ln] = jnp.where(in_top, jnp.exp(s - v1[0]), 0.0)
            n_ref[i, 0, :, ln] = jnp.where(in_top, kept, 0.0)
        for i in range(groups // 2):
            bs, rs = [], []
            for s in (rows2[2 * i], rows2[2 * i + 1]):
                rank = _count_prefix(lambda t, s=s: t > s, v2)
                rs.append(rank)
                bs.append(jnp.where(rank < float(PEER_TOPK), jnp.exp(s - v2[0]) * inv_z, 0.0))
            b_ref[0, 16 * i:16 * i + 16, ln] = jnp.concatenate(bs, axis=0).astype(BF16)
            r_ref[0, 16 * i:16 * i + 16, ln] = jnp.concatenate(rs, axis=0).astype(BF16)
        return carry

    lax.fori_loop(0, xt_ref.shape[1] // LANE, chunk, 0)


def _peer_route(xt, wpq_t, keys):
    ntok = xt.shape[1]
    tt = min(ntok, 512)
    nblk = PEER_N_KEYS // 8
    return pl.pallas_call(
        _peer_route_kernel,
        grid=(ntok // tt, PEER_HEADS),
        in_specs=[pl.BlockSpec((D_MODEL, tt), lambda i, h: (0, i)),
                  pl.BlockSpec((2 * PEER_KEY_DIM, D_MODEL), lambda i, h: (h, 0)),
                  pl.BlockSpec((1, 2, PEER_N_KEYS, PEER_KEY_DIM), lambda i, h: (h, 0, 0, 0))],
        out_specs=[pl.BlockSpec((nblk, 1, 8, tt), lambda i, h: (0, h, 0, i)),
                   pl.BlockSpec((nblk, 1, 8, tt), lambda i, h: (0, h, 0, i)),
                   pl.BlockSpec((1, PEER_N_KEYS, tt), lambda i, h: (h, 0, i)),
                   pl.BlockSpec((1, PEER_N_KEYS, tt), lambda i, h: (h, 0, i))],
        out_shape=[jax.ShapeDtypeStruct((nblk, PEER_HEADS, 8, ntok), F32),
                   jax.ShapeDtypeStruct((nblk, PEER_HEADS, 8, ntok), F32),
                   jax.ShapeDtypeStruct((PEER_HEADS, PEER_N_KEYS, ntok), BF16),
                   jax.ShapeDtypeStruct((PEER_HEADS, PEER_N_KEYS, ntok), BF16)],
        scratch_shapes=[pltpu.VMEM((PEER_N_KEYS, tt), F32), pltpu.VMEM((PEER_N_KEYS, tt), F32)],
        compiler_params=_params("parallel", "arbitrary"),
        name="peer_route",
    )(xt, wpq_t, keys)


PEER_I1_PER_TILE = 8
PEER_TILE = PEER_I1_PER_TILE * PEER_N_KEYS


def _peer_dense_kernel(xt_ref, a_ref, n_ref, b_ref, r_ref, u_ref, vt_ref, ln_ref, o_ref,
                       acc_ref, xb_ref, gh_ref, *, alpha):
    e = pl.program_id(1)

    @pl.when(e == 0)
    def _():
        acc_ref[...] = jnp.zeros_like(acc_ref)
        xb_ref[...] = xt_ref[...].astype(BF16)

    tt = xb_ref.shape[1]
    rows = BF16_SUBLANES
    gh_ref[...] = _dot(u_ref[...], xb_ref[...]).astype(BF16)
    cw = min(tt, 2 * LANE)
    for j in range(PEER_I1_PER_TILE):
        for c in range(tt // cw):
            ln = slice(c * cw, (c + 1) * cw)
            a_rows = [jnp.broadcast_to(a_ref[0, h, j:j + 1, ln], (rows, cw)).astype(BF16) for h in range(PEER_HEADS)]
            n_rows = [jnp.broadcast_to(n_ref[0, h, j:j + 1, ln], (rows, cw)).astype(BF16) for h in range(PEER_HEADS)]
            for g in range(PEER_N_KEYS // rows):
                i2 = slice(g * rows, (g + 1) * rows)
                gate = None
                for h in range(PEER_HEADS):
                    term = jnp.where(r_ref[h, i2, ln] < n_rows[h], b_ref[h, i2, ln] * a_rows[h], jnp.zeros((), BF16))
                    gate = term if gate is None else gate + term
                e0 = j * PEER_N_KEYS + g * rows
                gh_ref[e0:e0 + rows, ln] = gate * _gelu_tanh(gh_ref[e0:e0 + rows, ln])
    acc_ref[...] += _dot(vt_ref[...], gh_ref[...])

    @pl.when(e == pl.num_programs(1) - 1)
    def _():
        y = alpha * xt_ref[...] + acc_ref[...]
        mu = jnp.mean(y, axis=0, keepdims=True)
        c = y - mu
        var = jnp.mean(c * c, axis=0, keepdims=True)
        o_ref[...] = (c * lax.rsqrt(var + NORM_EPS)).T * ln_ref[0:1, :] + ln_ref[1:2, :]


def _peer_dense(xt, a, n, b, r, u, vt, ln, alpha):
    ntok = xt.shape[1]
    tt = min(ntok, 1024)
    ntile = PEER_EXPERTS // PEER_TILE
    return pl.pallas_call(
        functools.partial(_peer_dense_kernel, alpha=alpha),
        grid=(ntok // tt, ntile),
        in_specs=[pl.BlockSpec((D_MODEL, tt), lambda i, e: (0, i)),
                  pl.BlockSpec((1, PEER_HEADS, PEER_I1_PER_TILE, tt), lambda i, e: (e, 0, 0, i)),
                  pl.BlockSpec((1, PEER_HEADS, PEER_I1_PER_TILE, tt), lambda i, e: (e, 0, 0, i)),
                  pl.BlockSpec((PEER_HEADS, PEER_N_KEYS, tt), lambda i, e: (0, 0, i)),
                  pl.BlockSpec((PEER_HEADS, PEER_N_KEYS, tt), lambda i, e: (0, 0, i)),
                  pl.BlockSpec((PEER_TILE, D_MODEL), lambda i, e: (e, 0)),
                  pl.BlockSpec((D_MODEL, PEER_TILE), lambda i, e: (0, e)),
                  pl.BlockSpec(ln.shape, lambda i, e: (0, 0))],
        out_specs=pl.BlockSpec((tt, D_MODEL), lambda i, e: (i, 0)),
        out_shape=jax.ShapeDtypeStruct((ntok, D_MODEL), F32),
        scratch_shapes=[pltpu.VMEM((D_MODEL, tt), F32), pltpu.VMEM((D_MODEL, tt), BF16),
                        pltpu.VMEM((PEER_TILE, tt), BF16)],
        compiler_params=_params("parallel", "arbitrary"),
        name="peer_dense",
    )(xt, a, n, b, r, u, vt, ln)


def _pad_heads(w, heads, width):
    rows = w.shape[0]
    w = w.reshape(rows, heads, width)
    return jnp.pad(w, ((0, 0), (0, 0), (0, HEAD_PAD - width))).reshape(rows, heads * HEAD_PAD)


def _rotate_half_cols(w, heads, width, nope):
    rows = w.shape[0]
    w = w.reshape(rows, heads, width)
    half = (width - nope) // 2
    x1, x2 = w[..., nope:nope + half], w[..., nope + half:]
    out = jnp.concatenate([jnp.zeros_like(w[..., :nope]), -x2, x1], axis=-1)
    return out.reshape(rows, heads * width)


def _pack_inproj(w_in, g_cq, g_ckv, w_uq, w_ukv):
    c = 0
    cols = {}
    for name, width in (("cq", MLA_Q_RANK), ("ckv", MLA_KV_RANK), ("kr", MLA_ROPE), ("ssm", SSM_WIDTH),
                        ("qsb", SB_WIDTH), ("ksb", SB_WIDTH), ("vsb", SB_WIDTH)):
        cols[name] = w_in[:, c:c + width]
        c += width
    rows = w_in.shape[0]
    zeros = lambda n: jnp.zeros((rows, n), F32)
    kr = cols["kr"]
    half = MLA_ROPE // 2
    kr_plain = jnp.concatenate([zeros(MLA_NOPE), kr, zeros(HEAD_PAD - MLA_NOPE - MLA_ROPE)], axis=1)
    kr_swap = jnp.concatenate([zeros(MLA_NOPE), -kr[:, half:], kr[:, :half],
                               zeros(HEAD_PAD - MLA_NOPE - MLA_ROPE)], axis=1)
    sb_scale = SB_DIM ** -0.5 * LOG2E
    win = jnp.concatenate([cols["cq"], cols["ckv"], kr_plain, kr_swap, cols["ssm"],
                           _pad_heads(cols["qsb"] * sb_scale, SB_HEADS, SB_DIM),
                           _pad_heads(cols["ksb"], SB_HEADS, SB_DIM), cols["vsb"]], axis=1).astype(BF16)

    qk_dim = MLA_NOPE + MLA_ROPE
    wq = w_uq * (g_cq[:, None] * (qk_dim ** -0.5 * LOG2E))
    wuq = jnp.concatenate([_pad_heads(wq, MLA_HEADS, qk_dim),
                           _pad_heads(_rotate_half_cols(wq, MLA_HEADS, qk_dim, MLA_NOPE), MLA_HEADS, qk_dim)],
                          axis=1).astype(BF16)
    wkv = (w_ukv * g_ckv[:, None]).reshape(MLA_KV_RANK, MLA_HEADS, MLA_NOPE + MLA_V)
    wukv = jnp.concatenate([_pad_heads(wkv[..., :MLA_NOPE].reshape(MLA_KV_RANK, -1), MLA_HEADS, MLA_NOPE),
                            _pad_heads(wkv[..., MLA_NOPE:].reshape(MLA_KV_RANK, -1), MLA_HEADS, MLA_V)],
                           axis=1).astype(BF16)
    return win, wuq, wukv


def _pad_mla_rows(w):
    cols = w.shape[1]
    head_rows = jnp.pad(w[:MLA_V_W].reshape(MLA_HEADS, MLA_V, cols), ((0, 0), (0, HEAD_PAD - MLA_V), (0, 0)))
    return jnp.concatenate([head_rows.reshape(MLA_QK_W, cols), w[MLA_V_W:]], axis=0)


def _pack_ssm(lam_re, lam_im, log_step, b_re, b_im, c_re, c_im, d_skip):
    step = jnp.exp(log_step)[:, None]
    decay = jnp.exp(lam_re * step)
    ab_re, ab_im = decay * jnp.cos(lam_im * step), decay * jnp.sin(lam_im * step)
    inv = 1.0 / (lam_re * lam_re + lam_im * lam_im)
    f_re = ((ab_re - 1.0) * lam_re + ab_im * lam_im) * inv
    f_im = (ab_im * lam_re - (ab_re - 1.0) * lam_im) * inv
    bb_re = f_re[..., None] * b_re - f_im[..., None] * b_im
    bb_im = f_re[..., None] * b_im + f_im[..., None] * b_re
    eye = jnp.eye(SSM_GROUPS, dtype=F32)
    blk = lambda w: jnp.einsum("gph,gk->ghkp", w, eye).reshape(SSM_WIDTH, SSM_STATES)
    bbd = jnp.concatenate([blk(bb_re), blk(bb_im)], axis=1).astype(BF16)
    blk_c = lambda w: jnp.einsum("ghp,gk->gpkh", w, eye).reshape(SSM_STATES, SSM_WIDTH)
    cbd = jnp.concatenate([blk_c(c_re), blk_c(-c_im)], axis=0).astype(BF16)
    lam = jnp.stack([ab_re.reshape(-1), ab_im.reshape(-1)])
    return bbd, lam, cbd, d_skip.reshape(1, SSM_WIDTH)


def kernel(x, mem, positions, w_in, g_cq, g_ckv, w_uq, w_ukv, ssm_lam_re, ssm_lam_im, ssm_log_step, ssm_b_re, ssm_b_im, ssm_c_re, ssm_c_im, ssm_d, w_glu, b_glu, g_mix, w_out, ln_mix_g, ln_mix_b, w_mq, w_mkv, w_mo, ln_mem_g, ln_mem_b, w_pq, peer_sub_keys, peer_u, peer_v, ln_ffn_g, ln_ffn_b):
    bsz, seq, _ = x.shape
    depth = w_in.shape[0]
    alpha = (2 * depth) ** 0.25
    cos_t, sin_t = _rope_tables(positions)
    for l in range(depth):
        win, wuq, wukv = _pack_inproj(w_in[l], g_cq[l], g_ckv[l], w_uq[l], w_ukv[l])
        q, k, v, u_tm, q_sb, k_sb, v_sb = _inproj(x, win, wuq, wukv, cos_t, sin_t)
        y_mla = _mla_attention(q, k, v)
        y_sb = _sb_attention(q_sb, k_sb, v_sb)
        bbd, lam, cbd, d_row = _pack_ssm(ssm_lam_re[l], ssm_lam_im[l], ssm_log_step[l], ssm_b_re[l], ssm_b_im[l],
                                         ssm_c_re[l], ssm_c_im[l], ssm_d[l])
        y_ssm = _ssm(u_tm.reshape(seq * bsz, SSM_WIDTH), bbd, lam, cbd, d_row,
                     w_glu[l].astype(BF16), b_glu[l].reshape(1, SSM_WIDTH), bsz)
        mk, mv = _memkv(mem, w_mkv[l].astype(BF16))
        xt = _mix_mem(y_mla, y_ssm.reshape(seq, bsz * SSM_WIDTH), y_sb, x,
                      _pad_mla_rows(g_mix[l][:, None]).reshape(1, -1), _pad_mla_rows(w_out[l]).astype(BF16),
                      jnp.stack([ln_mix_g[l], ln_mix_b[l]]),
                      mk, mv, (w_mq[l] * MEM_HEAD_DIM ** -0.5).astype(BF16), w_mo[l].astype(BF16),
                      jnp.stack([ln_mem_g[l], ln_mem_b[l]]), alpha)
        a, n, b, r = _peer_route(xt, w_pq[l].T.astype(BF16), peer_sub_keys[l].astype(BF16))
        x = _peer_dense(xt, a, n, b, r, peer_u[l].astype(BF16), peer_v[l].T.astype(BF16),
                        jnp.stack([ln_ffn_g[l], ln_ffn_b[l]]), alpha).reshape(bsz, seq, D_MODEL)
    return x
```

```python
import functools
import math

import jax
import jax.numpy as jnp
from jax import lax
from jax.experimental import pallas as pl
from jax.experimental.pallas import tpu as pltpu

F32 = jnp.float32
BF16 = jnp.bfloat16

D_MODEL = 1024
CHUNK = 64
NORM_EPS = 1e-5
NEG_BIG = -1e30
LOG2E = math.log2(math.e)

MLA_HEADS = 6
MLA_NOPE = 64
MLA_ROPE = 32
MLA_V = 64
MLA_Q_RANK = 256
MLA_KV_RANK = 128
ROPE_THETA = 10000.0

SSM_GROUPS = 24
SSM_CH = 16
SSM_STATE = 64
SSM_WIDTH = SSM_GROUPS * SSM_CH
SSM_STATES = SSM_GROUPS * SSM_STATE
SSM_BLOCKS = 3
SSM_BLOCK_CH = SSM_WIDTH // SSM_BLOCKS
SSM_BLOCK_STATES = SSM_STATES // SSM_BLOCKS

SB_HEADS = 4
SB_DIM = 64
SB_WIDTH = SB_HEADS * SB_DIM

MEM_HEADS = 4
MEM_HEAD_DIM = D_MODEL // MEM_HEADS

PEER_HEADS = 8
PEER_N_KEYS = 128
PEER_TOPK = 16
PEER_KEY_DIM = 128
PEER_EXPERTS = PEER_N_KEYS * PEER_N_KEYS

LANE = 128
BF16_SUBLANES = 16
HEAD_PAD = 128

VMEM_LIMIT = 56 * 1024 * 1024

_C_CQ = 0
_C_CKV = _C_CQ + MLA_Q_RANK
_C_KRP = _C_CKV + MLA_KV_RANK
_C_KRS = _C_KRP + HEAD_PAD
_C_SSM = _C_KRS + HEAD_PAD
_C_QSB = _C_SSM + SSM_WIDTH
_C_KSB = _C_QSB + SB_HEADS * HEAD_PAD
_C_VSB = _C_KSB + SB_HEADS * HEAD_PAD
_C_END = _C_VSB + SB_WIDTH
MLA_QK_W = MLA_HEADS * HEAD_PAD
MLA_V_W = MLA_HEADS * MLA_V


def _params(*sem):
    return pltpu.CompilerParams(dimension_semantics=sem, vmem_limit_bytes=VMEM_LIMIT)


def _dot(a, b):
    return jnp.dot(a, b, preferred_element_type=F32)


def _dot_nt(a, b):
    return lax.dot_general(a, b, (((1,), (1,)), ((), ())), preferred_element_type=F32)


def _gelu_tanh(x):
    c = math.sqrt(2.0 / math.pi)
    inner = x * (c + (c * 0.044715) * (x * x))
    return (0.5 * x) * (1.0 + jnp.tanh(inner))


def _layer_norm_rows(v, g, b):
    mu = jnp.mean(v, axis=-1, keepdims=True)
    c = v - mu
    var = jnp.mean(c * c, axis=-1, keepdims=True)
    return c * lax.rsqrt(var + NORM_EPS) * g + b


def _rms_rows(v, width):
    return v * lax.rsqrt(jnp.sum(v * v, axis=-1, keepdims=True) * (1.0 / width) + NORM_EPS)


def _rope_kernel(pos_ref, freq_ref, cos_ref, sin_ref):
    ang = pos_ref[0].astype(F32) * freq_ref[...]
    lane = lax.broadcasted_iota(jnp.int32, ang.shape, 1)
    rot = (lane >= MLA_NOPE) & (lane < MLA_NOPE + MLA_ROPE)
    cos_ref[0] = jnp.where(rot, jnp.cos(ang), jnp.where(lane < MLA_NOPE, 1.0, 0.0))
    sin_ref[0] = jnp.where(rot, jnp.sin(ang), 0.0)


def _rope_tables(positions):
    bsz, seq = positions.shape
    ts = min(seq, 512)
    half = MLA_ROPE // 2
    freq = ROPE_THETA ** (-jnp.arange(half, dtype=F32) / half)
    freq_row = jnp.zeros((1, HEAD_PAD), F32).at[0, MLA_NOPE:MLA_NOPE + MLA_ROPE].set(jnp.tile(freq, 2))
    return pl.pallas_call(
        _rope_kernel,
        grid=(bsz, seq // ts),
        in_specs=[pl.BlockSpec((1, ts, 1), lambda b, s: (b, s, 0)),
                  pl.BlockSpec((1, HEAD_PAD), lambda b, s: (0, 0))],
        out_specs=[pl.BlockSpec((1, ts, HEAD_PAD), lambda b, s: (b, s, 0))] * 2,
        out_shape=[jax.ShapeDtypeStruct((bsz, seq, HEAD_PAD), F32)] * 2,
        compiler_params=_params("parallel", "parallel"),
        name="rope_tables",
    )(positions.reshape(bsz, seq, 1), freq_row)


def _inproj_kernel(x_ref, win_ref, wuq_ref, wukv_ref, cos_ref, sin_ref, ones_ref,
                   q_ref, k_ref, v_ref, u_ref, qs_ref, ks_ref, vs_ref):
    xb = x_ref[0].astype(BF16)
    acc = _dot(xb, win_ref[...])
    cos = cos_ref[0]
    sin = sin_ref[0]
    cqn = _rms_rows(acc[:, _C_CQ:_C_CKV], MLA_Q_RANK).astype(BF16)
    qq = _dot(cqn, wuq_ref[...])
    ckvn = _rms_rows(acc[:, _C_CKV:_C_KRP], MLA_KV_RANK).astype(BF16)
    kv = _dot(ckvn, wukv_ref[...])
    k_rope = acc[:, _C_KRP:_C_KRS] * cos + acc[:, _C_KRS:_C_SSM] * sin
    for h in range(MLA_HEADS):
        lo, hi = h * HEAD_PAD, (h + 1) * HEAD_PAD
        q_ref[0, :, lo:hi] = (qq[:, lo:hi] * cos + qq[:, MLA_QK_W + lo:MLA_QK_W + hi] * sin).astype(BF16)
        k_ref[0, :, lo:hi] = (kv[:, lo:hi] + k_rope).astype(BF16)
    v_ref[0] = (kv[:, MLA_QK_W:] + ones_ref[...]).T.astype(BF16)
    u_ref[...] = acc[:, _C_SSM:_C_QSB].astype(BF16)
    qs_ref[0] = acc[:, _C_QSB:_C_KSB].astype(BF16)
    ks_ref[0] = acc[:, _C_KSB:_C_VSB].astype(BF16)
    vs_ref[0] = acc[:, _C_VSB:_C_END].T.astype(BF16)


def _inproj(x, win, wuq, wukv, cos_t, sin_t):
    bsz, seq, _ = x.shape
    ts = min(seq, 512)
    tok = lambda w: pl.BlockSpec((1, ts, w), lambda b, s: (b, s, 0))
    tok_t = lambda w: pl.BlockSpec((1, w, ts), lambda b, s: (b, 0, s))
    full = lambda a: pl.BlockSpec(a.shape, lambda b, s: (0,) * a.ndim)
    sb_w = SB_HEADS * HEAD_PAD
    lane = jnp.arange(MLA_QK_W) % HEAD_PAD
    ones_row = (lane == MLA_V).astype(F32).reshape(1, MLA_QK_W)
    return pl.pallas_call(
        _inproj_kernel,
        grid=(bsz, seq // ts),
        in_specs=[tok(D_MODEL), full(win), full(wuq), full(wukv), tok(HEAD_PAD), tok(HEAD_PAD), full(ones_row)],
        out_specs=[tok(MLA_QK_W), tok(MLA_QK_W), tok_t(MLA_QK_W),
                   pl.BlockSpec((ts, SSM_WIDTH), lambda b, s: (s, b)),
                   tok(sb_w), tok(sb_w), tok_t(SB_WIDTH)],
        out_shape=[jax.ShapeDtypeStruct((bsz, seq, MLA_QK_W), BF16),
                   jax.ShapeDtypeStruct((bsz, seq, MLA_QK_W), BF16),
                   jax.ShapeDtypeStruct((bsz, MLA_QK_W, seq), BF16),
                   jax.ShapeDtypeStruct((seq, bsz * SSM_WIDTH), BF16),
                   jax.ShapeDtypeStruct((bsz, seq, sb_w), BF16),
                   jax.ShapeDtypeStruct((bsz, seq, sb_w), BF16),
                   jax.ShapeDtypeStruct((bsz, SB_WIDTH, seq), BF16)],
        compiler_params=_params("parallel", "parallel"),
        name="inproj",
    )(x, win, wuq, wukv, cos_t, sin_t, ones_row)


def _mla_kernel(q_ref, k_ref, vt_ref, o_ref, m_ref, acc_ref, s_ref, p_ref, *, tq):
    qi = pl.program_id(1)
    key_chunk = lax.broadcasted_iota(jnp.int32, (tq, tq), 0) // CHUNK
    query_chunk = lax.broadcasted_iota(jnp.int32, (tq, tq), 1) // CHUNK
    allowed = key_chunk <= query_chunk
    m_ref[...] = jnp.full(m_ref.shape, NEG_BIG, F32)
    acc_ref[...] = jnp.zeros(acc_ref.shape, F32)

    def block(kb, masked):
        ks = pl.multiple_of(kb * tq, tq)
        for h in range(MLA_HEADS):
            tile = slice(h * HEAD_PAD, (h + 1) * HEAD_PAD)
            s_ref[h] = _dot_nt(k_ref[0, pl.ds(ks, tq), tile], q_ref[0, :, tile])
        alphas = []
        for h in range(MLA_HEADS):
            s = s_ref[h]
            if masked:
                s = jnp.where(allowed, s, NEG_BIG)
            m_old = m_ref[h]
            m_new = jnp.maximum(m_old, jnp.max(s, axis=0, keepdims=True))
            p_ref[h] = jnp.exp2(s - m_new).astype(BF16)
            alphas.append(jnp.exp2(m_old - m_new))
            m_ref[h] = m_new
        for h in range(MLA_HEADS):
            tile = slice(h * HEAD_PAD, (h + 1) * HEAD_PAD)
            acc_ref[h] = alphas[h] * acc_ref[h] + _dot(vt_ref[0, tile, pl.ds(ks, tq)], p_ref[h])

    def body(kb, carry):
        block(kb, False)
        return carry

    lax.fori_loop(0, qi, body, 0)
    block(qi, True)
    row = lax.broadcasted_iota(jnp.int32, (HEAD_PAD, tq), 0)
    for h in range(MLA_HEADS):
        acc = acc_ref[h]
        out_t = jnp.where(row < MLA_V, acc / acc[MLA_V:MLA_V + 1, :], 0.0)
        o_ref[0, :, h * HEAD_PAD:(h + 1) * HEAD_PAD] = out_t.T.astype(BF16)


def _mla_attention(q, k, vt):
    bsz, seq, _ = q.shape
    tq = min(seq, 256)
    return pl.pallas_call(
        functools.partial(_mla_kernel, tq=tq),
        grid=(bsz, seq // tq),
        in_specs=[pl.BlockSpec((1, tq, MLA_QK_W), lambda b, i: (b, i, 0)),
                  pl.BlockSpec((1, seq, MLA_QK_W), lambda b, i: (b, 0, 0)),
                  pl.BlockSpec((1, MLA_QK_W, seq), lambda b, i: (b, 0, 0))],
        out_specs=pl.BlockSpec((1, tq, MLA_QK_W), lambda b, i: (b, i, 0)),
        out_shape=jax.ShapeDtypeStruct((bsz, seq, MLA_QK_W), BF16),
        scratch_shapes=[pltpu.VMEM((MLA_HEADS, 1, tq), F32), pltpu.VMEM((MLA_HEADS, HEAD_PAD, tq), F32),
                        pltpu.VMEM((MLA_HEADS, tq, tq), F32), pltpu.VMEM((MLA_HEADS, tq, tq), BF16)],
        compiler_params=_params("parallel", "arbitrary"),
        name="mla_attention",
    )(q, k, vt)


def _sb_kernel(q_ref, k_ref, vt_ref, tri_ref, o_ref, right_ref, acc_ref, z_ref, split_ref, sum_ref, w_ref, *, tq):
    qi = pl.program_id(1)
    earlier = lax.broadcasted_iota(jnp.int32, (tq, tq), 0) < lax.broadcasted_iota(jnp.int32, (tq, tq), 1)
    right_ref[...] = jnp.zeros(right_ref.shape, F32)
    acc_ref[...] = jnp.zeros(acc_ref.shape, F32)

    def block(kb, masked):
        ks = pl.multiple_of(kb * tq, tq)
        for h in range(SB_HEADS):
            tile = slice(h * HEAD_PAD, (h + 1) * HEAD_PAD)
            z_ref[h] = _dot_nt(k_ref[0, pl.ds(ks, tq), tile], q_ref[0, :, tile])
        for h in range(SB_HEADS):
            z = z_ref[h]
            fail = jnp.maximum(z, 0.0) + jnp.log2(1.0 + jnp.exp2(-jnp.abs(z)))
            z_ref[h] = z - fail
            if masked:
                fail = jnp.where(earlier, fail, 0.0)
            hi = fail.astype(BF16)
            split_ref[h, 0:tq, :] = hi
            split_ref[h, tq:, :] = (fail - hi.astype(F32)).astype(BF16)
            sum_ref[h] = jnp.sum(fail, axis=0, keepdims=True)
        for h in range(SB_HEADS):
            between = _dot(tri_ref[...], split_ref[h]) + right_ref[h]
            w = jnp.exp2(z_ref[h] - between)
            if masked:
                w = jnp.where(earlier, w, 0.0)
            w_ref[h] = w.astype(BF16)
            right_ref[h] += sum_ref[h]
        for h in range(SB_HEADS):
            pair = slice((h // 2) * HEAD_PAD, (h // 2 + 1) * HEAD_PAD)
            acc_ref[h] += _dot(vt_ref[0, pair, pl.ds(ks, tq)], w_ref[h])

    def body(i, carry):
        block(qi - 1 - i, False)
        return carry

    block(qi, True)
    lax.fori_loop(0, qi, body, 0)
    row = lax.broadcasted_iota(jnp.int32, (HEAD_PAD, tq), 0)
    for pair in range(SB_HEADS // 2):
        out_t = jnp.where(row < SB_DIM, acc_ref[2 * pair], acc_ref[2 * pair + 1])
        o_ref[0, :, pair * HEAD_PAD:(pair + 1) * HEAD_PAD] = out_t.T.astype(BF16)


def _sb_attention(q, k, vt):
    bsz, seq, _ = q.shape
    tq = min(seq, 256)
    sb_w = SB_HEADS * HEAD_PAD
    tri = (lax.broadcasted_iota(jnp.int32, (tq, tq), 1) > lax.broadcasted_iota(jnp.int32, (tq, tq), 0)).astype(BF16)
    tri2 = jnp.concatenate([tri, tri], axis=1)
    return pl.pallas_call(
        functools.partial(_sb_kernel, tq=tq),
        grid=(bsz, seq // tq),
        in_specs=[pl.BlockSpec((1, tq, sb_w), lambda b, i: (b, i, 0)),
                  pl.BlockSpec((1, seq, sb_w), lambda b, i: (b, 0, 0)),
                  pl.BlockSpec((1, SB_WIDTH, seq), lambda b, i: (b, 0, 0)),
                  pl.BlockSpec((tq, 2 * tq), lambda b, i: (0, 0))],
        out_specs=pl.BlockSpec((1, tq, SB_WIDTH), lambda b, i: (b, i, 0)),
        out_shape=jax.ShapeDtypeStruct((bsz, seq, SB_WIDTH), BF16),
        scratch_shapes=[pltpu.VMEM((SB_HEADS, 1, tq), F32), pltpu.VMEM((SB_HEADS, HEAD_PAD, tq), F32),
                        pltpu.VMEM((SB_HEADS, tq, tq), F32), pltpu.VMEM((SB_HEADS, 2 * tq, tq), BF16),
                        pltpu.VMEM((SB_HEADS, 1, tq), F32), pltpu.VMEM((SB_HEADS, tq, tq), BF16)],
        compiler_params=_params("parallel", "arbitrary"),
        name="sb_attention",
    )(q, k, vt, tri2)


def _ssm_kernel(u_ref, bbd_ref, lam_ref, cbd_ref, d_ref, wglu_ref, bglu_ref, y_ref, h_ref, hs_ref, *, tt, nb):
    @pl.when(pl.program_id(0) == 0)
    def _():
        h_ref[...] = jnp.zeros_like(h_ref)

    u = u_ref[...]
    half, wide = SSM_BLOCK_STATES, 2 * SSM_BLOCK_STATES
    for b in range(SSM_BLOCKS):
        hs_ref[:, b * wide:(b + 1) * wide] = _dot(u[:, b * SSM_BLOCK_CH:(b + 1) * SSM_BLOCK_CH], bbd_ref[b])
    lam_re = jnp.broadcast_to(lam_ref[0:1, :], (nb, SSM_STATES))
    lam_im = jnp.broadcast_to(lam_ref[1:2, :], (nb, SSM_STATES))

    def step(t, carry):
        r0 = pl.multiple_of(t * nb, nb)
        out = []
        for b in range(SSM_BLOCKS):
            h_re, h_im = carry[b]
            l_re, l_im = lam_re[:, b * half:(b + 1) * half], lam_im[:, b * half:(b + 1) * half]
            re_cols = slice(b * wide, b * wide + half)
            im_cols = slice(b * wide + half, (b + 1) * wide)
            n_re = l_re * h_re - l_im * h_im + hs_ref[pl.ds(r0, nb), re_cols]
            n_im = l_re * h_im + l_im * h_re + hs_ref[pl.ds(r0, nb), im_cols]
            hs_ref[pl.ds(r0, nb), re_cols] = n_re
            hs_ref[pl.ds(r0, nb), im_cols] = n_im
            out.append((n_re, n_im))
        return tuple(out)

    init = tuple((h_ref[:, b * wide:b * wide + half], h_ref[:, b * wide + half:(b + 1) * wide]) for b in range(SSM_BLOCKS))
    final = lax.fori_loop(0, tt, step, init)
    for b in range(SSM_BLOCKS):
        h_ref[:, b * wide:b * wide + half] = final[b][0]
        h_ref[:, b * wide + half:(b + 1) * wide] = final[b][1]

    y = jnp.concatenate([_dot(hs_ref[:, b * wide:(b + 1) * wide].astype(BF16), cbd_ref[b]) for b in range(SSM_BLOCKS)],
                        axis=1) + d_ref[...] * u.astype(F32)
    y = jax.nn.gelu(y)
    y = y * jax.nn.sigmoid(_dot(y.astype(BF16), wglu_ref[...]) + bglu_ref[...])
    y_ref[...] = y.astype(BF16)


def _ssm(u_tm, bbd, lam, cbd, d_row, wglu, bglu_row, nb):
    rows = u_tm.shape[0]
    seq = rows // nb
    tt = min(seq, 128)
    full = lambda a: pl.BlockSpec(a.shape, lambda i: (0,) * a.ndim)
    return pl.pallas_call(
        functools.partial(_ssm_kernel, tt=tt, nb=nb),
        grid=(seq // tt,),
        in_specs=[pl.BlockSpec((tt * nb, SSM_WIDTH), lambda i: (i, 0)),
                  full(bbd), full(lam), full(cbd), full(d_row), full(wglu), full(bglu_row)],
        out_specs=pl.BlockSpec((tt * nb, SSM_WIDTH), lambda i: (i, 0)),
        out_shape=jax.ShapeDtypeStruct((rows, SSM_WIDTH), BF16),
        scratch_shapes=[pltpu.VMEM((nb, 2 * SSM_STATES), F32),
                        pltpu.VMEM((tt * nb, 2 * SSM_STATES), F32)],
        compiler_params=_params("arbitrary"),
        name="ssm",
    )(u_tm, bbd, lam, cbd, d_row, wglu, bglu_row)


def _memkv_kernel(mem_ref, w_ref, k_ref, v_ref):
    kv = _dot(mem_ref[0].astype(BF16), w_ref[...])
    k_ref[0] = kv[:, :D_MODEL].astype(BF16)
    v_ref[0] = kv[:, D_MODEL:].astype(BF16)


def _memkv(mem, w_mkv):
    bsz, mlen, _ = mem.shape
    return pl.pallas_call(
        _memkv_kernel,
        grid=(bsz,),
        in_specs=[pl.BlockSpec((1, mlen, D_MODEL), lambda b: (b, 0, 0)),
                  pl.BlockSpec(w_mkv.shape, lambda b: (0, 0))],
        out_specs=[pl.BlockSpec((1, mlen, D_MODEL), lambda b: (b, 0, 0))] * 2,
        out_shape=[jax.ShapeDtypeStruct((bsz, mlen, D_MODEL), BF16)] * 2,
        compiler_params=_params("parallel"),
        name="mem_kv",
    )(mem, w_mkv)


def _mix_mem_kernel(ya_ref, ys_ref, yb_ref, x_ref, gmix_ref, wout_ref, ln1_ref,
                    mk_ref, mv_ref, wmq_ref, wmo_ref, ln2_ref, xt_ref, *, alpha):
    gm = gmix_ref[...]
    c1, c2 = MLA_QK_W, MLA_QK_W + SSM_WIDTH
    ya = (_rms_rows(ya_ref[0].astype(F32), MLA_V_W) * gm[:, 0:c1]).astype(BF16)
    ys = (_rms_rows(ys_ref[...].astype(F32), SSM_WIDTH) * gm[:, c1:c2]).astype(BF16)
    yb = (_rms_rows(yb_ref[0].astype(F32), SB_WIDTH) * gm[:, c2:]).astype(BF16)
    mix = _dot(ya, wout_ref[0:c1, :]) + _dot(ys, wout_ref[c1:c2, :]) + _dot(yb, wout_ref[c2:, :])
    x1 = _layer_norm_rows(alpha * x_ref[0] + mix, ln1_ref[0:1, :], ln1_ref[1:2, :])

    q = _dot(x1.astype(BF16), wmq_ref[...]).astype(BF16)
    heads = []
    for h in range(MEM_HEADS):
        lo, hi = h * MEM_HEAD_DIM, (h + 1) * MEM_HEAD_DIM
        s = _dot_nt(q[:, lo:hi], mk_ref[0, :, lo:hi])
        p = jnp.exp(s - jnp.max(s, axis=-1, keepdims=True))
        p = p / jnp.sum(p, axis=-1, keepdims=True)
        heads.append(_dot(p.astype(BF16), mv_ref[0, :, lo:hi]).astype(BF16))
    o = jnp.concatenate(heads, axis=-1)
    x2 = _layer_norm_rows(alpha * x1 + _dot(o, wmo_ref[...]), ln2_ref[0:1, :], ln2_ref[1:2, :])
    xt_ref[...] = x2.T


def _mix_mem(ya, ys_tm, yb, x, gmix, wout, ln1, mk, mv, wmq, wmo, ln2, alpha):
    bsz, seq, _ = x.shape
    ts = min(seq, 512)
    nst = seq // ts
    mlen = mk.shape[1]
    tok = lambda w: pl.BlockSpec((1, ts, w), lambda b, s: (b, s, 0))
    full = lambda a: pl.BlockSpec(a.shape, lambda b, s: (0,) * a.ndim)
    return pl.pallas_call(
        functools.partial(_mix_mem_kernel, alpha=alpha),
        grid=(bsz, nst),
        in_specs=[tok(MLA_QK_W), pl.BlockSpec((ts, SSM_WIDTH), lambda b, s: (s, b)), tok(SB_WIDTH), tok(D_MODEL),
                  full(gmix), full(wout), full(ln1),
                  pl.BlockSpec((1, mlen, D_MODEL), lambda b, s: (b, 0, 0)),
                  pl.BlockSpec((1, mlen, D_MODEL), lambda b, s: (b, 0, 0)),
                  full(wmq), full(wmo), full(ln2)],
        out_specs=pl.BlockSpec((D_MODEL, ts), lambda b, s: (0, b * nst + s)),
        out_shape=jax.ShapeDtypeStruct((D_MODEL, bsz * seq), F32),
        compiler_params=_params("parallel", "parallel"),
        name="mix_mem",
    )(ya, ys_tm, yb, x, gmix, wout, ln1, mk, mv, wmq, wmo, ln2)


def _sort16_desc(v):
    v = list(v)
    for k in (2, 4, 8, 16):
        j = k // 2
        while j >= 1:
            for i in range(16):
                l = i ^ j
                if l > i:
                    hi, lo = jnp.maximum(v[i], v[l]), jnp.minimum(v[i], v[l])
                    v[i], v[l] = (hi, lo) if (i & k) == 0 else (lo, hi)
            j //= 2
    return v


def _merge_bitonic_desc(v):
    v = list(v)
    for j in (8, 4, 2, 1):
        for i in range(16):
            l = i ^ j
            if l > i:
                v[i], v[l] = jnp.maximum(v[i], v[l]), jnp.minimum(v[i], v[l])
    return v


def _top16_sorted(rows):
    v = _sort16_desc(rows)
    for shift in (4, 2, 1):
        partner = [pltpu.roll(x, shift, axis=0) for x in v]
        v = _merge_bitonic_desc([jnp.maximum(v[i], partner[15 - i]) for i in range(16)])
    return v


def _count_prefix(pred, v):
    p8 = pred(v[7])
    p4 = pred(jnp.where(p8, v[11], v[3]))
    p2 = pred(jnp.where(p8, jnp.where(p4, v[13], v[9]), jnp.where(p4, v[5], v[1])))
    p1 = pred(jnp.where(p8, jnp.where(p4, jnp.where(p2, v[14], v[12]), jnp.where(p2, v[10], v[8])),
                        jnp.where(p4, jnp.where(p2, v[6], v[4]), jnp.where(p2, v[2], v[0]))))
    count = (jnp.where(p8, 8.0, 0.0) + jnp.where(p4, 4.0, 0.0)) + (jnp.where(p2, 2.0, 0.0) + jnp.where(p1, 1.0, 0.0))
    return jnp.where(pred(v[15]), 16.0, count)


def _peer_route_kernel(xt_ref, wpq_ref, keys_ref, a_ref, n_ref, b_ref, r_ref, s1_ref, s2_ref):
    xb = xt_ref[...].astype(BF16)
    q = _dot(wpq_ref[...], xb).astype(BF16)
    s1_ref[...] = _dot(keys_ref[0, 0], q[0:PEER_KEY_DIM, :])
    s2_ref[...] = _dot(keys_ref[0, 1], q[PEER_KEY_DIM:, :])
    sub = lax.broadcasted_iota(jnp.int32, (8, LANE), 0)
    groups = PEER_N_KEYS // 8

    def spread(vals):
        out = vals[7]
        for j in range(6, -1, -1):
            out = jnp.where(sub == j, vals[j], out)
        return out

    def chunk(c, carry):
        ln = pl.ds(pl.multiple_of(c * LANE, LANE), LANE)
        rows1 = [s1_ref[8 * i:8 * i + 8, ln] for i in range(groups)]
        rows2 = [s2_ref[8 * i:8 * i + 8, ln] for i in range(groups)]
        v1 = _top16_sorted(rows1)
        v2 = _top16_sorted(rows2)
        v2lo, v2hi, v1hi = spread(v2[:8]), spread(v2[8:]), spread(v1[8:])
        cands = ([v1[0] + v2lo, v1[0] + v2hi] + [v1[r] + v2lo for r in range(1, 8)] + [v1hi + v2[0]]
                 + [v1[r] + v2hi for r in range(1, 7)])
        top = _top16_sorted(cands)
        thr = top[15]
        z = jnp.exp(top[0] - top[0])
        for k in range(1, PEER_TOPK):
            z = z + jnp.exp(top[k] - top[0])
        inv_z = 1.0 / z
        for i in range(groups):
            s = rows1[i]
            kept = _count_prefix(lambda t, s=s: s + t >= thr, v2)
            in_top = s >= v1[15]
            a_ref[i, 0, :, ln] = jnp.where(in_top, jnp.exp(s - v1[0]), 0.0)
            n_ref[i, 0, :, ln] = jnp.where(in_top, kept, 0.0)
        for i in range(groups // 2):
            bs, rs = [], []
            for s in (rows2[2 * i], rows2[2 * i + 1]):
                rank = _count_prefix(lambda t, s=s: t > s, v2)
                rs.append(rank)
                bs.append(jnp.where(rank < float(PEER_TOPK), jnp.exp(s - v2[0]) * inv_z, 0.0))
            b_ref[0, 16 * i:16 * i + 16, ln] = jnp.concatenate(bs, axis=0).astype(BF16)
            r_ref[0, 16 * i:16 * i + 16, ln] = jnp.concatenate(rs, axis=0).astype(BF16)
        return carry

    lax.fori_loop(0, xt_ref.shape[1] // LANE, chunk, 0)


def _peer_route(xt, wpq_t, keys):
    ntok = xt.shape[1]
    tt = min(ntok, 512)
    nblk = PEER_N_KEYS // 8
    return pl.pallas_call(
        _peer_route_kernel,
        grid=(ntok // tt, PEER_HEADS),
        in_specs=[pl.BlockSpec((D_MODEL, tt), lambda i, h: (0, i)),
                  pl.BlockSpec((2 * PEER_KEY_DIM, D_MODEL), lambda i, h: (h, 0)),
                  pl.BlockSpec((1, 2, PEER_N_KEYS, PEER_KEY_DIM), lambda i, h: (h, 0, 0, 0))],
        out_specs=[pl.BlockSpec((nblk, 1, 8, tt), lambda i, h: (0, h, 0, i)),
                   pl.BlockSpec((nblk, 1, 8, tt), lambda i, h: (0, h, 0, i)),
                   pl.BlockSpec((1, PEER_N_KEYS, tt), lambda i, h: (h, 0, i)),
                   pl.BlockSpec((1, PEER_N_KEYS, tt), lambda i, h: (h, 0, i))],
        out_shape=[jax.ShapeDtypeStruct((nblk, PEER_HEADS, 8, ntok), F32),
                   jax.ShapeDtypeStruct((nblk, PEER_HEADS, 8, ntok), F32),
                   jax.ShapeDtypeStruct((PEER_HEADS, PEER_N_KEYS, ntok), BF16),
                   jax.ShapeDtypeStruct((PEER_HEADS, PEER_N_KEYS, ntok), BF16)],
        scratch_shapes=[pltpu.VMEM((PEER_N_KEYS, tt), F32), pltpu.VMEM((PEER_N_KEYS, tt), F32)],
        compiler_params=_params("parallel", "arbitrary"),
        name="peer_route",
    )(xt, wpq_t, keys)


PEER_I1_PER_TILE = 8
PEER_TILE = PEER_I1_PER_TILE * PEER_N_KEYS


def _peer_dense_kernel(xt_ref, a_ref, n_ref, b_ref, r_ref, u_ref, vt_ref, ln_ref, o_ref,
                       acc_ref, xb_ref, gh_ref, g_ref, br_ref, *, alpha):
    e = pl.program_id(1)
    tt = xb_ref.shape[1]
    rows = BF16_SUBLANES
    groups = PEER_N_KEYS // rows

    def br_row(c, g, h):
        return ((c * groups + g) * 2 * PEER_HEADS + 2 * h) * rows

    @pl.when(e == 0)
    def _():
        acc_ref[...] = jnp.zeros_like(acc_ref)
        xb_ref[...] = xt_ref[...].astype(BF16)
        for c in range(tt // LANE):
            for g in range(groups):
                for h in range(PEER_HEADS):
                    t0 = br_row(c, g, h)
                    br_ref[t0:t0 + rows, :] = b_ref[h, g * rows:(g + 1) * rows, c * LANE:(c + 1) * LANE]
                    br_ref[t0 + rows:t0 + 2 * rows, :] = r_ref[h, g * rows:(g + 1) * rows, c * LANE:(c + 1) * LANE]

    for j in range(PEER_I1_PER_TILE):
        for c in range(tt // LANE):
            ln = slice(c * LANE, (c + 1) * LANE)
            a_rows = [jnp.broadcast_to(a_ref[0, h, j:j + 1, ln], (rows, LANE)).astype(BF16) for h in range(PEER_HEADS)]
            n_rows = [jnp.broadcast_to(n_ref[0, h, j:j + 1, ln], (rows, LANE)).astype(BF16) for h in range(PEER_HEADS)]
            for g in range(groups):
                gate = None
                for h in range(PEER_HEADS):
                    t0 = br_row(c, g, h)
                    term = jnp.where(br_ref[t0 + rows:t0 + 2 * rows, :] < n_rows[h], br_ref[t0:t0 + rows, :] * a_rows[h],
                                     jnp.zeros((), BF16))
                    gate = term if gate is None else gate + term
                e0 = j * PEER_N_KEYS + g * rows
                g_ref[e0:e0 + rows, ln] = gate
    gh_ref[...] = _dot(u_ref[...], xb_ref[...]).astype(BF16)
    gh_ref[...] = g_ref[...] * _gelu_tanh(gh_ref[...])
    acc_ref[...] += _dot(vt_ref[...], gh_ref[...])

    @pl.when(e == pl.num_programs(1) - 1)
    def _():
        y = alpha * xt_ref[...] + acc_ref[...]
        mu = jnp.mean(y, axis=0, keepdims=True)
        c = y - mu
        var = jnp.mean(c * c, axis=0, keepdims=True)
        o_ref[...] = (c * lax.rsqrt(var + NORM_EPS)).T * ln_ref[0:1, :] + ln_ref[1:2, :]


def _peer_dense(xt, a, n, b, r, u, vt, ln, alpha):
    ntok = xt.shape[1]
    tt = min(ntok, 1024)
    ntile = PEER_EXPERTS // PEER_TILE
    return pl.pallas_call(
        functools.partial(_peer_dense_kernel, alpha=alpha),
        grid=(ntok // tt, ntile),
        in_specs=[pl.BlockSpec((D_MODEL, tt), lambda i, e: (0, i)),
                  pl.BlockSpec((1, PEER_HEADS, PEER_I1_PER_TILE, tt), lambda i, e: (e, 0, 0, i)),
                  pl.BlockSpec((1, PEER_HEADS, PEER_I1_PER_TILE, tt), lambda i, e: (e, 0, 0, i)),
                  pl.BlockSpec((PEER_HEADS, PEER_N_KEYS, tt), lambda i, e: (0, 0, i)),
                  pl.BlockSpec((PEER_HEADS, PEER_N_KEYS, tt), lambda i, e: (0, 0, i)),
                  pl.BlockSpec((PEER_TILE, D_MODEL), lambda i, e: (e, 0)),
                  pl.BlockSpec((D_MODEL, PEER_TILE), lambda i, e: (0, e)),
                  pl.BlockSpec(ln.shape, lambda i, e: (0, 0))],
        out_specs=pl.BlockSpec((tt, D_MODEL), lambda i, e: (i, 0)),
        out_shape=jax.ShapeDtypeStruct((ntok, D_MODEL), F32),
        scratch_shapes=[pltpu.VMEM((D_MODEL, tt), F32), pltpu.VMEM((D_MODEL, tt), BF16),
                        pltpu.VMEM((PEER_TILE, tt), BF16), pltpu.VMEM((PEER_TILE, tt), BF16),
                        pltpu.VMEM((2 * PEER_HEADS * PEER_N_KEYS * (tt // LANE), LANE), BF16)],
        compiler_params=_params("parallel", "arbitrary"),
        name="peer_dense",
    )(xt, a, n, b, r, u, vt, ln)


def _pad_heads(w, heads, width):
    rows = w.shape[0]
    w = w.reshape(rows, heads, width)
    return jnp.pad(w, ((0, 0), (0, 0), (0, HEAD_PAD - width))).reshape(rows, heads * HEAD_PAD)


def _rotate_half_cols(w, heads, width, nope):
    rows = w.shape[0]
    w = w.reshape(rows, heads, width)
    half = (width - nope) // 2
    x1, x2 = w[..., nope:nope + half], w[..., nope + half:]
    out = jnp.concatenate([jnp.zeros_like(w[..., :nope]), -x2, x1], axis=-1)
    return out.reshape(rows, heads * width)


def _pack_inproj(w_in, g_cq, g_ckv, w_uq, w_ukv):
    c = 0
    cols = {}
    for name, width in (("cq", MLA_Q_RANK), ("ckv", MLA_KV_RANK), ("kr", MLA_ROPE), ("ssm", SSM_WIDTH),
                        ("qsb", SB_WIDTH), ("ksb", SB_WIDTH), ("vsb", SB_WIDTH)):
        cols[name] = w_in[:, c:c + width]
        c += width
    rows = w_in.shape[0]
    zeros = lambda n: jnp.zeros((rows, n), F32)
    kr = cols["kr"]
    half = MLA_ROPE // 2
    kr_plain = jnp.concatenate([zeros(MLA_NOPE), kr, zeros(HEAD_PAD - MLA_NOPE - MLA_ROPE)], axis=1)
    kr_swap = jnp.concatenate([zeros(MLA_NOPE), -kr[:, half:], kr[:, :half],
                               zeros(HEAD_PAD - MLA_NOPE - MLA_ROPE)], axis=1)
    sb_scale = SB_DIM ** -0.5 * LOG2E
    win = jnp.concatenate([cols["cq"], cols["ckv"], kr_plain, kr_swap, cols["ssm"],
                           _pad_heads(cols["qsb"] * sb_scale, SB_HEADS, SB_DIM),
                           _pad_heads(cols["ksb"], SB_HEADS, SB_DIM), cols["vsb"]], axis=1).astype(BF16)

    qk_dim = MLA_NOPE + MLA_ROPE
    wq = w_uq * (g_cq[:, None] * (qk_dim ** -0.5 * LOG2E))
    wuq = jnp.concatenate([_pad_heads(wq, MLA_HEADS, qk_dim),
                           _pad_heads(_rotate_half_cols(wq, MLA_HEADS, qk_dim, MLA_NOPE), MLA_HEADS, qk_dim)],
                          axis=1).astype(BF16)
    wkv = (w_ukv * g_ckv[:, None]).reshape(MLA_KV_RANK, MLA_HEADS, MLA_NOPE + MLA_V)
    wukv = jnp.concatenate([_pad_heads(wkv[..., :MLA_NOPE].reshape(MLA_KV_RANK, -1), MLA_HEADS, MLA_NOPE),
                            _pad_heads(wkv[..., MLA_NOPE:].reshape(MLA_KV_RANK, -1), MLA_HEADS, MLA_V)],
                           axis=1).astype(BF16)
    return win, wuq, wukv


def _pad_mla_rows(w):
    cols = w.shape[1]
    head_rows = jnp.pad(w[:MLA_V_W].reshape(MLA_HEADS, MLA_V, cols), ((0, 0), (0, HEAD_PAD - MLA_V), (0, 0)))
    return jnp.concatenate([head_rows.reshape(MLA_QK_W, cols), w[MLA_V_W:]], axis=0)


def _pack_ssm(lam_re, lam_im, log_step, b_re, b_im, c_re, c_im, d_skip):
    step = jnp.exp(log_step)[:, None]
    decay = jnp.exp(lam_re * step)
    ab_re, ab_im = decay * jnp.cos(lam_im * step), decay * jnp.sin(lam_im * step)
    inv = 1.0 / (lam_re * lam_re + lam_im * lam_im)
    f_re = ((ab_re - 1.0) * lam_re + ab_im * lam_im) * inv
    f_im = (ab_im * lam_re - (ab_re - 1.0) * lam_im) * inv
    bb_re = f_re[..., None] * b_re - f_im[..., None] * b_im
    bb_im = f_re[..., None] * b_im + f_im[..., None] * b_re
    per = SSM_GROUPS // SSM_BLOCKS
    eye = jnp.eye(per, dtype=F32)
    split = lambda w: w.reshape((SSM_BLOCKS, per) + w.shape[1:])
    blk = lambda w: jnp.einsum("bgph,gk->bghkp", split(w), eye).reshape(SSM_BLOCKS, SSM_BLOCK_CH, SSM_BLOCK_STATES)
    bbd = jnp.concatenate([blk(bb_re), blk(bb_im)], axis=2).astype(BF16)
    blk_c = lambda w: jnp.einsum("bghp,gk->bgpkh", split(w), eye).reshape(SSM_BLOCKS, SSM_BLOCK_STATES, SSM_BLOCK_CH)
    cbd = jnp.concatenate([blk_c(c_re), blk_c(-c_im)], axis=1).astype(BF16)
    lam = jnp.stack([ab_re.reshape(-1), ab_im.reshape(-1)])
    return bbd, lam, cbd, d_skip.reshape(1, SSM_WIDTH)


def kernel(x, mem, positions, w_in, g_cq, g_ckv, w_uq, w_ukv, ssm_lam_re, ssm_lam_im, ssm_log_step, ssm_b_re, ssm_b_im, ssm_c_re, ssm_c_im, ssm_d, w_glu, b_glu, g_mix, w_out, ln_mix_g, ln_mix_b, w_mq, w_mkv, w_mo, ln_mem_g, ln_mem_b, w_pq, peer_sub_keys, peer_u, peer_v, ln_ffn_g, ln_ffn_b):
    bsz, seq, _ = x.shape
    depth = w_in.shape[0]
    alpha = (2 * depth) ** 0.25
    cos_t, sin_t = _rope_tables(positions)
    for l in range(depth):
        win, wuq, wukv = _pack_inproj(w_in[l], g_cq[l], g_ckv[l], w_uq[l], w_ukv[l])
        q, k, v, u_tm, q_sb, k_sb, v_sb = _inproj(x, win, wuq, wukv, cos_t, sin_t)
        y_mla = _mla_attention(q, k, v)
        y_sb = _sb_attention(q_sb, k_sb, v_sb)
        bbd, lam, cbd, d_row = _pack_ssm(ssm_lam_re[l], ssm_lam_im[l], ssm_log_step[l], ssm_b_re[l], ssm_b_im[l],
                                         ssm_c_re[l], ssm_c_im[l], ssm_d[l])
        y_ssm = _ssm(u_tm.reshape(seq * bsz, SSM_WIDTH), bbd, lam, cbd, d_row,
                     w_glu[l].astype(BF16), b_glu[l].reshape(1, SSM_WIDTH), bsz)
        mk, mv = _memkv(mem, w_mkv[l].astype(BF16))
        xt = _mix_mem(y_mla, y_ssm.reshape(seq, bsz * SSM_WIDTH), y_sb, x,
                      _pad_mla_rows(g_mix[l][:, None]).reshape(1, -1), _pad_mla_rows(w_out[l]).astype(BF16),
                      jnp.stack([ln_mix_g[l], ln_mix_b[l]]),
                      mk, mv, (w_mq[l] * MEM_HEAD_DIM ** -0.5).astype(BF16), w_mo[l].astype(BF16),
                      jnp.stack([ln_mem_g[l], ln_mem_b[l]]), alpha)
        a, n, b, r = _peer_route(xt, w_pq[l].T.astype(BF16), peer_sub_keys[l].astype(BF16))
        x = _peer_dense(xt, a, n, b, r, peer_u[l].astype(BF16), peer_v[l].T.astype(BF16),
                        jnp.stack([ln_ffn_g[l], ln_ffn_b[l]]), alpha).reshape(bsz, seq, D_MODEL)
    return x
```

```python
import functools
import math

import jax
import jax.numpy as jnp
from jax import lax
from jax.experimental import pallas as pl
from jax.experimental.pallas import tpu as pltpu

F32 = jnp.float32
BF16 = jnp.bfloat16

D_MODEL = 1024
CHUNK = 64
NORM_EPS = 1e-5
NEG_BIG = -1e30
LOG2E = math.log2(math.e)

MLA_HEADS = 6
MLA_NOPE = 64
MLA_ROPE = 32
MLA_V = 64
MLA_Q_RANK = 256
MLA_KV_RANK = 128
ROPE_THETA = 10000.0

SSM_GROUPS = 24
SSM_CH = 16
SSM_STATE = 64
SSM_WIDTH = SSM_GROUPS * SSM_CH
SSM_STATES = SSM_GROUPS * SSM_STATE
SSM_BLOCKS = 3
SSM_BLOCK_CH = SSM_WIDTH // SSM_BLOCKS
SSM_BLOCK_STATES = SSM_STATES // SSM_BLOCKS

SB_HEADS = 4
SB_DIM = 64
SB_WIDTH = SB_HEADS * SB_DIM

MEM_HEADS = 4
MEM_HEAD_DIM = D_MODEL // MEM_HEADS

PEER_HEADS = 8
PEER_N_KEYS = 128
PEER_TOPK = 16
PEER_KEY_DIM = 128
PEER_EXPERTS = PEER_N_KEYS * PEER_N_KEYS

LANE = 128
BF16_SUBLANES = 16
HEAD_PAD = 128
ATTN_TQ = 512
ATTN_TK = 256

VMEM_LIMIT = 56 * 1024 * 1024

_C_CQ = 0
_C_CKV = _C_CQ + MLA_Q_RANK
_C_KRP = _C_CKV + MLA_KV_RANK
_C_KRS = _C_KRP + HEAD_PAD
_C_SSM = _C_KRS + HEAD_PAD
_C_QSB = _C_SSM + SSM_WIDTH
_C_KSB = _C_QSB + SB_HEADS * HEAD_PAD
_C_VSB = _C_KSB + SB_HEADS * HEAD_PAD
_C_END = _C_VSB + SB_WIDTH
MLA_QK_W = MLA_HEADS * HEAD_PAD
MLA_V_W = MLA_HEADS * MLA_V


def _params(*sem):
    return pltpu.CompilerParams(dimension_semantics=sem, vmem_limit_bytes=VMEM_LIMIT)


def _dot(a, b):
    return jnp.dot(a, b, preferred_element_type=F32)


def _dot_nt(a, b):
    return lax.dot_general(a, b, (((1,), (1,)), ((), ())), preferred_element_type=F32)


def _gelu_tanh(x):
    c = math.sqrt(2.0 / math.pi)
    inner = x * (c + (c * 0.044715) * (x * x))
    return (0.5 * x) * (1.0 + jnp.tanh(inner))


def _layer_norm_rows(v, g, b):
    mu = jnp.mean(v, axis=-1, keepdims=True)
    c = v - mu
    var = jnp.mean(c * c, axis=-1, keepdims=True)
    return c * lax.rsqrt(var + NORM_EPS) * g + b


def _rms_rows(v, width):
    return v * lax.rsqrt(jnp.sum(v * v, axis=-1, keepdims=True) * (1.0 / width) + NORM_EPS)


def _rope_kernel(pos_ref, freq_ref, cos_ref, sin_ref):
    ang = pos_ref[0].astype(F32) * freq_ref[...]
    lane = lax.broadcasted_iota(jnp.int32, ang.shape, 1)
    rot = (lane >= MLA_NOPE) & (lane < MLA_NOPE + MLA_ROPE)
    cos_ref[0] = jnp.where(rot, jnp.cos(ang), jnp.where(lane < MLA_NOPE, 1.0, 0.0))
    sin_ref[0] = jnp.where(rot, jnp.sin(ang), 0.0)


def _rope_tables(positions):
    bsz, seq = positions.shape
    ts = min(seq, 512)
    half = MLA_ROPE // 2
    freq = ROPE_THETA ** (-jnp.arange(half, dtype=F32) / half)
    freq_row = jnp.zeros((1, HEAD_PAD), F32).at[0, MLA_NOPE:MLA_NOPE + MLA_ROPE].set(jnp.tile(freq, 2))
    return pl.pallas_call(
        _rope_kernel,
        grid=(bsz, seq // ts),
        in_specs=[pl.BlockSpec((1, ts, 1), lambda b, s: (b, s, 0)),
                  pl.BlockSpec((1, HEAD_PAD), lambda b, s: (0, 0))],
        out_specs=[pl.BlockSpec((1, ts, HEAD_PAD), lambda b, s: (b, s, 0))] * 2,
        out_shape=[jax.ShapeDtypeStruct((bsz, seq, HEAD_PAD), F32)] * 2,
        compiler_params=_params("parallel", "parallel"),
        name="rope_tables",
    )(positions.reshape(bsz, seq, 1), freq_row)


def _inproj_kernel(x_ref, win_ref, wuq_ref, wukv_ref, cos_ref, sin_ref, ones_ref,
                   q_ref, k_ref, v_ref, u_ref, qs_ref, ks_ref, vs_ref):
    xb = x_ref[0].astype(BF16)
    acc = _dot(xb, win_ref[...])
    cos = cos_ref[0]
    sin = sin_ref[0]
    cqn = _rms_rows(acc[:, _C_CQ:_C_CKV], MLA_Q_RANK).astype(BF16)
    qq = _dot(cqn, wuq_ref[...])
    ckvn = _rms_rows(acc[:, _C_CKV:_C_KRP], MLA_KV_RANK).astype(BF16)
    kv = _dot(ckvn, wukv_ref[...])
    k_rope = acc[:, _C_KRP:_C_KRS] * cos + acc[:, _C_KRS:_C_SSM] * sin
    for h in range(MLA_HEADS):
        lo, hi = h * HEAD_PAD, (h + 1) * HEAD_PAD
        q_ref[0, :, lo:hi] = (qq[:, lo:hi] * cos + qq[:, MLA_QK_W + lo:MLA_QK_W + hi] * sin).astype(BF16)
        k_ref[0, :, lo:hi] = (kv[:, lo:hi] + k_rope).astype(BF16)
    v_ref[0] = (kv[:, MLA_QK_W:] + ones_ref[...]).T.astype(BF16)
    u_ref[...] = acc[:, _C_SSM:_C_QSB].astype(BF16)
    qs_ref[0] = acc[:, _C_QSB:_C_KSB].astype(BF16)
    ks_ref[0] = acc[:, _C_KSB:_C_VSB].astype(BF16)
    vs_ref[0] = acc[:, _C_VSB:_C_END].T.astype(BF16)


def _inproj(x, win, wuq, wukv, cos_t, sin_t):
    bsz, seq, _ = x.shape
    ts = min(seq, 512)
    tok = lambda w: pl.BlockSpec((1, ts, w), lambda b, s: (b, s, 0))
    tok_t = lambda w: pl.BlockSpec((1, w, ts), lambda b, s: (b, 0, s))
    full = lambda a: pl.BlockSpec(a.shape, lambda b, s: (0,) * a.ndim)
    sb_w = SB_HEADS * HEAD_PAD
    lane = jnp.arange(MLA_QK_W) % HEAD_PAD
    ones_row = (lane == MLA_V).astype(F32).reshape(1, MLA_QK_W)
    return pl.pallas_call(
        _inproj_kernel,
        grid=(bsz, seq // ts),
        in_specs=[tok(D_MODEL), full(win), full(wuq), full(wukv), tok(HEAD_PAD), tok(HEAD_PAD), full(ones_row)],
        out_specs=[tok(MLA_QK_W), tok(MLA_QK_W), tok_t(MLA_QK_W),
                   pl.BlockSpec((ts, SSM_WIDTH), lambda b, s: (s, b)),
                   tok(sb_w), tok(sb_w), tok_t(SB_WIDTH)],
        out_shape=[jax.ShapeDtypeStruct((bsz, seq, MLA_QK_W), BF16),
                   jax.ShapeDtypeStruct((bsz, seq, MLA_QK_W), BF16),
                   jax.ShapeDtypeStruct((bsz, MLA_QK_W, seq), BF16),
                   jax.ShapeDtypeStruct((seq, bsz * SSM_WIDTH), BF16),
                   jax.ShapeDtypeStruct((bsz, seq, sb_w), BF16),
                   jax.ShapeDtypeStruct((bsz, seq, sb_w), BF16),
                   jax.ShapeDtypeStruct((bsz, SB_WIDTH, seq), BF16)],
        compiler_params=_params("parallel", "parallel"),
        name="inproj",
    )(x, win, wuq, wukv, cos_t, sin_t, ones_row)


def _mla_kernel(q_ref, k_ref, vt_ref, o_ref, m_ref, acc_ref, s_ref, p_ref, *, tq, tk):
    qi = pl.program_id(1)
    per = tq // tk
    query_chunk = (qi * tq + lax.broadcasted_iota(jnp.int32, (tk, tq), 1)) // CHUNK
    key_in_tile = lax.broadcasted_iota(jnp.int32, (tk, tq), 0)
    m_ref[...] = jnp.full(m_ref.shape, NEG_BIG, F32)
    acc_ref[...] = jnp.zeros(acc_ref.shape, F32)

    def block(kb, masked):
        ks = pl.multiple_of(kb * tk, tk)
        for h in range(MLA_HEADS):
            tile = slice(h * HEAD_PAD, (h + 1) * HEAD_PAD)
            s_ref[h] = _dot_nt(k_ref[0, pl.ds(ks, tk), tile], q_ref[0, :, tile])
        if masked:
            allowed = (ks + key_in_tile) // CHUNK <= query_chunk
        alphas = []
        for h in range(MLA_HEADS):
            s = s_ref[h]
            if masked:
                s = jnp.where(allowed, s, NEG_BIG)
            m_old = m_ref[h]
            m_new = jnp.maximum(m_old, jnp.max(s, axis=0, keepdims=True))
            p_ref[h] = jnp.exp2(s - m_new).astype(BF16)
            alphas.append(jnp.exp2(m_old - m_new))
            m_ref[h] = m_new
        for h in range(MLA_HEADS):
            tile = slice(h * HEAD_PAD, (h + 1) * HEAD_PAD)
            acc_ref[h] = alphas[h] * acc_ref[h] + _dot(vt_ref[0, tile, pl.ds(ks, tk)], p_ref[h])

    def body(kb, carry):
        block(kb, False)
        return carry

    lax.fori_loop(0, qi * per, body, 0)
    for d in range(per):
        block(qi * per + d, True)
    row = lax.broadcasted_iota(jnp.int32, (HEAD_PAD, tq), 0)
    for h in range(MLA_HEADS):
        acc = acc_ref[h]
        out_t = jnp.where(row < MLA_V, acc / acc[MLA_V:MLA_V + 1, :], 0.0)
        o_ref[0, :, h * HEAD_PAD:(h + 1) * HEAD_PAD] = out_t.T.astype(BF16)


def _mla_attention(q, k, vt):
    bsz, seq, _ = q.shape
    tq, tk = min(seq, ATTN_TQ), min(seq, ATTN_TK)
    return pl.pallas_call(
        functools.partial(_mla_kernel, tq=tq, tk=tk),
        grid=(bsz, seq // tq),
        in_specs=[pl.BlockSpec((1, tq, MLA_QK_W), lambda b, i: (b, i, 0)),
                  pl.BlockSpec((1, seq, MLA_QK_W), lambda b, i: (b, 0, 0)),
                  pl.BlockSpec((1, MLA_QK_W, seq), lambda b, i: (b, 0, 0))],
        out_specs=pl.BlockSpec((1, tq, MLA_QK_W), lambda b, i: (b, i, 0)),
        out_shape=jax.ShapeDtypeStruct((bsz, seq, MLA_QK_W), BF16),
        scratch_shapes=[pltpu.VMEM((MLA_HEADS, 1, tq), F32), pltpu.VMEM((MLA_HEADS, HEAD_PAD, tq), F32),
                        pltpu.VMEM((MLA_HEADS, tk, tq), F32), pltpu.VMEM((MLA_HEADS, tk, tq), BF16)],
        compiler_params=_params("parallel", "arbitrary"),
        name="mla_attention",
    )(q, k, vt)


def _sb_kernel(q_ref, k_ref, vt_ref, tri_ref, o_ref, right_ref, acc_ref, z_ref, split_ref, sum_ref, w_ref, *, tq, tk):
    qi = pl.program_id(1)
    per = tq // tk
    query_pos = qi * tq + lax.broadcasted_iota(jnp.int32, (tk, tq), 1)
    key_in_tile = lax.broadcasted_iota(jnp.int32, (tk, tq), 0)
    right_ref[...] = jnp.zeros(right_ref.shape, F32)
    acc_ref[...] = jnp.zeros(acc_ref.shape, F32)

    def block(kb, masked):
        ks = pl.multiple_of(kb * tk, tk)
        for h in range(SB_HEADS):
            tile = slice(h * HEAD_PAD, (h + 1) * HEAD_PAD)
            z_ref[h] = _dot_nt(k_ref[0, pl.ds(ks, tk), tile], q_ref[0, :, tile])
        if masked:
            earlier = ks + key_in_tile < query_pos
        for h in range(SB_HEADS):
            z = z_ref[h]
            fail = jnp.maximum(z, 0.0) + jnp.log2(1.0 + jnp.exp2(-jnp.abs(z)))
            z_ref[h] = z - fail
            if masked:
                fail = jnp.where(earlier, fail, 0.0)
            hi = fail.astype(BF16)
            split_ref[h, 0:tk, :] = hi
            split_ref[h, tk:, :] = (fail - hi.astype(F32)).astype(BF16)
            sum_ref[h] = jnp.sum(fail, axis=0, keepdims=True)
        for h in range(SB_HEADS):
            between = _dot(tri_ref[...], split_ref[h]) + right_ref[h]
            w = jnp.exp2(z_ref[h] - between)
            if masked:
                w = jnp.where(earlier, w, 0.0)
            w_ref[h] = w.astype(BF16)
            right_ref[h] += sum_ref[h]
        for h in range(SB_HEADS):
            pair = slice((h // 2) * HEAD_PAD, (h // 2 + 1) * HEAD_PAD)
            acc_ref[h] += _dot(vt_ref[0, pair, pl.ds(ks, tk)], w_ref[h])

    def body(i, carry):
        block(qi * per - 1 - i, False)
        return carry

    for d in range(per - 1, -1, -1):
        block(qi * per + d, True)
    lax.fori_loop(0, qi * per, body, 0)
    row = lax.broadcasted_iota(jnp.int32, (HEAD_PAD, tq), 0)
    for pair in range(SB_HEADS // 2):
        out_t = jnp.where(row < SB_DIM, acc_ref[2 * pair], acc_ref[2 * pair + 1])
        o_ref[0, :, pair * HEAD_PAD:(pair + 1) * HEAD_PAD] = out_t.T.astype(BF16)


def _sb_attention(q, k, vt):
    bsz, seq, _ = q.shape
    tq, tk = min(seq, ATTN_TQ), min(seq, ATTN_TK)
    sb_w = SB_HEADS * HEAD_PAD
    tri = (lax.broadcasted_iota(jnp.int32, (tk, tk), 1) > lax.broadcasted_iota(jnp.int32, (tk, tk), 0)).astype(BF16)
    tri2 = jnp.concatenate([tri, tri], axis=1)
    return pl.pallas_call(
        functools.partial(_sb_kernel, tq=tq, tk=tk),
        grid=(bsz, seq // tq),
        in_specs=[pl.BlockSpec((1, tq, sb_w), lambda b, i: (b, i, 0)),
                  pl.BlockSpec((1, seq, sb_w), lambda b, i: (b, 0, 0)),
                  pl.BlockSpec((1, SB_WIDTH, seq), lambda b, i: (b, 0, 0)),
                  pl.BlockSpec((tk, 2 * tk), lambda b, i: (0, 0))],
        out_specs=pl.BlockSpec((1, tq, SB_WIDTH), lambda b, i: (b, i, 0)),
        out_shape=jax.ShapeDtypeStruct((bsz, seq, SB_WIDTH), BF16),
        scratch_shapes=[pltpu.VMEM((SB_HEADS, 1, tq), F32), pltpu.VMEM((SB_HEADS, HEAD_PAD, tq), F32),
                        pltpu.VMEM((SB_HEADS, tk, tq), F32), pltpu.VMEM((SB_HEADS, 2 * tk, tq), BF16),
                        pltpu.VMEM((SB_HEADS, 1, tq), F32), pltpu.VMEM((SB_HEADS, tk, tq), BF16)],
        compiler_params=_params("parallel", "arbitrary"),
        name="sb_attention",
    )(q, k, vt, tri2)


def _ssm_kernel(u_ref, bbd_ref, lam_ref, cbd_ref, d_ref, wglu_ref, bglu_ref, y_ref, h_ref, hs_ref, *, tt, nb):
    @pl.when(pl.program_id(0) == 0)
    def _():
        h_ref[...] = jnp.zeros_like(h_ref)

    u = u_ref[...]
    half, wide = SSM_BLOCK_STATES, 2 * SSM_BLOCK_STATES
    for b in range(SSM_BLOCKS):
        hs_ref[:, b * wide:(b + 1) * wide] = _dot(u[:, b * SSM_BLOCK_CH:(b + 1) * SSM_BLOCK_CH], bbd_ref[b])
    lam_re = jnp.broadcast_to(lam_ref[0:1, :], (nb, SSM_STATES))
    lam_im = jnp.broadcast_to(lam_ref[1:2, :], (nb, SSM_STATES))

    def step(t, carry):
        r0 = pl.multiple_of(t * nb, nb)
        out = []
        for b in range(SSM_BLOCKS):
            h_re, h_im = carry[b]
            l_re, l_im = lam_re[:, b * half:(b + 1) * half], lam_im[:, b * half:(b + 1) * half]
            re_cols = slice(b * wide, b * wide + half)
            im_cols = slice(b * wide + half, (b + 1) * wide)
            n_re = l_re * h_re - l_im * h_im + hs_ref[pl.ds(r0, nb), re_cols]
            n_im = l_re * h_im + l_im * h_re + hs_ref[pl.ds(r0, nb), im_cols]
            hs_ref[pl.ds(r0, nb), re_cols] = n_re
            hs_ref[pl.ds(r0, nb), im_cols] = n_im
            out.append((n_re, n_im))
        return tuple(out)

    init = tuple((h_ref[:, b * wide:b * wide + half], h_ref[:, b * wide + half:(b + 1) * wide]) for b in range(SSM_BLOCKS))
    final = lax.fori_loop(0, tt, step, init)
    for b in range(SSM_BLOCKS):
        h_ref[:, b * wide:b * wide + half] = final[b][0]
        h_ref[:, b * wide + half:(b + 1) * wide] = final[b][1]

    y = jnp.concatenate([_dot(hs_ref[:, b * wide:(b + 1) * wide].astype(BF16), cbd_ref[b]) for b in range(SSM_BLOCKS)],
                        axis=1) + d_ref[...] * u.astype(F32)
    y = jax.nn.gelu(y)
    y = y * jax.nn.sigmoid(_dot(y.astype(BF16), wglu_ref[...]) + bglu_ref[...])
    y_ref[...] = y.astype(BF16)


def _ssm(u_tm, bbd, lam, cbd, d_row, wglu, bglu_row, nb):
    rows = u_tm.shape[0]
    seq = rows // nb
    tt = min(seq, 128)
    full = lambda a: pl.BlockSpec(a.shape, lambda i: (0,) * a.ndim)
    return pl.pallas_call(
        functools.partial(_ssm_kernel, tt=tt, nb=nb),
        grid=(seq // tt,),
        in_specs=[pl.BlockSpec((tt * nb, SSM_WIDTH), lambda i: (i, 0)),
                  full(bbd), full(lam), full(cbd), full(d_row), full(wglu), full(bglu_row)],
        out_specs=pl.BlockSpec((tt * nb, SSM_WIDTH), lambda i: (i, 0)),
        out_shape=jax.ShapeDtypeStruct((rows, SSM_WIDTH), BF16),
        scratch_shapes=[pltpu.VMEM((nb, 2 * SSM_STATES), F32),
                        pltpu.VMEM((tt * nb, 2 * SSM_STATES), F32)],
        compiler_params=_params("arbitrary"),
        name="ssm",
    )(u_tm, bbd, lam, cbd, d_row, wglu, bglu_row)


def _memkv_kernel(mem_ref, w_ref, k_ref, v_ref):
    kv = _dot(mem_ref[0].astype(BF16), w_ref[...])
    k_ref[0] = kv[:, :D_MODEL].astype(BF16)
    v_ref[0] = kv[:, D_MODEL:].astype(BF16)


def _memkv(mem, w_mkv):
    bsz, mlen, _ = mem.shape
    return pl.pallas_call(
        _memkv_kernel,
        grid=(bsz,),
        in_specs=[pl.BlockSpec((1, mlen, D_MODEL), lambda b: (b, 0, 0)),
                  pl.BlockSpec(w_mkv.shape, lambda b: (0, 0))],
        out_specs=[pl.BlockSpec((1, mlen, D_MODEL), lambda b: (b, 0, 0))] * 2,
        out_shape=[jax.ShapeDtypeStruct((bsz, mlen, D_MODEL), BF16)] * 2,
        compiler_params=_params("parallel"),
        name="mem_kv",
    )(mem, w_mkv)


def _mix_mem_kernel(ya_ref, ys_ref, yb_ref, x_ref, gmix_ref, wout_ref, ln1_ref,
                    mk_ref, mv_ref, wmq_ref, wmo_ref, ln2_ref, xt_ref, *, alpha):
    gm = gmix_ref[...]
    c1, c2 = MLA_QK_W, MLA_QK_W + SSM_WIDTH
    ya = (_rms_rows(ya_ref[0].astype(F32), MLA_V_W) * gm[:, 0:c1]).astype(BF16)
    ys = (_rms_rows(ys_ref[...].astype(F32), SSM_WIDTH) * gm[:, c1:c2]).astype(BF16)
    yb = (_rms_rows(yb_ref[0].astype(F32), SB_WIDTH) * gm[:, c2:]).astype(BF16)
    mix = _dot(ya, wout_ref[0:c1, :]) + _dot(ys, wout_ref[c1:c2, :]) + _dot(yb, wout_ref[c2:, :])
    x1 = _layer_norm_rows(alpha * x_ref[0] + mix, ln1_ref[0:1, :], ln1_ref[1:2, :])

    q = _dot(x1.astype(BF16), wmq_ref[...]).astype(BF16)
    heads = []
    for h in range(MEM_HEADS):
        lo, hi = h * MEM_HEAD_DIM, (h + 1) * MEM_HEAD_DIM
        s = _dot_nt(q[:, lo:hi], mk_ref[0, :, lo:hi])
        p = jnp.exp(s - jnp.max(s, axis=-1, keepdims=True))
        p = p / jnp.sum(p, axis=-1, keepdims=True)
        heads.append(_dot(p.astype(BF16), mv_ref[0, :, lo:hi]).astype(BF16))
    o = jnp.concatenate(heads, axis=-1)
    x2 = _layer_norm_rows(alpha * x1 + _dot(o, wmo_ref[...]), ln2_ref[0:1, :], ln2_ref[1:2, :])
    xt_ref[...] = x2.T


def _mix_mem(ya, ys_tm, yb, x, gmix, wout, ln1, mk, mv, wmq, wmo, ln2, alpha):
    bsz, seq, _ = x.shape
    ts = min(seq, 512)
    nst = seq // ts
    mlen = mk.shape[1]
    tok = lambda w: pl.BlockSpec((1, ts, w), lambda b, s: (b, s, 0))
    full = lambda a: pl.BlockSpec(a.shape, lambda b, s: (0,) * a.ndim)
    return pl.pallas_call(
        functools.partial(_mix_mem_kernel, alpha=alpha),
        grid=(bsz, nst),
        in_specs=[tok(MLA_QK_W), pl.BlockSpec((ts, SSM_WIDTH), lambda b, s: (s, b)), tok(SB_WIDTH), tok(D_MODEL),
                  full(gmix), full(wout), full(ln1),
                  pl.BlockSpec((1, mlen, D_MODEL), lambda b, s: (b, 0, 0)),
                  pl.BlockSpec((1, mlen, D_MODEL), lambda b, s: (b, 0, 0)),
                  full(wmq), full(wmo), full(ln2)],
        out_specs=pl.BlockSpec((D_MODEL, ts), lambda b, s: (0, b * nst + s)),
        out_shape=jax.ShapeDtypeStruct((D_MODEL, bsz * seq), F32),
        compiler_params=_params("parallel", "parallel"),
        name="mix_mem",
    )(ya, ys_tm, yb, x, gmix, wout, ln1, mk, mv, wmq, wmo, ln2)


def _sort16_desc(v):
    v = list(v)
    for k in (2, 4, 8, 16):
        j = k // 2
        while j >= 1:
            for i in range(16):
                l = i ^ j
                if l > i:
                    hi, lo = jnp.maximum(v[i], v[l]), jnp.minimum(v[i], v[l])
                    v[i], v[l] = (hi, lo) if (i & k) == 0 else (lo, hi)
            j //= 2
    return v


def _merge_bitonic_desc(v):
    v = list(v)
    for j in (8, 4, 2, 1):
        for i in range(16):
            l = i ^ j
            if l > i:
                v[i], v[l] = jnp.maximum(v[i], v[l]), jnp.minimum(v[i], v[l])
    return v


def _top16_sorted(rows):
    v = _sort16_desc(rows)
    for shift in (4, 2, 1):
        partner = [pltpu.roll(x, shift, axis=0) for x in v]
        v = _merge_bitonic_desc([jnp.maximum(v[i], partner[15 - i]) for i in range(16)])
    return v


def _count_prefix(pred, v):
    p8 = pred(v[7])
    p4 = pred(jnp.where(p8, v[11], v[3]))
    p2 = pred(jnp.where(p8, jnp.where(p4, v[13], v[9]), jnp.where(p4, v[5], v[1])))
    p1 = pred(jnp.where(p8, jnp.where(p4, jnp.where(p2, v[14], v[12]), jnp.where(p2, v[10], v[8])),
                        jnp.where(p4, jnp.where(p2, v[6], v[4]), jnp.where(p2, v[2], v[0]))))
    count = (jnp.where(p8, 8.0, 0.0) + jnp.where(p4, 4.0, 0.0)) + (jnp.where(p2, 2.0, 0.0) + jnp.where(p1, 1.0, 0.0))
    return jnp.where(pred(v[15]), 16.0, count)


def _peer_route_kernel(xt_ref, wpq_ref, keys_ref, a_ref, n_ref, b_ref, r_ref, s1_ref, s2_ref):
    xb = xt_ref[...].astype(BF16)
    q = _dot(wpq_ref[...], xb).astype(BF16)
    s1_ref[...] = _dot(keys_ref[0, 0], q[0:PEER_KEY_DIM, :])
    s2_ref[...] = _dot(keys_ref[0, 1], q[PEER_KEY_DIM:, :])
    sub = lax.broadcasted_iota(jnp.int32, (8, LANE), 0)
    groups = PEER_N_KEYS // 8

    def spread(vals):
        out = vals[7]
        for j in range(6, -1, -1):
            out = jnp.where(sub == j, vals[j], out)
        return out

    def chunk(c, carry):
        ln = pl.ds(pl.multiple_of(c * LANE, LANE), LANE)
        rows1 = [s1_ref[8 * i:8 * i + 8, ln] for i in range(groups)]
        rows2 = [s2_ref[8 * i:8 * i + 8, ln] for i in range(groups)]
        v1 = _top16_sorted(rows1)
        v2 = _top16_sorted(rows2)
        v2lo, v2hi, v1hi = spread(v2[:8]), spread(v2[8:]), spread(v1[8:])
        cands = ([v1[0] + v2lo, v1[0] + v2hi] + [v1[r] + v2lo for r in range(1, 8)] + [v1hi + v2[0]]
                 + [v1[r] + v2hi for r in range(1, 7)])
        top = _top16_sorted(cands)
        thr = top[15]
        z = jnp.exp(top[0] - top[0])
        for k in range(1, PEER_TOPK):
            z = z + jnp.exp(top[k] - top[0])
        inv_z = 1.0 / z
        for i in range(groups):
            s = rows1[i]
            kept = _count_prefix(lambda t, s=s: s + t >= thr, v2)
            in_top = s >= v1[15]
            a_ref[i, 0, :, ln] = jnp.where(in_top, jnp.exp(s - v1[0]), 0.0)
            n_ref[i, 0, :, ln] = jnp.where(in_top, kept, 0.0)
        for i in range(groups // 2):
            bs, rs = [], []
            for s in (rows2[2 * i], rows2[2 * i + 1]):
                rank = _count_prefix(lambda t, s=s: t > s, v2)
                rs.append(rank)
                bs.append(jnp.where(rank < float(PEER_TOPK), jnp.exp(s - v2[0]) * inv_z, 0.0))
            b_ref[0, 16 * i:16 * i + 16, ln] = jnp.concatenate(bs, axis=0).astype(BF16)
            r_ref[0, 16 * i:16 * i + 16, ln] = jnp.concatenate(rs, axis=0).astype(BF16)
        return carry

    lax.fori_loop(0, xt_ref.shape[1] // LANE, chunk, 0)


def _peer_route(xt, wpq_t, keys):
    ntok = xt.shape[1]
    tt = min(ntok, 512)
    nblk = PEER_N_KEYS // 8
    return pl.pallas_call(
        _peer_route_kernel,
        grid=(ntok // tt, PEER_HEADS),
        in_specs=[pl.BlockSpec((D_MODEL, tt), lambda i, h: (0, i)),
                  pl.BlockSpec((2 * PEER_KEY_DIM, D_MODEL), lambda i, h: (h, 0)),
                  pl.BlockSpec((1, 2, PEER_N_KEYS, PEER_KEY_DIM), lambda i, h: (h, 0, 0, 0))],
        out_specs=[pl.BlockSpec((nblk, 1, 8, tt), lambda i, h: (0, h, 0, i)),
                   pl.BlockSpec((nblk, 1, 8, tt), lambda i, h: (0, h, 0, i)),
                   pl.BlockSpec((1, PEER_N_KEYS, tt), lambda i, h: (h, 0, i)),
                   pl.BlockSpec((1, PEER_N_KEYS, tt), lambda i, h: (h, 0, i))],
        out_shape=[jax.ShapeDtypeStruct((nblk, PEER_HEADS, 8, ntok), F32),
                   jax.ShapeDtypeStruct((nblk, PEER_HEADS, 8, ntok), F32),
                   jax.ShapeDtypeStruct((PEER_HEADS, PEER_N_KEYS, ntok), BF16),
                   jax.ShapeDtypeStruct((PEER_HEADS, PEER_N_KEYS, ntok), BF16)],
        scratch_shapes=[pltpu.VMEM((PEER_N_KEYS, tt), F32), pltpu.VMEM((PEER_N_KEYS, tt), F32)],
        compiler_params=_params("parallel", "arbitrary"),
        name="peer_route",
    )(xt, wpq_t, keys)


PEER_I1_PER_TILE = 8
PEER_TILE = PEER_I1_PER_TILE * PEER_N_KEYS


def _peer_dense_kernel(xt_ref, a_ref, n_ref, b_ref, r_ref, u_ref, vt_ref, ln_ref, o_ref,
                       acc_ref, xb_ref, gh_ref, *, alpha):
    e = pl.program_id(1)

    @pl.when(e == 0)
    def _():
        acc_ref[...] = jnp.zeros_like(acc_ref)
        xb_ref[...] = xt_ref[...].astype(BF16)

    tt = xb_ref.shape[1]
    rows = BF16_SUBLANES
    gh_ref[...] = _dot(u_ref[...], xb_ref[...]).astype(BF16)
    cw = min(tt, 2 * LANE)
    for j in range(PEER_I1_PER_TILE):
        for c in range(tt // cw):
            ln = slice(c * cw, (c + 1) * cw)
            a_rows = [jnp.broadcast_to(a_ref[0, h, j:j + 1, ln], (rows, cw)).astype(BF16) for h in range(PEER_HEADS)]
            n_rows = [jnp.broadcast_to(n_ref[0, h, j:j + 1, ln], (rows, cw)).astype(BF16) for h in range(PEER_HEADS)]
            for g in range(PEER_N_KEYS // rows):
                i2 = slice(g * rows, (g + 1) * rows)
                gate = None
                for h in range(PEER_HEADS):
                    term = jnp.where(r_ref[h, i2, ln] < n_rows[h], b_ref[h, i2, ln] * a_rows[h], jnp.zeros((), BF16))
                    gate = term if gate is None else gate + term
                e0 = j * PEER_N_KEYS + g * rows
                gh_ref[e0:e0 + rows, ln] = gate * _gelu_tanh(gh_ref[e0:e0 + rows, ln])
    acc_ref[...] += _dot(vt_ref[...], gh_ref[...])

    @pl.when(e == pl.num_programs(1) - 1)
    def _():
        y = alpha * xt_ref[...] + acc_ref[...]
        mu = jnp.mean(y, axis=0, keepdims=True)
        c = y - mu
        var = jnp.mean(c * c, axis=0, keepdims=True)
        o_ref[...] = (c * lax.rsqrt(var + NORM_EPS)).T * ln_ref[0:1, :] + ln_ref[1:2, :]


def _peer_dense(xt, a, n, b, r, u, vt, ln, alpha):
    ntok = xt.shape[1]
    tt = min(ntok, 1024)
    ntile = PEER_EXPERTS // PEER_TILE
    return pl.pallas_call(
        functools.partial(_peer_dense_kernel, alpha=alpha),
        grid=(ntok // tt, ntile),
        in_specs=[pl.BlockSpec((D_MODEL, tt), lambda i, e: (0, i)),
                  pl.BlockSpec((1, PEER_HEADS, PEER_I1_PER_TILE, tt), lambda i, e: (e, 0, 0, i)),
                  pl.BlockSpec((1, PEER_HEADS, PEER_I1_PER_TILE, tt), lambda i, e: (e, 0, 0, i)),
                  pl.BlockSpec((PEER_HEADS, PEER_N_KEYS, tt), lambda i, e: (0, 0, i)),
                  pl.BlockSpec((PEER_HEADS, PEER_N_KEYS, tt), lambda i, e: (0, 0, i)),
                  pl.BlockSpec((PEER_TILE, D_MODEL), lambda i, e: (e, 0)),
                  pl.BlockSpec((D_MODEL, PEER_TILE), lambda i, e: (0, e)),
                  pl.BlockSpec(ln.shape, lambda i, e: (0, 0))],
        out_specs=pl.BlockSpec((tt, D_MODEL), lambda i, e: (i, 0)),
        out_shape=jax.ShapeDtypeStruct((ntok, D_MODEL), F32),
        scratch_shapes=[pltpu.VMEM((D_MODEL, tt), F32), pltpu.VMEM((D_MODEL, tt), BF16),
                        pltpu.VMEM((PEER_TILE, tt), BF16)],
        compiler_params=_params("parallel", "arbitrary"),
        name="peer_dense",
    )(xt, a, n, b, r, u, vt, ln)


def _pad_heads(w, heads, width):
    rows = w.shape[0]
    w = w.reshape(rows, heads, width)
    return jnp.pad(w, ((0, 0), (0, 0), (0, HEAD_PAD - width))).reshape(rows, heads * HEAD_PAD)


def _rotate_half_cols(w, heads, width, nope):
    rows = w.shape[0]
    w = w.reshape(rows, heads, width)
    half = (width - nope) // 2
    x1, x2 = w[..., nope:nope + half], w[..., nope + half:]
    out = jnp.concatenate([jnp.zeros_like(w[..., :nope]), -x2, x1], axis=-1)
    return out.reshape(rows, heads * width)


def _pack_inproj(w_in, g_cq, g_ckv, w_uq, w_ukv):
    c = 0
    cols = {}
    for name, width in (("cq", MLA_Q_RANK), ("ckv", MLA_KV_RANK), ("kr", MLA_ROPE), ("ssm", SSM_WIDTH),
                        ("qsb", SB_WIDTH), ("ksb", SB_WIDTH), ("vsb", SB_WIDTH)):
        cols[name] = w_in[:, c:c + width]
        c += width
    rows = w_in.shape[0]
    zeros = lambda n: jnp.zeros((rows, n), F32)
    kr = cols["kr"]
    half = MLA_ROPE // 2
    kr_plain = jnp.concatenate([zeros(MLA_NOPE), kr, zeros(HEAD_PAD - MLA_NOPE - MLA_ROPE)], axis=1)
    kr_swap = jnp.concatenate([zeros(MLA_NOPE), -kr[:, half:], kr[:, :half],
                               zeros(HEAD_PAD - MLA_NOPE - MLA_ROPE)], axis=1)
    sb_scale = SB_DIM ** -0.5 * LOG2E
    win = jnp.concatenate([cols["cq"], cols["ckv"], kr_plain, kr_swap, cols["ssm"],
                           _pad_heads(cols["qsb"] * sb_scale, SB_HEADS, SB_DIM),
                           _pad_heads(cols["ksb"], SB_HEADS, SB_DIM), cols["vsb"]], axis=1).astype(BF16)

    qk_dim = MLA_NOPE + MLA_ROPE
    wq = w_uq * (g_cq[:, None] * (qk_dim ** -0.5 * LOG2E))
    wuq = jnp.concatenate([_pad_heads(wq, MLA_HEADS, qk_dim),
                           _pad_heads(_rotate_half_cols(wq, MLA_HEADS, qk_dim, MLA_NOPE), MLA_HEADS, qk_dim)],
                          axis=1).astype(BF16)
    wkv = (w_ukv * g_ckv[:, None]).reshape(MLA_KV_RANK, MLA_HEADS, MLA_NOPE + MLA_V)
    wukv = jnp.concatenate([_pad_heads(wkv[..., :MLA_NOPE].reshape(MLA_KV_RANK, -1), MLA_HEADS, MLA_NOPE),
                            _pad_heads(wkv[..., MLA_NOPE:].reshape(MLA_KV_RANK, -1), MLA_HEADS, MLA_V)],
                           axis=1).astype(BF16)
    return win, wuq, wukv


def _pad_mla_rows(w):
    cols = w.shape[1]
    head_rows = jnp.pad(w[:MLA_V_W].reshape(MLA_HEADS, MLA_V, cols), ((0, 0), (0, HEAD_PAD - MLA_V), (0, 0)))
    return jnp.concatenate([head_rows.reshape(MLA_QK_W, cols), w[MLA_V_W:]], axis=0)


def _pack_ssm(lam_re, lam_im, log_step, b_re, b_im, c_re, c_im, d_skip):
    step = jnp.exp(log_step)[:, None]
    decay = jnp.exp(lam_re * step)
    ab_re, ab_im = decay * jnp.cos(lam_im * step), decay * jnp.sin(lam_im * step)
    inv = 1.0 / (lam_re * lam_re + lam_im * lam_im)
    f_re = ((ab_re - 1.0) * lam_re + ab_im * lam_im) * inv
    f_im = (ab_im * lam_re - (ab_re - 1.0) * lam_im) * inv
    bb_re = f_re[..., None] * b_re - f_im[..., None] * b_im
    bb_im = f_re[..., None] * b_im + f_im[..., None] * b_re
    per = SSM_GROUPS // SSM_BLOCKS
    eye = jnp.eye(per, dtype=F32)
    split = lambda w: w.reshape((SSM_BLOCKS, per) + w.shape[1:])
    blk = lambda w: jnp.einsum("bgph,gk->bghkp", split(w), eye).reshape(SSM_BLOCKS, SSM_BLOCK_CH, SSM_BLOCK_STATES)
    bbd = jnp.concatenate([blk(bb_re), blk(bb_im)], axis=2).astype(BF16)
    blk_c = lambda w: jnp.einsum("bghp,gk->bgpkh", split(w), eye).reshape(SSM_BLOCKS, SSM_BLOCK_STATES, SSM_BLOCK_CH)
    cbd = jnp.concatenate([blk_c(c_re), blk_c(-c_im)], axis=1).astype(BF16)
    lam = jnp.stack([ab_re.reshape(-1), ab_im.reshape(-1)])
    return bbd, lam, cbd, d_skip.reshape(1, SSM_WIDTH)


def kernel(x, mem, positions, w_in, g_cq, g_ckv, w_uq, w_ukv, ssm_lam_re, ssm_lam_im, ssm_log_step, ssm_b_re, ssm_b_im, ssm_c_re, ssm_c_im, ssm_d, w_glu, b_glu, g_mix, w_out, ln_mix_g, ln_mix_b, w_mq, w_mkv, w_mo, ln_mem_g, ln_mem_b, w_pq, peer_sub_keys, peer_u, peer_v, ln_ffn_g, ln_ffn_b):
    bsz, seq, _ = x.shape
    depth = w_in.shape[0]
    alpha = (2 * depth) ** 0.25
    cos_t, sin_t = _rope_tables(positions)
    for l in range(depth):
        win, wuq, wukv = _pack_inproj(w_in[l], g_cq[l], g_ckv[l], w_uq[l], w_ukv[l])
        q, k, v, u_tm, q_sb, k_sb, v_sb = _inproj(x, win, wuq, wukv, cos_t, sin_t)
        y_mla = _mla_attention(q, k, v)
        y_sb = _sb_attention(q_sb, k_sb, v_sb)
        bbd, lam, cbd, d_row = _pack_ssm(ssm_lam_re[l], ssm_lam_im[l], ssm_log_step[l], ssm_b_re[l], ssm_b_im[l],
                                         ssm_c_re[l], ssm_c_im[l], ssm_d[l])
        y_ssm = _ssm(u_tm.reshape(seq * bsz, SSM_WIDTH), bbd, lam, cbd, d_row,
                     w_glu[l].astype(BF16), b_glu[l].reshape(1, SSM_WIDTH), bsz)
        mk, mv = _memkv(mem, w_mkv[l].astype(BF16))
        xt = _mix_mem(y_mla, y_ssm.reshape(seq, bsz * SSM_WIDTH), y_sb, x,
                      _pad_mla_rows(g_mix[l][:, None]).reshape(1, -1), _pad_mla_rows(w_out[l]).astype(BF16),
                      jnp.stack([ln_mix_g[l], ln_mix_b[l]]),
                      mk, mv, (w_mq[l] * MEM_HEAD_DIM ** -0.5).astype(BF16), w_mo[l].astype(BF16),
                      jnp.stack([ln_mem_g[l], ln_mem_b[l]]), alpha)
        a, n, b, r = _peer_route(xt, w_pq[l].T.astype(BF16), peer_sub_keys[l].astype(BF16))
        x = _peer_dense(xt, a, n, b, r, peer_u[l].astype(BF16), peer_v[l].T.astype(BF16),
                        jnp.stack([ln_ffn_g[l], ln_ffn_b[l]]), alpha).reshape(bsz, seq, D_MODEL)
    return x
```

```python
import functools
import math

import jax
import jax.numpy as jnp
from jax import lax
from jax.experimental import pallas as pl
from jax.experimental.pallas import tpu as pltpu

F32 = jnp.float32
BF16 = jnp.bfloat16

D_MODEL = 1024
CHUNK = 64
NORM_EPS = 1e-5
NEG_BIG = -1e30
LOG2E = math.log2(math.e)

MLA_HEADS = 6
MLA_NOPE = 64
MLA_ROPE = 32
MLA_V = 64
MLA_Q_RANK = 256
MLA_KV_RANK = 128
ROPE_THETA = 10000.0

SSM_GROUPS = 24
SSM_CH = 16
SSM_STATE = 64
SSM_WIDTH = SSM_GROUPS * SSM_CH
SSM_STATES = SSM_GROUPS * SSM_STATE
SSM_BLOCKS = 3
SSM_BLOCK_CH = SSM_WIDTH // SSM_BLOCKS
SSM_BLOCK_STATES = SSM_STATES // SSM_BLOCKS

SB_HEADS = 4
SB_DIM = 64
SB_WIDTH = SB_HEADS * SB_DIM

MEM_HEADS = 4
MEM_HEAD_DIM = D_MODEL // MEM_HEADS

PEER_HEADS = 8
PEER_N_KEYS = 128
PEER_TOPK = 16
PEER_KEY_DIM = 128
PEER_EXPERTS = PEER_N_KEYS * PEER_N_KEYS

LANE = 128
BF16_SUBLANES = 16
HEAD_PAD = 128
ATTN_TQ = 512
ATTN_TK = 256

VMEM_LIMIT = 56 * 1024 * 1024

_C_CQ = 0
_C_CKV = _C_CQ + MLA_Q_RANK
_C_KRP = _C_CKV + MLA_KV_RANK
_C_KRS = _C_KRP + HEAD_PAD
_C_SSM = _C_KRS + HEAD_PAD
_C_QSB = _C_SSM + SSM_WIDTH
_C_KSB = _C_QSB + SB_HEADS * HEAD_PAD
_C_VSB = _C_KSB + SB_HEADS * HEAD_PAD
_C_END = _C_VSB + SB_WIDTH
MLA_QK_W = MLA_HEADS * HEAD_PAD
MLA_V_W = MLA_HEADS * MLA_V


def _params(*sem):
    return pltpu.CompilerParams(dimension_semantics=sem, vmem_limit_bytes=VMEM_LIMIT)


def _dot(a, b):
    return jnp.dot(a, b, preferred_element_type=F32)


def _dot_nt(a, b):
    return lax.dot_general(a, b, (((1,), (1,)), ((), ())), preferred_element_type=F32)


def _gelu_tanh(x):
    c = math.sqrt(2.0 / math.pi)
    inner = x * (c + (c * 0.044715) * (x * x))
    return (0.5 * x) * (1.0 + jnp.tanh(inner))


def _layer_norm_rows(v, g, b):
    mu = jnp.mean(v, axis=-1, keepdims=True)
    c = v - mu
    var = jnp.mean(c * c, axis=-1, keepdims=True)
    return c * lax.rsqrt(var + NORM_EPS) * g + b


def _rms_rows(v, width):
    return v * lax.rsqrt(jnp.sum(v * v, axis=-1, keepdims=True) * (1.0 / width) + NORM_EPS)


def _rope_kernel(pos_ref, freq_ref, cos_ref, sin_ref):
    ang = pos_ref[0].astype(F32) * freq_ref[...]
    lane = lax.broadcasted_iota(jnp.int32, ang.shape, 1)
    rot = (lane >= MLA_NOPE) & (lane < MLA_NOPE + MLA_ROPE)
    cos_ref[0] = jnp.where(rot, jnp.cos(ang), jnp.where(lane < MLA_NOPE, 1.0, 0.0))
    sin_ref[0] = jnp.where(rot, jnp.sin(ang), 0.0)


def _rope_tables(positions):
    bsz, seq = positions.shape
    ts = min(seq, 512)
    half = MLA_ROPE // 2
    freq = ROPE_THETA ** (-jnp.arange(half, dtype=F32) / half)
    freq_row = jnp.zeros((1, HEAD_PAD), F32).at[0, MLA_NOPE:MLA_NOPE + MLA_ROPE].set(jnp.tile(freq, 2))
    return pl.pallas_call(
        _rope_kernel,
        grid=(bsz, seq // ts),
        in_specs=[pl.BlockSpec((1, ts, 1), lambda b, s: (b, s, 0)),
                  pl.BlockSpec((1, HEAD_PAD), lambda b, s: (0, 0))],
        out_specs=[pl.BlockSpec((1, ts, HEAD_PAD), lambda b, s: (b, s, 0))] * 2,
        out_shape=[jax.ShapeDtypeStruct((bsz, seq, HEAD_PAD), F32)] * 2,
        compiler_params=_params("parallel", "parallel"),
        name="rope_tables",
    )(positions.reshape(bsz, seq, 1), freq_row)


def _inproj_kernel(x_ref, win_ref, wuq_ref, wukv_ref, cos_ref, sin_ref, ones_ref,
                   q_ref, k_ref, v_ref, u_ref, qs_ref, ks_ref, vs_ref):
    xb = x_ref[0].astype(BF16)
    acc = _dot(xb, win_ref[...])
    cos = cos_ref[0]
    sin = sin_ref[0]
    cqn = _rms_rows(acc[:, _C_CQ:_C_CKV], MLA_Q_RANK).astype(BF16)
    qq = _dot(cqn, wuq_ref[...])
    ckvn = _rms_rows(acc[:, _C_CKV:_C_KRP], MLA_KV_RANK).astype(BF16)
    kv = _dot(ckvn, wukv_ref[...])
    k_rope = acc[:, _C_KRP:_C_KRS] * cos + acc[:, _C_KRS:_C_SSM] * sin
    for h in range(MLA_HEADS):
        lo, hi = h * HEAD_PAD, (h + 1) * HEAD_PAD
        q_ref[0, :, lo:hi] = (qq[:, lo:hi] * cos + qq[:, MLA_QK_W + lo:MLA_QK_W + hi] * sin).astype(BF16)
        k_ref[0, :, lo:hi] = (kv[:, lo:hi] + k_rope).astype(BF16)
    v_ref[0] = (kv[:, MLA_QK_W:] + ones_ref[...]).T.astype(BF16)
    u_ref[...] = acc[:, _C_SSM:_C_QSB].astype(BF16)
    qs_ref[0] = acc[:, _C_QSB:_C_KSB].astype(BF16)
    ks_ref[0] = acc[:, _C_KSB:_C_VSB].astype(BF16)
    vs_ref[0] = acc[:, _C_VSB:_C_END].T.astype(BF16)


def _inproj(x, win, wuq, wukv, cos_t, sin_t):
    bsz, seq, _ = x.shape
    ts = min(seq, 512)
    tok = lambda w: pl.BlockSpec((1, ts, w), lambda b, s: (b, s, 0))
    tok_t = lambda w: pl.BlockSpec((1, w, ts), lambda b, s: (b, 0, s))
    full = lambda a: pl.BlockSpec(a.shape, lambda b, s: (0,) * a.ndim)
    sb_w = SB_HEADS * HEAD_PAD
    lane = jnp.arange(MLA_QK_W) % HEAD_PAD
    ones_row = (lane == MLA_V).astype(F32).reshape(1, MLA_QK_W)
    return pl.pallas_call(
        _inproj_kernel,
        grid=(bsz, seq // ts),
        in_specs=[tok(D_MODEL), full(win), full(wuq), full(wukv), tok(HEAD_PAD), tok(HEAD_PAD), full(ones_row)],
        out_specs=[tok(MLA_QK_W), tok(MLA_QK_W), tok_t(MLA_QK_W),
                   pl.BlockSpec((ts, SSM_WIDTH), lambda b, s: (s, b)),
                   tok(sb_w), tok(sb_w), tok_t(SB_WIDTH)],
        out_shape=[jax.ShapeDtypeStruct((bsz, seq, MLA_QK_W), BF16),
                   jax.ShapeDtypeStruct((bsz, seq, MLA_QK_W), BF16),
                   jax.ShapeDtypeStruct((bsz, MLA_QK_W, seq), BF16),
                   jax.ShapeDtypeStruct((seq, bsz * SSM_WIDTH), BF16),
                   jax.ShapeDtypeStruct((bsz, seq, sb_w), BF16),
                   jax.ShapeDtypeStruct((bsz, seq, sb_w), BF16),
                   jax.ShapeDtypeStruct((bsz, SB_WIDTH, seq), BF16)],
        compiler_params=_params("parallel", "parallel"),
        name="inproj",
    )(x, win, wuq, wukv, cos_t, sin_t, ones_row)


def _mla_kernel(q_ref, k_ref, vt_ref, o_ref, m_ref, acc_ref, s_ref, p_ref, *, tq, tk):
    qi = pl.program_id(1)
    per = tq // tk
    query_chunk = (qi * tq + lax.broadcasted_iota(jnp.int32, (tk, tq), 1)) // CHUNK
    key_in_tile = lax.broadcasted_iota(jnp.int32, (tk, tq), 0)
    m_ref[...] = jnp.full(m_ref.shape, NEG_BIG, F32)
    acc_ref[...] = jnp.zeros(acc_ref.shape, F32)

    def block(kb, masked):
        ks = pl.multiple_of(kb * tk, tk)
        for h in range(MLA_HEADS):
            tile = slice(h * HEAD_PAD, (h + 1) * HEAD_PAD)
            s_ref[h] = _dot_nt(k_ref[0, pl.ds(ks, tk), tile], q_ref[0, :, tile])
        if masked:
            allowed = (ks + key_in_tile) // CHUNK <= query_chunk
        alphas = []
        for h in range(MLA_HEADS):
            s = s_ref[h]
            if masked:
                s = jnp.where(allowed, s, NEG_BIG)
            m_old = m_ref[h]
            m_new = jnp.maximum(m_old, jnp.max(s, axis=0, keepdims=True))
            p_ref[h] = jnp.exp2(s - m_new).astype(BF16)
            alphas.append(jnp.exp2(m_old - m_new))
            m_ref[h] = m_new
        for h in range(MLA_HEADS):
            tile = slice(h * HEAD_PAD, (h + 1) * HEAD_PAD)
            acc_ref[h] = alphas[h] * acc_ref[h] + _dot(vt_ref[0, tile, pl.ds(ks, tk)], p_ref[h])

    def body(kb, carry):
        block(kb, False)
        return carry

    lax.fori_loop(0, qi * per, body, 0)
    for d in range(per):
        block(qi * per + d, True)
    row = lax.broadcasted_iota(jnp.int32, (HEAD_PAD, tq), 0)
    for h in range(MLA_HEADS):
        acc = acc_ref[h]
        out_t = jnp.where(row < MLA_V, acc / acc[MLA_V:MLA_V + 1, :], 0.0)
        o_ref[0, :, h * HEAD_PAD:(h + 1) * HEAD_PAD] = out_t.T.astype(BF16)


def _mla_attention(q, k, vt):
    bsz, seq, _ = q.shape
    tq, tk = min(seq, ATTN_TQ), min(seq, ATTN_TK)
    return pl.pallas_call(
        functools.partial(_mla_kernel, tq=tq, tk=tk),
        grid=(bsz, seq // tq),
        in_specs=[pl.BlockSpec((1, tq, MLA_QK_W), lambda b, i: (b, i, 0)),
                  pl.BlockSpec((1, seq, MLA_QK_W), lambda b, i: (b, 0, 0)),
                  pl.BlockSpec((1, MLA_QK_W, seq), lambda b, i: (b, 0, 0))],
        out_specs=pl.BlockSpec((1, tq, MLA_QK_W), lambda b, i: (b, i, 0)),
        out_shape=jax.ShapeDtypeStruct((bsz, seq, MLA_QK_W), BF16),
        scratch_shapes=[pltpu.VMEM((MLA_HEADS, 1, tq), F32), pltpu.VMEM((MLA_HEADS, HEAD_PAD, tq), F32),
                        pltpu.VMEM((MLA_HEADS, tk, tq), F32), pltpu.VMEM((MLA_HEADS, tk, tq), BF16)],
        compiler_params=_params("parallel", "arbitrary"),
        name="mla_attention",
    )(q, k, vt)


def _sb_kernel(q_ref, k_ref, vt_ref, tri_ref, o_ref, right_ref, acc_ref, z_ref, split_ref, sum_ref, w_ref, *, tq, tk):
    qi = pl.program_id(1)
    per = tq // tk
    query_pos = qi * tq + lax.broadcasted_iota(jnp.int32, (tk, tq), 1)
    key_in_tile = lax.broadcasted_iota(jnp.int32, (tk, tq), 0)
    right_ref[...] = jnp.zeros(right_ref.shape, F32)
    acc_ref[...] = jnp.zeros(acc_ref.shape, F32)

    def block(kb, masked):
        ks = pl.multiple_of(kb * tk, tk)
        for h in range(SB_HEADS):
            tile = slice(h * HEAD_PAD, (h + 1) * HEAD_PAD)
            z_ref[h] = _dot_nt(k_ref[0, pl.ds(ks, tk), tile], q_ref[0, :, tile])
        if masked:
            earlier = ks + key_in_tile < query_pos
        for h in range(SB_HEADS):
            z = z_ref[h]
            fail = jnp.maximum(z, 0.0) + jnp.log2(1.0 + jnp.exp2(-jnp.abs(z)))
            z_ref[h] = z - fail
            if masked:
                fail = jnp.where(earlier, fail, 0.0)
            hi = fail.astype(BF16)
            split_ref[h, 0:tk, :] = hi
            split_ref[h, tk:, :] = (fail - hi.astype(F32)).astype(BF16)
            sum_ref[h] = jnp.sum(fail, axis=0, keepdims=True)
        for h in range(SB_HEADS):
            between = _dot(tri_ref[...], split_ref[h]) + right_ref[h]
            w = jnp.exp2(z_ref[h] - between)
            if masked:
                w = jnp.where(earlier, w, 0.0)
            w_ref[h] = w.astype(BF16)
            right_ref[h] += sum_ref[h]
        for h in range(SB_HEADS):
            pair = slice((h // 2) * HEAD_PAD, (h // 2 + 1) * HEAD_PAD)
            acc_ref[h] += _dot(vt_ref[0, pair, pl.ds(ks, tk)], w_ref[h])

    def body(i, carry):
        block(qi * per - 1 - i, False)
        return carry

    for d in range(per - 1, -1, -1):
        block(qi * per + d, True)
    lax.fori_loop(0, qi * per, body, 0)
    row = lax.broadcasted_iota(jnp.int32, (HEAD_PAD, tq), 0)
    for pair in range(SB_HEADS // 2):
        out_t = jnp.where(row < SB_DIM, acc_ref[2 * pair], acc_ref[2 * pair + 1])
        o_ref[0, :, pair * HEAD_PAD:(pair + 1) * HEAD_PAD] = out_t.T.astype(BF16)


def _sb_attention(q, k, vt):
    bsz, seq, _ = q.shape
    tq, tk = min(seq, ATTN_TQ), min(seq, ATTN_TK)
    sb_w = SB_HEADS * HEAD_PAD
    tri = (lax.broadcasted_iota(jnp.int32, (tk, tk), 1) > lax.broadcasted_iota(jnp.int32, (tk, tk), 0)).astype(BF16)
    tri2 = jnp.concatenate([tri, tri], axis=1)
    return pl.pallas_call(
        functools.partial(_sb_kernel, tq=tq, tk=tk),
        grid=(bsz, seq // tq),
        in_specs=[pl.BlockSpec((1, tq, sb_w), lambda b, i: (b, i, 0)),
                  pl.BlockSpec((1, seq, sb_w), lambda b, i: (b, 0, 0)),
                  pl.BlockSpec((1, SB_WIDTH, seq), lambda b, i: (b, 0, 0)),
                  pl.BlockSpec((tk, 2 * tk), lambda b, i: (0, 0))],
        out_specs=pl.BlockSpec((1, tq, SB_WIDTH), lambda b, i: (b, i, 0)),
        out_shape=jax.ShapeDtypeStruct((bsz, seq, SB_WIDTH), BF16),
        scratch_shapes=[pltpu.VMEM((SB_HEADS, 1, tq), F32), pltpu.VMEM((SB_HEADS, HEAD_PAD, tq), F32),
                        pltpu.VMEM((SB_HEADS, tk, tq), F32), pltpu.VMEM((SB_HEADS, 2 * tk, tq), BF16),
                        pltpu.VMEM((SB_HEADS, 1, tq), F32), pltpu.VMEM((SB_HEADS, tk, tq), BF16)],
        compiler_params=_params("parallel", "arbitrary"),
        name="sb_attention",
    )(q, k, vt, tri2)


def _ssm_kernel(u_ref, bbd_ref, lam_ref, cbd_ref, d_ref, wglu_ref, bglu_ref, y_ref, h_ref, hs_ref, *, tt, nb):
    @pl.when(pl.program_id(0) == 0)
    def _():
        h_ref[...] = jnp.zeros_like(h_ref)

    u = u_ref[...]
    half, wide = SSM_BLOCK_STATES, 2 * SSM_BLOCK_STATES
    for b in range(SSM_BLOCKS):
        hs_ref[:, b * wide:(b + 1) * wide] = _dot(u[:, b * SSM_BLOCK_CH:(b + 1) * SSM_BLOCK_CH], bbd_ref[b])
    lam_re = jnp.broadcast_to(lam_ref[0:1, :], (nb, SSM_STATES))
    lam_im = jnp.broadcast_to(lam_ref[1:2, :], (nb, SSM_STATES))

    def step(t, carry):
        r0 = pl.multiple_of(t * nb, nb)
        out = []
        for b in range(SSM_BLOCKS):
            h_re, h_im = carry[b]
            l_re, l_im = lam_re[:, b * half:(b + 1) * half], lam_im[:, b * half:(b + 1) * half]
            re_cols = slice(b * wide, b * wide + half)
            im_cols = slice(b * wide + half, (b + 1) * wide)
            n_re = l_re * h_re - l_im * h_im + hs_ref[pl.ds(r0, nb), re_cols]
            n_im = l_re * h_im + l_im * h_re + hs_ref[pl.ds(r0, nb), im_cols]
            hs_ref[pl.ds(r0, nb), re_cols] = n_re
            hs_ref[pl.ds(r0, nb), im_cols] = n_im
            out.append((n_re, n_im))
        return tuple(out)

    init = tuple((h_ref[:, b * wide:b * wide + half], h_ref[:, b * wide + half:(b + 1) * wide]) for b in range(SSM_BLOCKS))
    final = lax.fori_loop(0, tt, step, init)
    for b in range(SSM_BLOCKS):
        h_ref[:, b * wide:b * wide + half] = final[b][0]
        h_ref[:, b * wide + half:(b + 1) * wide] = final[b][1]

    y = jnp.concatenate([_dot(hs_ref[:, b * wide:(b + 1) * wide].astype(BF16), cbd_ref[b]) for b in range(SSM_BLOCKS)],
                        axis=1) + d_ref[...] * u.astype(F32)
    y = jax.nn.gelu(y)
    y = y * jax.nn.sigmoid(_dot(y.astype(BF16), wglu_ref[...]) + bglu_ref[...])
    y_ref[...] = y.astype(BF16)


def _ssm(u_tm, bbd, lam, cbd, d_row, wglu, bglu_row, nb):
    rows = u_tm.shape[0]
    seq = rows // nb
    tt = min(seq, 128)
    full = lambda a: pl.BlockSpec(a.shape, lambda i: (0,) * a.ndim)
    return pl.pallas_call(
        functools.partial(_ssm_kernel, tt=tt, nb=nb),
        grid=(seq // tt,),
        in_specs=[pl.BlockSpec((tt * nb, SSM_WIDTH), lambda i: (i, 0)),
                  full(bbd), full(lam), full(cbd), full(d_row), full(wglu), full(bglu_row)],
        out_specs=pl.BlockSpec((tt * nb, SSM_WIDTH), lambda i: (i, 0)),
        out_shape=jax.ShapeDtypeStruct((rows, SSM_WIDTH), BF16),
        scratch_shapes=[pltpu.VMEM((nb, 2 * SSM_STATES), F32),
                        pltpu.VMEM((tt * nb, 2 * SSM_STATES), F32)],
        compiler_params=_params("arbitrary"),
        name="ssm",
    )(u_tm, bbd, lam, cbd, d_row, wglu, bglu_row)


def _memkv_kernel(mem_ref, w_ref, k_ref, v_ref):
    kv = _dot(mem_ref[0].astype(BF16), w_ref[...])
    k_ref[0] = kv[:, :D_MODEL].astype(BF16)
    v_ref[0] = kv[:, D_MODEL:].astype(BF16)


def _memkv(mem, w_mkv):
    bsz, mlen, _ = mem.shape
    return pl.pallas_call(
        _memkv_kernel,
        grid=(bsz,),
        in_specs=[pl.BlockSpec((1, mlen, D_MODEL), lambda b: (b, 0, 0)),
                  pl.BlockSpec(w_mkv.shape, lambda b: (0, 0))],
        out_specs=[pl.BlockSpec((1, mlen, D_MODEL), lambda b: (b, 0, 0))] * 2,
        out_shape=[jax.ShapeDtypeStruct((bsz, mlen, D_MODEL), BF16)] * 2,
        compiler_params=_params("parallel"),
        name="mem_kv",
    )(mem, w_mkv)


def _mix_mem_kernel(ya_ref, ys_ref, yb_ref, x_ref, gmix_ref, wout_ref, ln1_ref,
                    mk_ref, mv_ref, wmq_ref, wmo_ref, ln2_ref, xt_ref,
                    x1_ref, xq_ref, s_ref, p_ref, o_ref, *, alpha, halves):
    gm = gmix_ref[...]
    c1, c2 = MLA_QK_W, MLA_QK_W + SSM_WIDTH
    rows = x_ref.shape[1] // halves
    parts = [slice(i * rows, (i + 1) * rows) for i in range(halves)]
    for r in parts:
        ya = (_rms_rows(ya_ref[0, r, :].astype(F32), MLA_V_W) * gm[:, 0:c1]).astype(BF16)
        ys = (_rms_rows(ys_ref[r, :].astype(F32), SSM_WIDTH) * gm[:, c1:c2]).astype(BF16)
        yb = (_rms_rows(yb_ref[0, r, :].astype(F32), SB_WIDTH) * gm[:, c2:]).astype(BF16)
        x1_ref[r, :] = _dot(ya, wout_ref[0:c1, :]) + _dot(ys, wout_ref[c1:c2, :]) + _dot(yb, wout_ref[c2:, :])
    for r in parts:
        x1 = _layer_norm_rows(alpha * x_ref[0, r, :] + x1_ref[r, :], ln1_ref[0:1, :], ln1_ref[1:2, :])
        x1_ref[r, :] = x1
        xq_ref[r, :] = x1.astype(BF16)
    for r in parts:
        xq_ref[r, :] = _dot(xq_ref[r, :], wmq_ref[...]).astype(BF16)
    for i, r in enumerate(parts):
        for h in range(MEM_HEADS):
            cols = slice(h * MEM_HEAD_DIM, (h + 1) * MEM_HEAD_DIM)
            s_ref[i * MEM_HEADS + h] = _dot_nt(xq_ref[r, cols], mk_ref[0, :, cols])
    for i in range(halves * MEM_HEADS):
        s = s_ref[i]
        p = jnp.exp(s - jnp.max(s, axis=-1, keepdims=True))
        p_ref[i] = (p / jnp.sum(p, axis=-1, keepdims=True)).astype(BF16)
    for i, r in enumerate(parts):
        for h in range(MEM_HEADS):
            cols = slice(h * MEM_HEAD_DIM, (h + 1) * MEM_HEAD_DIM)
            o_ref[r, cols] = _dot(p_ref[i * MEM_HEADS + h], mv_ref[0, :, cols]).astype(BF16)
    for r in parts:
        x2 = _layer_norm_rows(alpha * x1_ref[r, :] + _dot(o_ref[r, :], wmo_ref[...]), ln2_ref[0:1, :], ln2_ref[1:2, :])
        xt_ref[:, r] = x2.T


def _mix_mem(ya, ys_tm, yb, x, gmix, wout, ln1, mk, mv, wmq, wmo, ln2, alpha):
    bsz, seq, _ = x.shape
    ts = min(seq, 512)
    nst = seq // ts
    halves = 2
    mlen = mk.shape[1]
    tok = lambda w: pl.BlockSpec((1, ts, w), lambda b, s: (b, s, 0))
    full = lambda a: pl.BlockSpec(a.shape, lambda b, s: (0,) * a.ndim)
    return pl.pallas_call(
        functools.partial(_mix_mem_kernel, alpha=alpha, halves=halves),
        grid=(bsz, nst),
        in_specs=[tok(MLA_QK_W), pl.BlockSpec((ts, SSM_WIDTH), lambda b, s: (s, b)), tok(SB_WIDTH), tok(D_MODEL),
                  full(gmix), full(wout), full(ln1),
                  pl.BlockSpec((1, mlen, D_MODEL), lambda b, s: (b, 0, 0)),
                  pl.BlockSpec((1, mlen, D_MODEL), lambda b, s: (b, 0, 0)),
                  full(wmq), full(wmo), full(ln2)],
        out_specs=pl.BlockSpec((D_MODEL, ts), lambda b, s: (0, b * nst + s)),
        out_shape=jax.ShapeDtypeStruct((D_MODEL, bsz * seq), F32),
        scratch_shapes=[pltpu.VMEM((ts, D_MODEL), F32), pltpu.VMEM((ts, D_MODEL), BF16),
                        pltpu.VMEM((halves * MEM_HEADS, ts // halves, mlen), F32),
                        pltpu.VMEM((halves * MEM_HEADS, ts // halves, mlen), BF16),
                        pltpu.VMEM((ts, D_MODEL), BF16)],
        compiler_params=_params("parallel", "parallel"),
        name="mix_mem",
    )(ya, ys_tm, yb, x, gmix, wout, ln1, mk, mv, wmq, wmo, ln2)


def _sort16_desc(v):
    v = list(v)
    for k in (2, 4, 8, 16):
        j = k // 2
        while j >= 1:
            for i in range(16):
                l = i ^ j
                if l > i:
                    hi, lo = jnp.maximum(v[i], v[l]), jnp.minimum(v[i], v[l])
                    v[i], v[l] = (hi, lo) if (i & k) == 0 else (lo, hi)
            j //= 2
    return v


def _merge_bitonic_desc(v):
    v = list(v)
    for j in (8, 4, 2, 1):
        for i in range(16):
            l = i ^ j
            if l > i:
                v[i], v[l] = jnp.maximum(v[i], v[l]), jnp.minimum(v[i], v[l])
    return v


def _top16_sorted(rows):
    v = _sort16_desc(rows)
    for shift in (4, 2, 1):
        partner = [pltpu.roll(x, shift, axis=0) for x in v]
        v = _merge_bitonic_desc([jnp.maximum(v[i], partner[15 - i]) for i in range(16)])
    return v


def _count_prefix(pred, v):
    p8 = pred(v[7])
    p4 = pred(jnp.where(p8, v[11], v[3]))
    p2 = pred(jnp.where(p8, jnp.where(p4, v[13], v[9]), jnp.where(p4, v[5], v[1])))
    p1 = pred(jnp.where(p8, jnp.where(p4, jnp.where(p2, v[14], v[12]), jnp.where(p2, v[10], v[8])),
                        jnp.where(p4, jnp.where(p2, v[6], v[4]), jnp.where(p2, v[2], v[0]))))
    count = (jnp.where(p8, 8.0, 0.0) + jnp.where(p4, 4.0, 0.0)) + (jnp.where(p2, 2.0, 0.0) + jnp.where(p1, 1.0, 0.0))
    return jnp.where(pred(v[15]), 16.0, count)


def _peer_route_kernel(xt_ref, wpq_ref, keys_ref, a_ref, n_ref, b_ref, r_ref, q_ref, s1_ref, s2_ref):
    q_ref[...] = _dot(wpq_ref[...], xt_ref[...].astype(BF16)).astype(BF16)
    lax.fori_loop(0, PEER_HEADS, functools.partial(_peer_route_head, q_ref, keys_ref, a_ref, n_ref, b_ref, r_ref,
                                                   s1_ref, s2_ref), 0)


def _peer_route_head(q_ref, keys_ref, a_ref, n_ref, b_ref, r_ref, s1_ref, s2_ref, h, carry):
    q0 = pl.multiple_of(h * (2 * PEER_KEY_DIM), 2 * PEER_KEY_DIM)
    s1_ref[...] = _dot(keys_ref[h, 0], q_ref[pl.ds(q0, PEER_KEY_DIM), :])
    s2_ref[...] = _dot(keys_ref[h, 1], q_ref[pl.ds(q0 + PEER_KEY_DIM, PEER_KEY_DIM), :])
    sub = lax.broadcasted_iota(jnp.int32, (8, LANE), 0)
    groups = PEER_N_KEYS // 8

    def spread(vals):
        out = vals[7]
        for j in range(6, -1, -1):
            out = jnp.where(sub == j, vals[j], out)
        return out

    def chunk(c, carry):
        ln = pl.ds(pl.multiple_of(c * LANE, LANE), LANE)
        rows1 = [s1_ref[8 * i:8 * i + 8, ln] for i in range(groups)]
        rows2 = [s2_ref[8 * i:8 * i + 8, ln] for i in range(groups)]
        v1 = _top16_sorted(rows1)
        v2 = _top16_sorted(rows2)
        v2lo, v2hi, v1hi = spread(v2[:8]), spread(v2[8:]), spread(v1[8:])
        cands = ([v1[0] + v2lo, v1[0] + v2hi] + [v1[r] + v2lo for r in range(1, 8)] + [v1hi + v2[0]]
                 + [v1[r] + v2hi for r in range(1, 7)])
        top = _top16_sorted(cands)
        thr = top[15]
        z = jnp.exp(top[0] - top[0])
        for k in range(1, PEER_TOPK):
            z = z + jnp.exp(top[k] - top[0])
        inv_z = 1.0 / z
        for i in range(groups):
            s = rows1[i]
            kept = _count_prefix(lambda t, s=s: s + t >= thr, v2)
            in_top = s >= v1[15]
            a_ref[i, h, :, ln] = jnp.where(in_top, jnp.exp(s - v1[0]), 0.0)
            n_ref[i, h, :, ln] = jnp.where(in_top, kept, 0.0)
        for i in range(groups // 2):
            bs, rs = [], []
            for s in (rows2[2 * i], rows2[2 * i + 1]):
                rank = _count_prefix(lambda t, s=s: t > s, v2)
                rs.append(rank)
                bs.append(jnp.where(rank < float(PEER_TOPK), jnp.exp(s - v2[0]) * inv_z, 0.0))
            b_ref[h, 16 * i:16 * i + 16, ln] = jnp.concatenate(bs, axis=0).astype(BF16)
            r_ref[h, 16 * i:16 * i + 16, ln] = jnp.concatenate(rs, axis=0).astype(BF16)
        return carry

    lax.fori_loop(0, s1_ref.shape[1] // LANE, chunk, 0)
    return carry


def _peer_route(xt, wpq_t, keys):
    ntok = xt.shape[1]
    tt = min(ntok, 512)
    nblk = PEER_N_KEYS // 8
    return pl.pallas_call(
        _peer_route_kernel,
        grid=(ntok // tt,),
        in_specs=[pl.BlockSpec((D_MODEL, tt), lambda i: (0, i)),
                  pl.BlockSpec(wpq_t.shape, lambda i: (0, 0)),
                  pl.BlockSpec(keys.shape, lambda i: (0, 0, 0, 0))],
        out_specs=[pl.BlockSpec((nblk, PEER_HEADS, 8, tt), lambda i: (0, 0, 0, i)),
                   pl.BlockSpec((nblk, PEER_HEADS, 8, tt), lambda i: (0, 0, 0, i)),
                   pl.BlockSpec((PEER_HEADS, PEER_N_KEYS, tt), lambda i: (0, 0, i)),
                   pl.BlockSpec((PEER_HEADS, PEER_N_KEYS, tt), lambda i: (0, 0, i))],
        out_shape=[jax.ShapeDtypeStruct((nblk, PEER_HEADS, 8, ntok), F32),
                   jax.ShapeDtypeStruct((nblk, PEER_HEADS, 8, ntok), F32),
                   jax.ShapeDtypeStruct((PEER_HEADS, PEER_N_KEYS, ntok), BF16),
                   jax.ShapeDtypeStruct((PEER_HEADS, PEER_N_KEYS, ntok), BF16)],
        scratch_shapes=[pltpu.VMEM((PEER_HEADS * 2 * PEER_KEY_DIM, tt), BF16),
                        pltpu.VMEM((PEER_N_KEYS, tt), F32), pltpu.VMEM((PEER_N_KEYS, tt), F32)],
        compiler_params=_params("parallel"),
        name="peer_route",
    )(xt, wpq_t, keys)


PEER_I1_PER_TILE = 8
PEER_TILE = PEER_I1_PER_TILE * PEER_N_KEYS


def _peer_dense_kernel(xt_ref, a_ref, n_ref, b_ref, r_ref, u_ref, vt_ref, ln_ref, o_ref,
                       acc_ref, xb_ref, gh_ref, *, alpha):
    e = pl.program_id(1)

    @pl.when(e == 0)
    def _():
        acc_ref[...] = jnp.zeros_like(acc_ref)
        xb_ref[...] = xt_ref[...].astype(BF16)

    tt = xb_ref.shape[1]
    rows = BF16_SUBLANES
    gh_ref[...] = _dot(u_ref[...], xb_ref[...]).astype(BF16)
    cw = min(tt, 2 * LANE)
    for j in range(PEER_I1_PER_TILE):
        for c in range(tt // cw):
            ln = slice(c * cw, (c + 1) * cw)
            a_rows = [jnp.broadcast_to(a_ref[0, h, j:j + 1, ln], (rows, cw)).astype(BF16) for h in range(PEER_HEADS)]
            n_rows = [jnp.broadcast_to(n_ref[0, h, j:j + 1, ln], (rows, cw)).astype(BF16) for h in range(PEER_HEADS)]
            for g in range(PEER_N_KEYS // rows):
                i2 = slice(g * rows, (g + 1) * rows)
                gate = None
                for h in range(PEER_HEADS):
                    term = jnp.where(r_ref[h, i2, ln] < n_rows[h], b_ref[h, i2, ln] * a_rows[h], jnp.zeros((), BF16))
                    gate = term if gate is None else gate + term
                e0 = j * PEER_N_KEYS + g * rows
                gh_ref[e0:e0 + rows, ln] = gate * _gelu_tanh(gh_ref[e0:e0 + rows, ln])
    acc_ref[...] += _dot(vt_ref[...], gh_ref[...])

    @pl.when(e == pl.num_programs(1) - 1)
    def _():
        y = alpha * xt_ref[...] + acc_ref[...]
        mu = jnp.mean(y, axis=0, keepdims=True)
        c = y - mu
        var = jnp.mean(c * c, axis=0, keepdims=True)
        o_ref[...] = (c * lax.rsqrt(var + NORM_EPS)).T * ln_ref[0:1, :] + ln_ref[1:2, :]


def _peer_dense(xt, a, n, b, r, u, vt, ln, alpha):
    ntok = xt.shape[1]
    tt = min(ntok, 1024)
    ntile = PEER_EXPERTS // PEER_TILE
    return pl.pallas_call(
        functools.partial(_peer_dense_kernel, alpha=alpha),
        grid=(ntok // tt, ntile),
        in_specs=[pl.BlockSpec((D_MODEL, tt), lambda i, e: (0, i)),
                  pl.BlockSpec((1, PEER_HEADS, PEER_I1_PER_TILE, tt), lambda i, e: (e, 0, 0, i)),
                  pl.BlockSpec((1, PEER_HEADS, PEER_I1_PER_TILE, tt), lambda i, e: (e, 0, 0, i)),
                  pl.BlockSpec((PEER_HEADS, PEER_N_KEYS, tt), lambda i, e: (0, 0, i)),
                  pl.BlockSpec((PEER_HEADS, PEER_N_KEYS, tt), lambda i, e: (0, 0, i)),
                  pl.BlockSpec((PEER_TILE, D_MODEL), lambda i, e: (e, 0)),
                  pl.BlockSpec((D_MODEL, PEER_TILE), lambda i, e: (0, e)),
                  pl.BlockSpec(ln.shape, lambda i, e: (0, 0))],
        out_specs=pl.BlockSpec((tt, D_MODEL), lambda i, e: (i, 0)),
        out_shape=jax.ShapeDtypeStruct((ntok, D_MODEL), F32),
        scratch_shapes=[pltpu.VMEM((D_MODEL, tt), F32), pltpu.VMEM((D_MODEL, tt), BF16),
                        pltpu.VMEM((PEER_TILE, tt), BF16)],
        compiler_params=_params("parallel", "arbitrary"),
        name="peer_dense",
    )(xt, a, n, b, r, u, vt, ln)


def _pad_heads(w, heads, width):
    rows = w.shape[0]
    w = w.reshape(rows, heads, width)
    return jnp.pad(w, ((0, 0), (0, 0), (0, HEAD_PAD - width))).reshape(rows, heads * HEAD_PAD)


def _rotate_half_cols(w, heads, width, nope):
    rows = w.shape[0]
    w = w.reshape(rows, heads, width)
    half = (width - nope) // 2
    x1, x2 = w[..., nope:nope + half], w[..., nope + half:]
    out = jnp.concatenate([jnp.zeros_like(w[..., :nope]), -x2, x1], axis=-1)
    return out.reshape(rows, heads * width)


def _pack_inproj(w_in, g_cq, g_ckv, w_uq, w_ukv):
    c = 0
    cols = {}
    for name, width in (("cq", MLA_Q_RANK), ("ckv", MLA_KV_RANK), ("kr", MLA_ROPE), ("ssm", SSM_WIDTH),
                        ("qsb", SB_WIDTH), ("ksb", SB_WIDTH), ("vsb", SB_WIDTH)):
        cols[name] = w_in[:, c:c + width]
        c += width
    rows = w_in.shape[0]
    zeros = lambda n: jnp.zeros((rows, n), F32)
    kr = cols["kr"]
    half = MLA_ROPE // 2
    kr_plain = jnp.concatenate([zeros(MLA_NOPE), kr, zeros(HEAD_PAD - MLA_NOPE - MLA_ROPE)], axis=1)
    kr_swap = jnp.concatenate([zeros(MLA_NOPE), -kr[:, half:], kr[:, :half],
                               zeros(HEAD_PAD - MLA_NOPE - MLA_ROPE)], axis=1)
    sb_scale = SB_DIM ** -0.5 * LOG2E
    win = jnp.concatenate([cols["cq"], cols["ckv"], kr_plain, kr_swap, cols["ssm"],
                           _pad_heads(cols["qsb"] * sb_scale, SB_HEADS, SB_DIM),
                           _pad_heads(cols["ksb"], SB_HEADS, SB_DIM), cols["vsb"]], axis=1).astype(BF16)

    qk_dim = MLA_NOPE + MLA_ROPE
    wq = w_uq * (g_cq[:, None] * (qk_dim ** -0.5 * LOG2E))
    wuq = jnp.concatenate([_pad_heads(wq, MLA_HEADS, qk_dim),
                           _pad_heads(_rotate_half_cols(wq, MLA_HEADS, qk_dim, MLA_NOPE), MLA_HEADS, qk_dim)],
                          axis=1).astype(BF16)
    wkv = (w_ukv * g_ckv[:, None]).reshape(MLA_KV_RANK, MLA_HEADS, MLA_NOPE + MLA_V)
    wukv = jnp.concatenate([_pad_heads(wkv[..., :MLA_NOPE].reshape(MLA_KV_RANK, -1), MLA_HEADS, MLA_NOPE),
                            _pad_heads(wkv[..., MLA_NOPE:].reshape(MLA_KV_RANK, -1), MLA_HEADS, MLA_V)],
                           axis=1).astype(BF16)
    return win, wuq, wukv


def _pad_mla_rows(w):
    cols = w.shape[1]
    head_rows = jnp.pad(w[:MLA_V_W].reshape(MLA_HEADS, MLA_V, cols), ((0, 0), (0, HEAD_PAD - MLA_V), (0, 0)))
    return jnp.concatenate([head_rows.reshape(MLA_QK_W, cols), w[MLA_V_W:]], axis=0)


def _pack_ssm(lam_re, lam_im, log_step, b_re, b_im, c_re, c_im, d_skip):
    step = jnp.exp(log_step)[:, None]
    decay = jnp.exp(lam_re * step)
    ab_re, ab_im = decay * jnp.cos(lam_im * step), decay * jnp.sin(lam_im * step)
    inv = 1.0 / (lam_re * lam_re + lam_im * lam_im)
    f_re = ((ab_re - 1.0) * lam_re + ab_im * lam_im) * inv
    f_im = (ab_im * lam_re - (ab_re - 1.0) * lam_im) * inv
    bb_re = f_re[..., None] * b_re - f_im[..., None] * b_im
    bb_im = f_re[..., None] * b_im + f_im[..., None] * b_re
    per = SSM_GROUPS // SSM_BLOCKS
    eye = jnp.eye(per, dtype=F32)
    split = lambda w: w.reshape((SSM_BLOCKS, per) + w.shape[1:])
    blk = lambda w: jnp.einsum("bgph,gk->bghkp", split(w), eye).reshape(SSM_BLOCKS, SSM_BLOCK_CH, SSM_BLOCK_STATES)
    bbd = jnp.concatenate([blk(bb_re), blk(bb_im)], axis=2).astype(BF16)
    blk_c = lambda w: jnp.einsum("bghp,gk->bgpkh", split(w), eye).reshape(SSM_BLOCKS, SSM_BLOCK_STATES, SSM_BLOCK_CH)
    cbd = jnp.concatenate([blk_c(c_re), blk_c(-c_im)], axis=1).astype(BF16)
    lam = jnp.stack([ab_re.reshape(-1), ab_im.reshape(-1)])
    return bbd, lam, cbd, d_skip.reshape(1, SSM_WIDTH)


def kernel(x, mem, positions, w_in, g_cq, g_ckv, w_uq, w_ukv, ssm_lam_re, ssm_lam_im, ssm_log_step, ssm_b_re, ssm_b_im, ssm_c_re, ssm_c_im, ssm_d, w_glu, b_glu, g_mix, w_out, ln_mix_g, ln_mix_b, w_mq, w_mkv, w_mo, ln_mem_g, ln_mem_b, w_pq, peer_sub_keys, peer_u, peer_v, ln_ffn_g, ln_ffn_b):
    bsz, seq, _ = x.shape
    depth = w_in.shape[0]
    alpha = (2 * depth) ** 0.25
    cos_t, sin_t = _rope_tables(positions)
    for l in range(depth):
        win, wuq, wukv = _pack_inproj(w_in[l], g_cq[l], g_ckv[l], w_uq[l], w_ukv[l])
        q, k, v, u_tm, q_sb, k_sb, v_sb = _inproj(x, win, wuq, wukv, cos_t, sin_t)
        y_mla = _mla_attention(q, k, v)
        y_sb = _sb_attention(q_sb, k_sb, v_sb)
        bbd, lam, cbd, d_row = _pack_ssm(ssm_lam_re[l], ssm_lam_im[l], ssm_log_step[l], ssm_b_re[l], ssm_b_im[l],
                                         ssm_c_re[l], ssm_c_im[l], ssm_d[l])
        y_ssm = _ssm(u_tm.reshape(seq * bsz, SSM_WIDTH), bbd, lam, cbd, d_row,
                     w_glu[l].astype(BF16), b_glu[l].reshape(1, SSM_WIDTH), bsz)
        mk, mv = _memkv(mem, w_mkv[l].astype(BF16))
        xt = _mix_mem(y_mla, y_ssm.reshape(seq, bsz * SSM_WIDTH), y_sb, x,
                      _pad_mla_rows(g_mix[l][:, None]).reshape(1, -1), _pad_mla_rows(w_out[l]).astype(BF16),
                      jnp.stack([ln_mix_g[l], ln_mix_b[l]]),
                      mk, mv, (w_mq[l] * MEM_HEAD_DIM ** -0.5).astype(BF16), w_mo[l].astype(BF16),
                      jnp.stack([ln_mem_g[l], ln_mem_b[l]]), alpha)
        a, n, b, r = _peer_route(xt, w_pq[l].T.astype(BF16), peer_sub_keys[l].astype(BF16))
        x = _peer_dense(xt, a, n, b, r, peer_u[l].astype(BF16), peer_v[l].T.astype(BF16),
                        jnp.stack([ln_ffn_g[l], ln_ffn_b[l]]), alpha).reshape(bsz, seq, D_MODEL)
    return x
```

```python
import functools
import math

import jax
import jax.numpy as jnp
from jax import lax
from jax.experimental import pallas as pl
from jax.experimental.pallas import tpu as pltpu

F32 = jnp.float32
BF16 = jnp.bfloat16

D_MODEL = 1024
CHUNK = 64
NORM_EPS = 1e-5
NEG_BIG = -1e30
LOG2E = math.log2(math.e)

MLA_HEADS = 6
MLA_NOPE = 64
MLA_ROPE = 32
MLA_V = 64
MLA_Q_RANK = 256
MLA_KV_RANK = 128
ROPE_THETA = 10000.0

SSM_GROUPS = 24
SSM_CH = 16
SSM_STATE = 64
SSM_WIDTH = SSM_GROUPS * SSM_CH
SSM_STATES = SSM_GROUPS * SSM_STATE
SSM_BLOCKS = 3
SSM_BLOCK_CH = SSM_WIDTH // SSM_BLOCKS
SSM_BLOCK_STATES = SSM_STATES // SSM_BLOCKS

SB_HEADS = 4
SB_DIM = 64
SB_WIDTH = SB_HEADS * SB_DIM

MEM_HEADS = 4
MEM_HEAD_DIM = D_MODEL // MEM_HEADS

PEER_HEADS = 8
PEER_N_KEYS = 128
PEER_TOPK = 16
PEER_KEY_DIM = 128
PEER_EXPERTS = PEER_N_KEYS * PEER_N_KEYS

LANE = 128
BF16_SUBLANES = 16
HEAD_PAD = 128
ATTN_TQ = 512
ATTN_TK = 256

VMEM_LIMIT = 56 * 1024 * 1024

_C_CQ = 0
_C_CKV = _C_CQ + MLA_Q_RANK
_C_KRP = _C_CKV + MLA_KV_RANK
_C_KRS = _C_KRP + HEAD_PAD
_C_SSM = _C_KRS + HEAD_PAD
_C_QSB = _C_SSM + SSM_WIDTH
_C_KSB = _C_QSB + SB_HEADS * HEAD_PAD
_C_VSB = _C_KSB + SB_HEADS * HEAD_PAD
_C_END = _C_VSB + SB_WIDTH
MLA_QK_W = MLA_HEADS * HEAD_PAD
MLA_V_W = MLA_HEADS * MLA_V


def _params(*sem):
    return pltpu.CompilerParams(dimension_semantics=sem, vmem_limit_bytes=VMEM_LIMIT)


def _dot(a, b):
    return jnp.dot(a, b, preferred_element_type=F32)


def _dot_nt(a, b):
    return lax.dot_general(a, b, (((1,), (1,)), ((), ())), preferred_element_type=F32)


def _gelu_tanh(x):
    c = math.sqrt(2.0 / math.pi)
    inner = x * (c + (c * 0.044715) * (x * x))
    return (0.5 * x) * (1.0 + jnp.tanh(inner))


def _layer_norm_rows(v, g, b):
    mu = jnp.mean(v, axis=-1, keepdims=True)
    c = v - mu
    var = jnp.mean(c * c, axis=-1, keepdims=True)
    return c * lax.rsqrt(var + NORM_EPS) * g + b


def _rms_rows(v, width):
    return v * lax.rsqrt(jnp.sum(v * v, axis=-1, keepdims=True) * (1.0 / width) + NORM_EPS)


def _rope_kernel(pos_ref, freq_ref, cos_ref, sin_ref):
    ang = pos_ref[0].astype(F32) * freq_ref[...]
    lane = lax.broadcasted_iota(jnp.int32, ang.shape, 1)
    rot = (lane >= MLA_NOPE) & (lane < MLA_NOPE + MLA_ROPE)
    cos_ref[0] = jnp.where(rot, jnp.cos(ang), jnp.where(lane < MLA_NOPE, 1.0, 0.0))
    sin_ref[0] = jnp.where(rot, jnp.sin(ang), 0.0)


def _rope_tables(positions):
    bsz, seq = positions.shape
    ts = min(seq, 512)
    half = MLA_ROPE // 2
    freq = ROPE_THETA ** (-jnp.arange(half, dtype=F32) / half)
    freq_row = jnp.zeros((1, HEAD_PAD), F32).at[0, MLA_NOPE:MLA_NOPE + MLA_ROPE].set(jnp.tile(freq, 2))
    return pl.pallas_call(
        _rope_kernel,
        grid=(bsz, seq // ts),
        in_specs=[pl.BlockSpec((1, ts, 1), lambda b, s: (b, s, 0)),
                  pl.BlockSpec((1, HEAD_PAD), lambda b, s: (0, 0))],
        out_specs=[pl.BlockSpec((1, ts, HEAD_PAD), lambda b, s: (b, s, 0))] * 2,
        out_shape=[jax.ShapeDtypeStruct((bsz, seq, HEAD_PAD), F32)] * 2,
        compiler_params=_params("parallel", "parallel"),
        name="rope_tables",
    )(positions.reshape(bsz, seq, 1), freq_row)


def _inproj_kernel(x_ref, win_ref, wuq_ref, wukv_ref, cos_ref, sin_ref, ones_ref,
                   q_ref, k_ref, v_ref, u_ref, qs_ref, ks_ref, vs_ref):
    xb = x_ref[0].astype(BF16)
    acc = _dot(xb, win_ref[...])
    cos = cos_ref[0]
    sin = sin_ref[0]
    cqn = _rms_rows(acc[:, _C_CQ:_C_CKV], MLA_Q_RANK).astype(BF16)
    qq = _dot(cqn, wuq_ref[...])
    ckvn = _rms_rows(acc[:, _C_CKV:_C_KRP], MLA_KV_RANK).astype(BF16)
    kv = _dot(ckvn, wukv_ref[...])
    k_rope = acc[:, _C_KRP:_C_KRS] * cos + acc[:, _C_KRS:_C_SSM] * sin
    for h in range(MLA_HEADS):
        lo, hi = h * HEAD_PAD, (h + 1) * HEAD_PAD
        q_ref[0, :, lo:hi] = (qq[:, lo:hi] * cos + qq[:, MLA_QK_W + lo:MLA_QK_W + hi] * sin).astype(BF16)
        k_ref[0, :, lo:hi] = (kv[:, lo:hi] + k_rope).astype(BF16)
    v_ref[0] = (kv[:, MLA_QK_W:] + ones_ref[...]).T.astype(BF16)
    u_ref[...] = acc[:, _C_SSM:_C_QSB].astype(BF16)
    qs_ref[0] = acc[:, _C_QSB:_C_KSB].astype(BF16)
    ks_ref[0] = acc[:, _C_KSB:_C_VSB].astype(BF16)
    vs_ref[0] = acc[:, _C_VSB:_C_END].T.astype(BF16)


def _inproj(x, win, wuq, wukv, cos_t, sin_t):
    bsz, seq, _ = x.shape
    ts = min(seq, 512)
    tok = lambda w: pl.BlockSpec((1, ts, w), lambda b, s: (b, s, 0))
    tok_t = lambda w: pl.BlockSpec((1, w, ts), lambda b, s: (b, 0, s))
    full = lambda a: pl.BlockSpec(a.shape, lambda b, s: (0,) * a.ndim)
    sb_w = SB_HEADS * HEAD_PAD
    lane = jnp.arange(MLA_QK_W) % HEAD_PAD
    ones_row = (lane == MLA_V).astype(F32).reshape(1, MLA_QK_W)
    return pl.pallas_call(
        _inproj_kernel,
        grid=(bsz, seq // ts),
        in_specs=[tok(D_MODEL), full(win), full(wuq), full(wukv), tok(HEAD_PAD), tok(HEAD_PAD), full(ones_row)],
        out_specs=[tok(MLA_QK_W), tok(MLA_QK_W), tok_t(MLA_QK_W),
                   pl.BlockSpec((ts, SSM_WIDTH), lambda b, s: (s, b)),
                   tok(sb_w), tok(sb_w), tok_t(SB_WIDTH)],
        out_shape=[jax.ShapeDtypeStruct((bsz, seq, MLA_QK_W), BF16),
                   jax.ShapeDtypeStruct((bsz, seq, MLA_QK_W), BF16),
                   jax.ShapeDtypeStruct((bsz, MLA_QK_W, seq), BF16),
                   jax.ShapeDtypeStruct((seq, bsz * SSM_WIDTH), BF16),
                   jax.ShapeDtypeStruct((bsz, seq, sb_w), BF16),
                   jax.ShapeDtypeStruct((bsz, seq, sb_w), BF16),
                   jax.ShapeDtypeStruct((bsz, SB_WIDTH, seq), BF16)],
        compiler_params=_params("parallel", "parallel"),
        name="inproj",
    )(x, win, wuq, wukv, cos_t, sin_t, ones_row)


def _mla_kernel(q_ref, k_ref, vt_ref, o_ref, m_ref, acc_ref, s_ref, p_ref, *, tq, tk):
    qi = pl.program_id(1)
    per = tq // tk
    query_chunk = (qi * tq + lax.broadcasted_iota(jnp.int32, (tk, tq), 1)) // CHUNK
    key_in_tile = lax.broadcasted_iota(jnp.int32, (tk, tq), 0)
    m_ref[...] = jnp.full(m_ref.shape, NEG_BIG, F32)
    acc_ref[...] = jnp.zeros(acc_ref.shape, F32)

    def block(kb, masked):
        ks = pl.multiple_of(kb * tk, tk)
        for h in range(MLA_HEADS):
            tile = slice(h * HEAD_PAD, (h + 1) * HEAD_PAD)
            s_ref[h] = _dot_nt(k_ref[0, pl.ds(ks, tk), tile], q_ref[0, :, tile])
        if masked:
            allowed = (ks + key_in_tile) // CHUNK <= query_chunk
        alphas = []
        for h in range(MLA_HEADS):
            s = s_ref[h]
            if masked:
                s = jnp.where(allowed, s, NEG_BIG)
            m_old = m_ref[h]
            m_new = jnp.maximum(m_old, jnp.max(s, axis=0, keepdims=True))
            p_ref[h] = jnp.exp2(s - m_new).astype(BF16)
            alphas.append(jnp.exp2(m_old - m_new))
            m_ref[h] = m_new
        for h in range(MLA_HEADS):
            tile = slice(h * HEAD_PAD, (h + 1) * HEAD_PAD)
            acc_ref[h] = alphas[h] * acc_ref[h] + _dot(vt_ref[0, tile, pl.ds(ks, tk)], p_ref[h])

    def body(kb, carry):
        block(kb, False)
        return carry

    lax.fori_loop(0, qi * per, body, 0)
    for d in range(per):
        block(qi * per + d, True)
    row = lax.broadcasted_iota(jnp.int32, (HEAD_PAD, tq), 0)
    for h in range(MLA_HEADS):
        acc = acc_ref[h]
        out_t = jnp.where(row < MLA_V, acc / acc[MLA_V:MLA_V + 1, :], 0.0)
        o_ref[0, :, h * HEAD_PAD:(h + 1) * HEAD_PAD] = out_t.T.astype(BF16)


def _mla_attention(q, k, vt):
    bsz, seq, _ = q.shape
    tq, tk = min(seq, ATTN_TQ), min(seq, ATTN_TK)
    return pl.pallas_call(
        functools.partial(_mla_kernel, tq=tq, tk=tk),
        grid=(bsz, seq // tq),
        in_specs=[pl.BlockSpec((1, tq, MLA_QK_W), lambda b, i: (b, i, 0)),
                  pl.BlockSpec((1, seq, MLA_QK_W), lambda b, i: (b, 0, 0)),
                  pl.BlockSpec((1, MLA_QK_W, seq), lambda b, i: (b, 0, 0))],
        out_specs=pl.BlockSpec((1, tq, MLA_QK_W), lambda b, i: (b, i, 0)),
        out_shape=jax.ShapeDtypeStruct((bsz, seq, MLA_QK_W), BF16),
        scratch_shapes=[pltpu.VMEM((MLA_HEADS, 1, tq), F32), pltpu.VMEM((MLA_HEADS, HEAD_PAD, tq), F32),
                        pltpu.VMEM((MLA_HEADS, tk, tq), F32), pltpu.VMEM((MLA_HEADS, tk, tq), BF16)],
        compiler_params=_params("parallel", "arbitrary"),
        name="mla_attention",
    )(q, k, vt)


def _sb_kernel(q_ref, k_ref, vt_ref, tri_ref, o_ref, right_ref, acc_ref, z_ref, split_ref, sum_ref, w_ref, *, tq, tk):
    qi = pl.program_id(1)
    per = tq // tk
    query_pos = qi * tq + lax.broadcasted_iota(jnp.int32, (tk, tq), 1)
    key_in_tile = lax.broadcasted_iota(jnp.int32, (tk, tq), 0)
    right_ref[...] = jnp.zeros(right_ref.shape, F32)
    acc_ref[...] = jnp.zeros(acc_ref.shape, F32)

    def block(kb, masked):
        ks = pl.multiple_of(kb * tk, tk)
        for h in range(SB_HEADS):
            tile = slice(h * HEAD_PAD, (h + 1) * HEAD_PAD)
            z_ref[h] = _dot_nt(k_ref[0, pl.ds(ks, tk), tile], q_ref[0, :, tile])
        if masked:
            earlier = ks + key_in_tile < query_pos
        for h in range(SB_HEADS):
            z = z_ref[h]
            neg_abs = pltpu.bitcast(pltpu.bitcast(z, jnp.uint32) | jnp.uint32(0x80000000), F32)
            fail = jnp.maximum(z, 0.0) + jnp.log2(1.0 + jnp.exp2(neg_abs))
            z_ref[h] = z - fail
            if masked:
                fail = jnp.where(earlier, fail, 0.0)
            hi = fail.astype(BF16)
            split_ref[h, 0:tk, :] = hi
            split_ref[h, tk:, :] = (fail - hi.astype(F32)).astype(BF16)
            sum_ref[h] = jnp.sum(fail, axis=0, keepdims=True)
        for h in range(SB_HEADS):
            between = _dot(tri_ref[...], split_ref[h]) + right_ref[h]
            w = jnp.exp2(z_ref[h] - between)
            if masked:
                w = jnp.where(earlier, w, 0.0)
            w_ref[h] = w.astype(BF16)
            right_ref[h] += sum_ref[h]
        for h in range(SB_HEADS):
            pair = slice((h // 2) * HEAD_PAD, (h // 2 + 1) * HEAD_PAD)
            acc_ref[h] += _dot(vt_ref[0, pair, pl.ds(ks, tk)], w_ref[h])

    def body(i, carry):
        block(qi * per - 1 - i, False)
        return carry

    for d in range(per - 1, -1, -1):
        block(qi * per + d, True)
    lax.fori_loop(0, qi * per, body, 0)
    row = lax.broadcasted_iota(jnp.int32, (HEAD_PAD, tq), 0)
    for pair in range(SB_HEADS // 2):
        out_t = jnp.where(row < SB_DIM, acc_ref[2 * pair], acc_ref[2 * pair + 1])
        o_ref[0, :, pair * HEAD_PAD:(pair + 1) * HEAD_PAD] = out_t.T.astype(BF16)


def _sb_attention(q, k, vt):
    bsz, seq, _ = q.shape
    tq, tk = min(seq, ATTN_TQ), min(seq, ATTN_TK)
    sb_w = SB_HEADS * HEAD_PAD
    tri = (lax.broadcasted_iota(jnp.int32, (tk, tk), 1) > lax.broadcasted_iota(jnp.int32, (tk, tk), 0)).astype(BF16)
    tri2 = jnp.concatenate([tri, tri], axis=1)
    return pl.pallas_call(
        functools.partial(_sb_kernel, tq=tq, tk=tk),
        grid=(bsz, seq // tq),
        in_specs=[pl.BlockSpec((1, tq, sb_w), lambda b, i: (b, i, 0)),
                  pl.BlockSpec((1, seq, sb_w), lambda b, i: (b, 0, 0)),
                  pl.BlockSpec((1, SB_WIDTH, seq), lambda b, i: (b, 0, 0)),
                  pl.BlockSpec((tk, 2 * tk), lambda b, i: (0, 0))],
        out_specs=pl.BlockSpec((1, tq, SB_WIDTH), lambda b, i: (b, i, 0)),
        out_shape=jax.ShapeDtypeStruct((bsz, seq, SB_WIDTH), BF16),
        scratch_shapes=[pltpu.VMEM((SB_HEADS, 1, tq), F32), pltpu.VMEM((SB_HEADS, HEAD_PAD, tq), F32),
                        pltpu.VMEM((SB_HEADS, tk, tq), F32), pltpu.VMEM((SB_HEADS, 2 * tk, tq), BF16),
                        pltpu.VMEM((SB_HEADS, 1, tq), F32), pltpu.VMEM((SB_HEADS, tk, tq), BF16)],
        compiler_params=_params("parallel", "arbitrary"),
        name="sb_attention",
    )(q, k, vt, tri2)


def _ssm_kernel(u_ref, bbd_ref, lam_ref, cbd_ref, d_ref, wglu_ref, bglu_ref, y_ref, h_ref, hs_ref, *, tt, nb):
    @pl.when(pl.program_id(0) == 0)
    def _():
        h_ref[...] = jnp.zeros_like(h_ref)

    u = u_ref[...]
    half, wide = SSM_BLOCK_STATES, 2 * SSM_BLOCK_STATES
    for b in range(SSM_BLOCKS):
        hs_ref[:, b * wide:(b + 1) * wide] = _dot(u[:, b * SSM_BLOCK_CH:(b + 1) * SSM_BLOCK_CH], bbd_ref[b])
    lam_re = jnp.broadcast_to(lam_ref[0:1, :], (nb, SSM_STATES))
    lam_im = jnp.broadcast_to(lam_ref[1:2, :], (nb, SSM_STATES))

    def step(t, carry):
        r0 = pl.multiple_of(t * nb, nb)
        out = []
        for b in range(SSM_BLOCKS):
            h_re, h_im = carry[b]
            l_re, l_im = lam_re[:, b * half:(b + 1) * half], lam_im[:, b * half:(b + 1) * half]
            re_cols = slice(b * wide, b * wide + half)
            im_cols = slice(b * wide + half, (b + 1) * wide)
            n_re = l_re * h_re - l_im * h_im + hs_ref[pl.ds(r0, nb), re_cols]
            n_im = l_re * h_im + l_im * h_re + hs_ref[pl.ds(r0, nb), im_cols]
            hs_ref[pl.ds(r0, nb), re_cols] = n_re
            hs_ref[pl.ds(r0, nb), im_cols] = n_im
            out.append((n_re, n_im))
        return tuple(out)

    init = tuple((h_ref[:, b * wide:b * wide + half], h_ref[:, b * wide + half:(b + 1) * wide]) for b in range(SSM_BLOCKS))
    final = lax.fori_loop(0, tt, step, init)
    for b in range(SSM_BLOCKS):
        h_ref[:, b * wide:b * wide + half] = final[b][0]
        h_ref[:, b * wide + half:(b + 1) * wide] = final[b][1]

    y = jnp.concatenate([_dot(hs_ref[:, b * wide:(b + 1) * wide].astype(BF16), cbd_ref[b]) for b in range(SSM_BLOCKS)],
                        axis=1) + d_ref[...] * u.astype(F32)
    y = jax.nn.gelu(y)
    y = y * jax.nn.sigmoid(_dot(y.astype(BF16), wglu_ref[...]) + bglu_ref[...])
    y_ref[...] = y.astype(BF16)


def _ssm(u_tm, bbd, lam, cbd, d_row, wglu, bglu_row, nb):
    rows = u_tm.shape[0]
    seq = rows // nb
    tt = min(seq, 128)
    full = lambda a: pl.BlockSpec(a.shape, lambda i: (0,) * a.ndim)
    return pl.pallas_call(
        functools.partial(_ssm_kernel, tt=tt, nb=nb),
        grid=(seq // tt,),
        in_specs=[pl.BlockSpec((tt * nb, SSM_WIDTH), lambda i: (i, 0)),
                  full(bbd), full(lam), full(cbd), full(d_row), full(wglu), full(bglu_row)],
        out_specs=pl.BlockSpec((tt * nb, SSM_WIDTH), lambda i: (i, 0)),
        out_shape=jax.ShapeDtypeStruct((rows, SSM_WIDTH), BF16),
        scratch_shapes=[pltpu.VMEM((nb, 2 * SSM_STATES), F32),
                        pltpu.VMEM((tt * nb, 2 * SSM_STATES), F32)],
        compiler_params=_params("arbitrary"),
        name="ssm",
    )(u_tm, bbd, lam, cbd, d_row, wglu, bglu_row)


def _memkv_kernel(mem_ref, w_ref, k_ref, v_ref):
    kv = _dot(mem_ref[0].astype(BF16), w_ref[...])
    k_ref[0] = kv[:, :D_MODEL].astype(BF16)
    v_ref[0] = kv[:, D_MODEL:].astype(BF16)


def _memkv(mem, w_mkv):
    bsz, mlen, _ = mem.shape
    return pl.pallas_call(
        _memkv_kernel,
        grid=(bsz,),
        in_specs=[pl.BlockSpec((1, mlen, D_MODEL), lambda b: (b, 0, 0)),
                  pl.BlockSpec(w_mkv.shape, lambda b: (0, 0))],
        out_specs=[pl.BlockSpec((1, mlen, D_MODEL), lambda b: (b, 0, 0))] * 2,
        out_shape=[jax.ShapeDtypeStruct((bsz, mlen, D_MODEL), BF16)] * 2,
        compiler_params=_params("parallel"),
        name="mem_kv",
    )(mem, w_mkv)


def _mix_mem_kernel(ya_ref, ys_ref, yb_ref, x_ref, gmix_ref, wout_ref, ln1_ref,
                    mk_ref, mv_ref, wmq_ref, wmo_ref, ln2_ref, xt_ref,
                    x1_ref, xq_ref, s_ref, p_ref, o_ref, *, alpha, halves):
    gm = gmix_ref[...]
    c1, c2 = MLA_QK_W, MLA_QK_W + SSM_WIDTH
    rows = x_ref.shape[1] // halves
    parts = [slice(i * rows, (i + 1) * rows) for i in range(halves)]
    for r in parts:
        ya = (_rms_rows(ya_ref[0, r, :].astype(F32), MLA_V_W) * gm[:, 0:c1]).astype(BF16)
        ys = (_rms_rows(ys_ref[r, :].astype(F32), SSM_WIDTH) * gm[:, c1:c2]).astype(BF16)
        yb = (_rms_rows(yb_ref[0, r, :].astype(F32), SB_WIDTH) * gm[:, c2:]).astype(BF16)
        x1_ref[r, :] = _dot(ya, wout_ref[0:c1, :]) + _dot(ys, wout_ref[c1:c2, :]) + _dot(yb, wout_ref[c2:, :])
    for r in parts:
        x1 = _layer_norm_rows(alpha * x_ref[0, r, :] + x1_ref[r, :], ln1_ref[0:1, :], ln1_ref[1:2, :])
        x1_ref[r, :] = x1
        xq_ref[r, :] = x1.astype(BF16)
    for r in parts:
        xq_ref[r, :] = _dot(xq_ref[r, :], wmq_ref[...]).astype(BF16)
    for i, r in enumerate(parts):
        for h in range(MEM_HEADS):
            cols = slice(h * MEM_HEAD_DIM, (h + 1) * MEM_HEAD_DIM)
            s_ref[i * MEM_HEADS + h] = _dot_nt(xq_ref[r, cols], mk_ref[0, :, cols])
    for i in range(halves * MEM_HEADS):
        s = s_ref[i]
        p = jnp.exp(s - jnp.max(s, axis=-1, keepdims=True))
        p_ref[i] = (p / jnp.sum(p, axis=-1, keepdims=True)).astype(BF16)
    for i, r in enumerate(parts):
        for h in range(MEM_HEADS):
            cols = slice(h * MEM_HEAD_DIM, (h + 1) * MEM_HEAD_DIM)
            o_ref[r, cols] = _dot(p_ref[i * MEM_HEADS + h], mv_ref[0, :, cols]).astype(BF16)
    for r in parts:
        x2 = _layer_norm_rows(alpha * x1_ref[r, :] + _dot(o_ref[r, :], wmo_ref[...]), ln2_ref[0:1, :], ln2_ref[1:2, :])
        xt_ref[:, r] = x2.T


def _mix_mem(ya, ys_tm, yb, x, gmix, wout, ln1, mk, mv, wmq, wmo, ln2, alpha):
    bsz, seq, _ = x.shape
    ts = min(seq, 512)
    nst = seq // ts
    halves = 2
    mlen = mk.shape[1]
    tok = lambda w: pl.BlockSpec((1, ts, w), lambda b, s: (b, s, 0))
    full = lambda a: pl.BlockSpec(a.shape, lambda b, s: (0,) * a.ndim)
    return pl.pallas_call(
        functools.partial(_mix_mem_kernel, alpha=alpha, halves=halves),
        grid=(bsz, nst),
        in_specs=[tok(MLA_QK_W), pl.BlockSpec((ts, SSM_WIDTH), lambda b, s: (s, b)), tok(SB_WIDTH), tok(D_MODEL),
                  full(gmix), full(wout), full(ln1),
                  pl.BlockSpec((1, mlen, D_MODEL), lambda b, s: (b, 0, 0)),
                  pl.BlockSpec((1, mlen, D_MODEL), lambda b, s: (b, 0, 0)),
                  full(wmq), full(wmo), full(ln2)],
        out_specs=pl.BlockSpec((D_MODEL, ts), lambda b, s: (0, b * nst + s)),
        out_shape=jax.ShapeDtypeStruct((D_MODEL, bsz * seq), F32),
        scratch_shapes=[pltpu.VMEM((ts, D_MODEL), F32), pltpu.VMEM((ts, D_MODEL), BF16),
                        pltpu.VMEM((halves * MEM_HEADS, ts // halves, mlen), F32),
                        pltpu.VMEM((halves * MEM_HEADS, ts // halves, mlen), BF16),
                        pltpu.VMEM((ts, D_MODEL), BF16)],
        compiler_params=_params("parallel", "parallel"),
        name="mix_mem",
    )(ya, ys_tm, yb, x, gmix, wout, ln1, mk, mv, wmq, wmo, ln2)


_SORT16 = ((0, 1), (2, 3), (0, 2), (1, 3), (1, 2), (4, 5), (6, 7), (4, 6), (5, 7), (5, 6), (0, 4), (2, 6), (2, 4),
           (1, 5), (3, 7), (3, 5), (1, 2), (3, 4), (5, 6), (8, 9), (10, 11), (8, 10), (9, 11), (9, 10), (12, 13),
           (14, 15), (12, 14), (13, 15), (13, 14), (8, 12), (10, 14), (10, 12), (9, 13), (11, 15), (11, 13), (9, 10),
           (11, 12), (13, 14), (0, 8), (4, 12), (4, 8), (2, 10), (6, 14), (6, 10), (2, 4), (6, 8), (10, 12), (1, 9),
           (5, 13), (5, 9), (3, 11), (7, 15), (7, 11), (3, 5), (7, 9), (11, 13), (1, 2), (3, 4), (5, 6), (7, 8),
           (9, 10), (11, 12), (13, 14))


def _sort16_desc(v):
    v = list(v)
    for i, j in _SORT16:
        v[i], v[j] = jnp.maximum(v[i], v[j]), jnp.minimum(v[i], v[j])
    return v


def _merge_bitonic_desc(v):
    v = list(v)
    for j in (8, 4, 2, 1):
        for i in range(16):
            l = i ^ j
            if l > i:
                v[i], v[l] = jnp.maximum(v[i], v[l]), jnp.minimum(v[i], v[l])
    return v


def _top16_sorted(rows, sort_result=True):
    v = _sort16_desc(rows)
    for shift in (4, 2, 1):
        partner = [pltpu.roll(x, shift, axis=0) for x in v]
        v = [jnp.maximum(v[i], partner[15 - i]) for i in range(16)]
        if sort_result or shift != 1:
            v = _merge_bitonic_desc(v)
    return v


def _count_prefix(pred, v):
    p8 = pred(v[7])
    p4 = pred(jnp.where(p8, v[11], v[3]))
    p2 = pred(jnp.where(p8, jnp.where(p4, v[13], v[9]), jnp.where(p4, v[5], v[1])))
    p1 = pred(jnp.where(p8, jnp.where(p4, jnp.where(p2, v[14], v[12]), jnp.where(p2, v[10], v[8])),
                        jnp.where(p4, jnp.where(p2, v[6], v[4]), jnp.where(p2, v[2], v[0]))))
    count = (jnp.where(p8, 8.0, 0.0) + jnp.where(p4, 4.0, 0.0)) + (jnp.where(p2, 2.0, 0.0) + jnp.where(p1, 1.0, 0.0))
    return jnp.where(pred(v[15]), 16.0, count)


def _peer_route_kernel(xt_ref, wpq_ref, keys_ref, a_ref, n_ref, b_ref, r_ref, q_ref, s1_ref, s2_ref):
    q_ref[...] = _dot(wpq_ref[...], xt_ref[...].astype(BF16)).astype(BF16)
    lax.fori_loop(0, PEER_HEADS, functools.partial(_peer_route_head, q_ref, keys_ref, a_ref, n_ref, b_ref, r_ref,
                                                   s1_ref, s2_ref), 0)


def _peer_route_head(q_ref, keys_ref, a_ref, n_ref, b_ref, r_ref, s1_ref, s2_ref, h, carry):
    q0 = pl.multiple_of(h * (2 * PEER_KEY_DIM), 2 * PEER_KEY_DIM)
    s1_ref[...] = _dot(keys_ref[h, 0], q_ref[pl.ds(q0, PEER_KEY_DIM), :])
    s2_ref[...] = _dot(keys_ref[h, 1], q_ref[pl.ds(q0 + PEER_KEY_DIM, PEER_KEY_DIM), :])
    sub = lax.broadcasted_iota(jnp.int32, (8, LANE), 0)
    groups = PEER_N_KEYS // 8

    def spread(vals):
        out = vals[7]
        for j in range(6, -1, -1):
            out = jnp.where(sub == j, vals[j], out)
        return out

    def chunk(c, carry):
        ln = pl.ds(pl.multiple_of(c * LANE, LANE), LANE)
        rows1 = [s1_ref[8 * i:8 * i + 8, ln] for i in range(groups)]
        rows2 = [s2_ref[8 * i:8 * i + 8, ln] for i in range(groups)]
        v1 = _top16_sorted(rows1)
        v2 = _top16_sorted(rows2)
        v2lo, v2hi, v1hi = spread(v2[:8]), spread(v2[8:]), spread(v1[8:])
        cands = ([v1[0] + v2lo, v1[0] + v2hi] + [v1[r] + v2lo for r in range(1, 8)] + [v1hi + v2[0]]
                 + [v1[r] + v2hi for r in range(1, 7)])
        top = _top16_sorted(cands, sort_result=False)
        thr = functools.reduce(jnp.minimum, top)
        cmax = v1[0] + v2[0]
        inv_z = 1.0 / functools.reduce(jnp.add, [jnp.exp(t - cmax) for t in top])
        for i in range(groups):
            s = rows1[i]
            kept = _count_prefix(lambda t, s=s: s + t >= thr, v2)
            in_top = s >= v1[15]
            a_ref[i, h, :, ln] = jnp.where(in_top, jnp.exp(s - v1[0]), 0.0)
            n_ref[i, h, :, ln] = jnp.where(in_top, kept, 0.0)
        for i in range(groups // 2):
            bs, rs = [], []
            for s in (rows2[2 * i], rows2[2 * i + 1]):
                rank = _count_prefix(lambda t, s=s: t > s, v2)
                rs.append(rank)
                bs.append(jnp.where(rank < float(PEER_TOPK), jnp.exp(s - v2[0]) * inv_z, 0.0))
            b_ref[h, 16 * i:16 * i + 16, ln] = jnp.concatenate(bs, axis=0).astype(BF16)
            r_ref[h, 16 * i:16 * i + 16, ln] = jnp.concatenate(rs, axis=0).astype(BF16)
        return carry

    lax.fori_loop(0, s1_ref.shape[1] // LANE, chunk, 0)
    return carry


def _peer_route(xt, wpq_t, keys):
    ntok = xt.shape[1]
    tt = min(ntok, 512)
    nblk = PEER_N_KEYS // 8
    return pl.pallas_call(
        _peer_route_kernel,
        grid=(ntok // tt,),
        in_specs=[pl.BlockSpec((D_MODEL, tt), lambda i: (0, i)),
                  pl.BlockSpec(wpq_t.shape, lambda i: (0, 0)),
                  pl.BlockSpec(keys.shape, lambda i: (0, 0, 0, 0))],
        out_specs=[pl.BlockSpec((nblk, PEER_HEADS, 8, tt), lambda i: (0, 0, 0, i)),
                   pl.BlockSpec((nblk, PEER_HEADS, 8, tt), lambda i: (0, 0, 0, i)),
                   pl.BlockSpec((PEER_HEADS, PEER_N_KEYS, tt), lambda i: (0, 0, i)),
                   pl.BlockSpec((PEER_HEADS, PEER_N_KEYS, tt), lambda i: (0, 0, i))],
        out_shape=[jax.ShapeDtypeStruct((nblk, PEER_HEADS, 8, ntok), F32),
                   jax.ShapeDtypeStruct((nblk, PEER_HEADS, 8, ntok), F32),
                   jax.ShapeDtypeStruct((PEER_HEADS, PEER_N_KEYS, ntok), BF16),
                   jax.ShapeDtypeStruct((PEER_HEADS, PEER_N_KEYS, ntok), BF16)],
        scratch_shapes=[pltpu.VMEM((PEER_HEADS * 2 * PEER_KEY_DIM, tt), BF16),
                        pltpu.VMEM((PEER_N_KEYS, tt), F32), pltpu.VMEM((PEER_N_KEYS, tt), F32)],
        compiler_params=_params("parallel"),
        name="peer_route",
    )(xt, wpq_t, keys)


PEER_I1_PER_TILE = 8
PEER_TILE = PEER_I1_PER_TILE * PEER_N_KEYS


def _peer_dense_kernel(xt_ref, a_ref, n_ref, b_ref, r_ref, u_ref, vt_ref, ln_ref, o_ref,
                       acc_ref, xb_ref, gh_ref, *, alpha):
    e = pl.program_id(1)

    @pl.when(e == 0)
    def _():
        acc_ref[...] = jnp.zeros_like(acc_ref)
        xb_ref[...] = xt_ref[...].astype(BF16)

    tt = xb_ref.shape[1]
    rows = BF16_SUBLANES
    gh_ref[...] = _dot(u_ref[...], xb_ref[...]).astype(BF16)
    cw = min(tt, 2 * LANE)
    for j in range(PEER_I1_PER_TILE):
        for c in range(tt // cw):
            ln = slice(c * cw, (c + 1) * cw)
            a_rows = [jnp.broadcast_to(a_ref[0, h, j:j + 1, ln], (rows, cw)).astype(BF16) for h in range(PEER_HEADS)]
            n_rows = [jnp.broadcast_to(n_ref[0, h, j:j + 1, ln], (rows, cw)).astype(BF16) for h in range(PEER_HEADS)]
            for g in range(PEER_N_KEYS // rows):
                i2 = slice(g * rows, (g + 1) * rows)
                gate = None
                for h in range(PEER_HEADS):
                    term = jnp.where(r_ref[h, i2, ln] < n_rows[h], b_ref[h, i2, ln] * a_rows[h], jnp.zeros((), BF16))
                    gate = term if gate is None else gate + term
                e0 = j * PEER_N_KEYS + g * rows
                gh_ref[e0:e0 + rows, ln] = gate * _gelu_tanh(gh_ref[e0:e0 + rows, ln])
    acc_ref[...] += _dot(vt_ref[...], gh_ref[...])

    @pl.when(e == pl.num_programs(1) - 1)
    def _():
        y = alpha * xt_ref[...] + acc_ref[...]
        mu = jnp.mean(y, axis=0, keepdims=True)
        c = y - mu
        var = jnp.mean(c * c, axis=0, keepdims=True)
        o_ref[...] = (c * lax.rsqrt(var + NORM_EPS)).T * ln_ref[0:1, :] + ln_ref[1:2, :]


def _peer_dense(xt, a, n, b, r, u, vt, ln, alpha):
    ntok = xt.shape[1]
    tt = min(ntok, 1024)
    ntile = PEER_EXPERTS // PEER_TILE
    return pl.pallas_call(
        functools.partial(_peer_dense_kernel, alpha=alpha),
        grid=(ntok // tt, ntile),
        in_specs=[pl.BlockSpec((D_MODEL, tt), lambda i, e: (0, i)),
                  pl.BlockSpec((1, PEER_HEADS, PEER_I1_PER_TILE, tt), lambda i, e: (e, 0, 0, i)),
                  pl.BlockSpec((1, PEER_HEADS, PEER_I1_PER_TILE, tt), lambda i, e: (e, 0, 0, i)),
                  pl.BlockSpec((PEER_HEADS, PEER_N_KEYS, tt), lambda i, e: (0, 0, i)),
                  pl.BlockSpec((PEER_HEADS, PEER_N_KEYS, tt), lambda i, e: (0, 0, i)),
                  pl.BlockSpec((PEER_TILE, D_MODEL), lambda i, e: (e, 0)),
                  pl.BlockSpec((D_MODEL, PEER_TILE), lambda i, e: (0, e)),
                  pl.BlockSpec(ln.shape, lambda i, e: (0, 0))],
        out_specs=pl.BlockSpec((tt, D_MODEL), lambda i, e: (i, 0)),
        out_shape=jax.ShapeDtypeStruct((ntok, D_MODEL), F32),
        scratch_shapes=[pltpu.VMEM((D_MODEL, tt), F32), pltpu.VMEM((D_MODEL, tt), BF16),
                        pltpu.VMEM((PEER_TILE, tt), BF16)],
        compiler_params=_params("parallel", "arbitrary"),
        name="peer_dense",
    )(xt, a, n, b, r, u, vt, ln)


def _pad_heads(w, heads, width):
    rows = w.shape[0]
    w = w.reshape(rows, heads, width)
    return jnp.pad(w, ((0, 0), (0, 0), (0, HEAD_PAD - width))).reshape(rows, heads * HEAD_PAD)


def _rotate_half_cols(w, heads, width, nope):
    rows = w.shape[0]
    w = w.reshape(rows, heads, width)
    half = (width - nope) // 2
    x1, x2 = w[..., nope:nope + half], w[..., nope + half:]
    out = jnp.concatenate([jnp.zeros_like(w[..., :nope]), -x2, x1], axis=-1)
    return out.reshape(rows, heads * width)


def _pack_inproj(w_in, g_cq, g_ckv, w_uq, w_ukv):
    c = 0
    cols = {}
    for name, width in (("cq", MLA_Q_RANK), ("ckv", MLA_KV_RANK), ("kr", MLA_ROPE), ("ssm", SSM_WIDTH),
                        ("qsb", SB_WIDTH), ("ksb", SB_WIDTH), ("vsb", SB_WIDTH)):
        cols[name] = w_in[:, c:c + width]
        c += width
    rows = w_in.shape[0]
    zeros = lambda n: jnp.zeros((rows, n), F32)
    kr = cols["kr"]
    half = MLA_ROPE // 2
    kr_plain = jnp.concatenate([zeros(MLA_NOPE), kr, zeros(HEAD_PAD - MLA_NOPE - MLA_ROPE)], axis=1)
    kr_swap = jnp.concatenate([zeros(MLA_NOPE), -kr[:, half:], kr[:, :half],
                               zeros(HEAD_PAD - MLA_NOPE - MLA_ROPE)], axis=1)
    sb_scale = SB_DIM ** -0.5 * LOG2E
    win = jnp.concatenate([cols["cq"], cols["ckv"], kr_plain, kr_swap, cols["ssm"],
                           _pad_heads(cols["qsb"] * sb_scale, SB_HEADS, SB_DIM),
                           _pad_heads(cols["ksb"], SB_HEADS, SB_DIM), cols["vsb"]], axis=1).astype(BF16)

    qk_dim = MLA_NOPE + MLA_ROPE
    wq = w_uq * (g_cq[:, None] * (qk_dim ** -0.5 * LOG2E))
    wuq = jnp.concatenate([_pad_heads(wq, MLA_HEADS, qk_dim),
                           _pad_heads(_rotate_half_cols(wq, MLA_HEADS, qk_dim, MLA_NOPE), MLA_HEADS, qk_dim)],
                          axis=1).astype(BF16)
    wkv = (w_ukv * g_ckv[:, None]).reshape(MLA_KV_RANK, MLA_HEADS, MLA_NOPE + MLA_V)
    wukv = jnp.concatenate([_pad_heads(wkv[..., :MLA_NOPE].reshape(MLA_KV_RANK, -1), MLA_HEADS, MLA_NOPE),
                            _pad_heads(wkv[..., MLA_NOPE:].reshape(MLA_KV_RANK, -1), MLA_HEADS, MLA_V)],
                           axis=1).astype(BF16)
    return win, wuq, wukv


def _pad_mla_rows(w):
    cols = w.shape[1]
    head_rows = jnp.pad(w[:MLA_V_W].reshape(MLA_HEADS, MLA_V, cols), ((0, 0), (0, HEAD_PAD - MLA_V), (0, 0)))
    return jnp.concatenate([head_rows.reshape(MLA_QK_W, cols), w[MLA_V_W:]], axis=0)


def _pack_ssm(lam_re, lam_im, log_step, b_re, b_im, c_re, c_im, d_skip):
    step = jnp.exp(log_step)[:, None]
    decay = jnp.exp(lam_re * step)
    ab_re, ab_im = decay * jnp.cos(lam_im * step), decay * jnp.sin(lam_im * step)
    inv = 1.0 / (lam_re * lam_re + lam_im * lam_im)
    f_re = ((ab_re - 1.0) * lam_re + ab_im * lam_im) * inv
    f_im = (ab_im * lam_re - (ab_re - 1.0) * lam_im) * inv
    bb_re = f_re[..., None] * b_re - f_im[..., None] * b_im
    bb_im = f_re[..., None] * b_im + f_im[..., None] * b_re
    per = SSM_GROUPS // SSM_BLOCKS
    eye = jnp.eye(per, dtype=F32)
    split = lambda w: w.reshape((SSM_BLOCKS, per) + w.shape[1:])
    blk = lambda w: jnp.einsum("bgph,gk->bghkp", split(w), eye).reshape(SSM_BLOCKS, SSM_BLOCK_CH, SSM_BLOCK_STATES)
    bbd = jnp.concatenate([blk(bb_re), blk(bb_im)], axis=2).astype(BF16)
    blk_c = lambda w: jnp.einsum("bghp,gk->bgpkh", split(w), eye).reshape(SSM_BLOCKS, SSM_BLOCK_STATES, SSM_BLOCK_CH)
    cbd = jnp.concatenate([blk_c(c_re), blk_c(-c_im)], axis=1).astype(BF16)
    lam = jnp.stack([ab_re.reshape(-1), ab_im.reshape(-1)])
    return bbd, lam, cbd, d_skip.reshape(1, SSM_WIDTH)


def kernel(x, mem, positions, w_in, g_cq, g_ckv, w_uq, w_ukv, ssm_lam_re, ssm_lam_im, ssm_log_step, ssm_b_re, ssm_b_im, ssm_c_re, ssm_c_im, ssm_d, w_glu, b_glu, g_mix, w_out, ln_mix_g, ln_mix_b, w_mq, w_mkv, w_mo, ln_mem_g, ln_mem_b, w_pq, peer_sub_keys, peer_u, peer_v, ln_ffn_g, ln_ffn_b):
    bsz, seq, _ = x.shape
    depth = w_in.shape[0]
    alpha = (2 * depth) ** 0.25
    cos_t, sin_t = _rope_tables(positions)
    for l in range(depth):
        win, wuq, wukv = _pack_inproj(w_in[l], g_cq[l], g_ckv[l], w_uq[l], w_ukv[l])
        q, k, v, u_tm, q_sb, k_sb, v_sb = _inproj(x, win, wuq, wukv, cos_t, sin_t)
        y_mla = _mla_attention(q, k, v)
        y_sb = _sb_attention(q_sb, k_sb, v_sb)
        bbd, lam, cbd, d_row = _pack_ssm(ssm_lam_re[l], ssm_lam_im[l], ssm_log_step[l], ssm_b_re[l], ssm_b_im[l],
                                         ssm_c_re[l], ssm_c_im[l], ssm_d[l])
        y_ssm = _ssm(u_tm.reshape(seq * bsz, SSM_WIDTH), bbd, lam, cbd, d_row,
                     w_glu[l].astype(BF16), b_glu[l].reshape(1, SSM_WIDTH), bsz)
        mk, mv = _memkv(mem, w_mkv[l].astype(BF16))
        xt = _mix_mem(y_mla, y_ssm.reshape(seq, bsz * SSM_WIDTH), y_sb, x,
                      _pad_mla_rows(g_mix[l][:, None]).reshape(1, -1), _pad_mla_rows(w_out[l]).astype(BF16),
                      jnp.stack([ln_mix_g[l], ln_mix_b[l]]),
                      mk, mv, (w_mq[l] * MEM_HEAD_DIM ** -0.5).astype(BF16), w_mo[l].astype(BF16),
                      jnp.stack([ln_mem_g[l], ln_mem_b[l]]), alpha)
        a, n, b, r = _peer_route(xt, w_pq[l].T.astype(BF16), peer_sub_keys[l].astype(BF16))
        x = _peer_dense(xt, a, n, b, r, peer_u[l].astype(BF16), peer_v[l].T.astype(BF16),
                        jnp.stack([ln_ffn_g[l], ln_ffn_b[l]]), alpha).reshape(bsz, seq, D_MODEL)
    return x
```

```python
import functools
import math

import jax
import jax.numpy as jnp
from jax import lax
from jax.experimental import pallas as pl
from jax.experimental.pallas import tpu as pltpu

F32 = jnp.float32
BF16 = jnp.bfloat16

D_MODEL = 1024
CHUNK = 64
NORM_EPS = 1e-5
NEG_BIG = -1e30
LOG2E = math.log2(math.e)

MLA_HEADS = 6
MLA_NOPE = 64
MLA_ROPE = 32
MLA_V = 64
MLA_Q_RANK = 256
MLA_KV_RANK = 128
ROPE_THETA = 10000.0

SSM_GROUPS = 24
SSM_CH = 16
SSM_STATE = 64
SSM_WIDTH = SSM_GROUPS * SSM_CH
SSM_STATES = SSM_GROUPS * SSM_STATE
SSM_BLOCKS = 3
SSM_BLOCK_CH = SSM_WIDTH // SSM_BLOCKS
SSM_BLOCK_STATES = SSM_STATES // SSM_BLOCKS

SB_HEADS = 4
SB_DIM = 64
SB_WIDTH = SB_HEADS * SB_DIM

MEM_HEADS = 4
MEM_HEAD_DIM = D_MODEL // MEM_HEADS

PEER_HEADS = 8
PEER_N_KEYS = 128
PEER_TOPK = 16
PEER_KEY_DIM = 128
PEER_EXPERTS = PEER_N_KEYS * PEER_N_KEYS

LANE = 128
BF16_SUBLANES = 16
HEAD_PAD = 128
ATTN_TQ = 512
ATTN_TK = 256
SB_SCAN = 128

VMEM_LIMIT = 56 * 1024 * 1024

_C_CQ = 0
_C_CKV = _C_CQ + MLA_Q_RANK
_C_KRP = _C_CKV + MLA_KV_RANK
_C_KRS = _C_KRP + HEAD_PAD
_C_SSM = _C_KRS + HEAD_PAD
_C_QSB = _C_SSM + SSM_WIDTH
_C_KSB = _C_QSB + SB_HEADS * HEAD_PAD
_C_VSB = _C_KSB + SB_HEADS * HEAD_PAD
_C_END = _C_VSB + SB_WIDTH
MLA_QK_W = MLA_HEADS * HEAD_PAD
MLA_V_W = MLA_HEADS * MLA_V
MLA_V_ROWS = MLA_V + BF16_SUBLANES


def _params(*sem):
    return pltpu.CompilerParams(dimension_semantics=sem, vmem_limit_bytes=VMEM_LIMIT)


def _dot(a, b):
    return jnp.dot(a, b, preferred_element_type=F32)


def _dot_nt(a, b):
    return lax.dot_general(a, b, (((1,), (1,)), ((), ())), preferred_element_type=F32)


def _gelu_tanh(x):
    c = math.sqrt(2.0 / math.pi)
    inner = x * (c + (c * 0.044715) * (x * x))
    return (0.5 * x) * (1.0 + jnp.tanh(inner))


def _layer_norm_rows(v, g, b):
    mu = jnp.mean(v, axis=-1, keepdims=True)
    c = v - mu
    var = jnp.mean(c * c, axis=-1, keepdims=True)
    return c * lax.rsqrt(var + NORM_EPS) * g + b


def _rms_rows(v, width):
    return v * lax.rsqrt(jnp.sum(v * v, axis=-1, keepdims=True) * (1.0 / width) + NORM_EPS)


def _rope_kernel(pos_ref, freq_ref, cos_ref, sin_ref):
    ang = pos_ref[0].astype(F32) * freq_ref[...]
    lane = lax.broadcasted_iota(jnp.int32, ang.shape, 1)
    rot = (lane >= MLA_NOPE) & (lane < MLA_NOPE + MLA_ROPE)
    cos_ref[0] = jnp.where(rot, jnp.cos(ang), jnp.where(lane < MLA_NOPE, 1.0, 0.0))
    sin_ref[0] = jnp.where(rot, jnp.sin(ang), 0.0)


def _rope_tables(positions):
    bsz, seq = positions.shape
    ts = min(seq, 512)
    half = MLA_ROPE // 2
    freq = ROPE_THETA ** (-jnp.arange(half, dtype=F32) / half)
    freq_row = jnp.zeros((1, HEAD_PAD), F32).at[0, MLA_NOPE:MLA_NOPE + MLA_ROPE].set(jnp.tile(freq, 2))
    return pl.pallas_call(
        _rope_kernel,
        grid=(bsz, seq // ts),
        in_specs=[pl.BlockSpec((1, ts, 1), lambda b, s: (b, s, 0)),
                  pl.BlockSpec((1, HEAD_PAD), lambda b, s: (0, 0))],
        out_specs=[pl.BlockSpec((1, ts, HEAD_PAD), lambda b, s: (b, s, 0))] * 2,
        out_shape=[jax.ShapeDtypeStruct((bsz, seq, HEAD_PAD), F32)] * 2,
        compiler_params=_params("parallel", "parallel"),
        name="rope_tables",
    )(positions.reshape(bsz, seq, 1), freq_row)


def _inproj_kernel(x_ref, win_ref, wuq_ref, wukv_ref, cos_ref, sin_ref, ones_ref,
                   q_ref, k_ref, v_ref, u_ref, qs_ref, ks_ref, vs_ref):
    xb = x_ref[0].astype(BF16)
    acc = _dot(xb, win_ref[...])
    cos = cos_ref[0]
    sin = sin_ref[0]
    cqn = _rms_rows(acc[:, _C_CQ:_C_CKV], MLA_Q_RANK).astype(BF16)
    qq = _dot(cqn, wuq_ref[...])
    ckvn = _rms_rows(acc[:, _C_CKV:_C_KRP], MLA_KV_RANK).astype(BF16)
    kv = _dot(ckvn, wukv_ref[...])
    k_rope = acc[:, _C_KRP:_C_KRS] * cos + acc[:, _C_KRS:_C_SSM] * sin
    for h in range(MLA_HEADS):
        lo, hi = h * HEAD_PAD, (h + 1) * HEAD_PAD
        q_ref[0, lo:hi, :] = (qq[:, lo:hi] * cos + qq[:, MLA_QK_W + lo:MLA_QK_W + hi] * sin).T.astype(BF16)
        k_ref[0, :, lo:hi] = (kv[:, lo:hi] + k_rope).astype(BF16)
    v_ref[0] = (kv[:, MLA_QK_W:] + ones_ref[...]).T.astype(BF16)
    u_ref[...] = acc[:, _C_SSM:_C_QSB].astype(BF16)
    qs_ref[0] = acc[:, _C_QSB:_C_KSB].T.astype(BF16)
    ks_ref[0] = acc[:, _C_KSB:_C_VSB].astype(BF16)
    vs_ref[0] = acc[:, _C_VSB:_C_END].T.astype(BF16)


def _inproj(x, win, wuq, wukv, cos_t, sin_t):
    bsz, seq, _ = x.shape
    ts = min(seq, 512)
    tok = lambda w: pl.BlockSpec((1, ts, w), lambda b, s: (b, s, 0))
    tok_t = lambda w: pl.BlockSpec((1, w, ts), lambda b, s: (b, 0, s))
    full = lambda a: pl.BlockSpec(a.shape, lambda b, s: (0,) * a.ndim)
    sb_w = SB_HEADS * HEAD_PAD
    lane = jnp.arange(MLA_QK_W) % HEAD_PAD
    ones_row = (lane == MLA_V).astype(F32).reshape(1, MLA_QK_W)
    return pl.pallas_call(
        _inproj_kernel,
        grid=(bsz, seq // ts),
        in_specs=[tok(D_MODEL), full(win), full(wuq), full(wukv), tok(HEAD_PAD), tok(HEAD_PAD), full(ones_row)],
        out_specs=[tok_t(MLA_QK_W), tok(MLA_QK_W), tok_t(MLA_QK_W),
                   pl.BlockSpec((ts, SSM_WIDTH), lambda b, s: (s, b)),
                   tok_t(sb_w), tok(sb_w), tok_t(SB_WIDTH)],
        out_shape=[jax.ShapeDtypeStruct((bsz, MLA_QK_W, seq), BF16),
                   jax.ShapeDtypeStruct((bsz, seq, MLA_QK_W), BF16),
                   jax.ShapeDtypeStruct((bsz, MLA_QK_W, seq), BF16),
                   jax.ShapeDtypeStruct((seq, bsz * SSM_WIDTH), BF16),
                   jax.ShapeDtypeStruct((bsz, sb_w, seq), BF16),
                   jax.ShapeDtypeStruct((bsz, seq, sb_w), BF16),
                   jax.ShapeDtypeStruct((bsz, SB_WIDTH, seq), BF16)],
        compiler_params=_params("parallel", "parallel"),
        name="inproj",
    )(x, win, wuq, wukv, cos_t, sin_t, ones_row)


def _mla_kernel(q_ref, k_ref, vt_ref, o_ref, m_ref, acc_ref, s_ref, p_ref, *, tq, tk):
    qi = pl.program_id(1)
    per = tq // tk
    query_chunk = (qi * tq + lax.broadcasted_iota(jnp.int32, (tk, tq), 1)) // CHUNK
    key_in_tile = lax.broadcasted_iota(jnp.int32, (tk, tq), 0)
    m_ref[...] = jnp.full(m_ref.shape, NEG_BIG, F32)
    acc_ref[...] = jnp.zeros(acc_ref.shape, F32)

    def block(kb, masked):
        ks = pl.multiple_of(kb * tk, tk)
        for h in range(MLA_HEADS):
            tile = slice(h * HEAD_PAD, (h + 1) * HEAD_PAD)
            s_ref[h] = _dot(k_ref[0, pl.ds(ks, tk), tile], q_ref[0, tile, :])
        if masked:
            allowed = (ks + key_in_tile) // CHUNK <= query_chunk
        alphas = []
        for h in range(MLA_HEADS):
            s = s_ref[h]
            if masked:
                s = jnp.where(allowed, s, NEG_BIG)
            m_old = m_ref[h]
            m_new = jnp.maximum(m_old, jnp.max(s, axis=0, keepdims=True))
            p_ref[h] = jnp.exp2(s - m_new).astype(BF16)
            alphas.append(jnp.exp2(m_old - m_new))
            m_ref[h] = m_new
        for h in range(MLA_HEADS):
            tile = slice(h * HEAD_PAD, (h + 1) * HEAD_PAD)
            used = slice(h * HEAD_PAD, h * HEAD_PAD + MLA_V_ROWS)
            acc_ref[h] = alphas[h] * acc_ref[h] + _dot(vt_ref[0, used, pl.ds(ks, tk)], p_ref[h])

    def body(kb, carry):
        block(kb, False)
        return carry

    lax.fori_loop(0, qi * per, body, 0)
    for d in range(per):
        block(qi * per + d, True)
    pad = jnp.zeros((HEAD_PAD - MLA_V, tq), F32)
    for h in range(MLA_HEADS):
        acc = acc_ref[h]
        out_t = jnp.concatenate([acc[0:MLA_V, :] / acc[MLA_V:MLA_V + 1, :], pad], axis=0)
        o_ref[0, :, h * HEAD_PAD:(h + 1) * HEAD_PAD] = out_t.T.astype(BF16)


def _mla_attention(qt, k, vt):
    bsz, seq, _ = k.shape
    tq, tk = min(seq, ATTN_TQ), min(seq, ATTN_TK)
    return pl.pallas_call(
        functools.partial(_mla_kernel, tq=tq, tk=tk),
        grid=(bsz, seq // tq),
        in_specs=[pl.BlockSpec((1, MLA_QK_W, tq), lambda b, i: (b, 0, i)),
                  pl.BlockSpec((1, seq, MLA_QK_W), lambda b, i: (b, 0, 0)),
                  pl.BlockSpec((1, MLA_QK_W, seq), lambda b, i: (b, 0, 0))],
        out_specs=pl.BlockSpec((1, tq, MLA_QK_W), lambda b, i: (b, i, 0)),
        out_shape=jax.ShapeDtypeStruct((bsz, seq, MLA_QK_W), BF16),
        scratch_shapes=[pltpu.VMEM((MLA_HEADS, 1, tq), F32), pltpu.VMEM((MLA_HEADS, MLA_V_ROWS, tq), F32),
                        pltpu.VMEM((MLA_HEADS, tk, tq), F32), pltpu.VMEM((MLA_HEADS, tk, tq), BF16)],
        compiler_params=_params("parallel", "arbitrary"),
        name="mla_attention",
    )(qt, k, vt)


def _sb_kernel(q_ref, k_ref, vt_ref, tri_ref, o_ref, right_ref, acc_ref, z_ref, split_ref, sum_ref, w_ref, *, tq, tk):
    qi = pl.program_id(1)
    per = tq // tk
    scan = min(tk, SB_SCAN)
    nscan = tk // scan
    query_pos = qi * tq + lax.broadcasted_iota(jnp.int32, (tk, tq), 1)
    key_in_tile = lax.broadcasted_iota(jnp.int32, (tk, tq), 0)
    right_ref[...] = jnp.zeros(right_ref.shape, F32)
    acc_ref[...] = jnp.zeros(acc_ref.shape, F32)

    def block(kb, masked):
        ks = pl.multiple_of(kb * tk, tk)
        for h in range(SB_HEADS):
            tile = slice(h * HEAD_PAD, (h + 1) * HEAD_PAD)
            z_ref[h] = _dot(k_ref[0, pl.ds(ks, tk), tile], q_ref[0, tile, :])
        if masked:
            earlier = ks + key_in_tile < query_pos
        for h in range(SB_HEADS):
            z = z_ref[h]
            neg_abs = pltpu.bitcast(pltpu.bitcast(z, jnp.uint32) | jnp.uint32(0x80000000), F32)
            fail = jnp.maximum(z, 0.0) + jnp.log2(1.0 + jnp.exp2(neg_abs))
            z_ref[h] = z - fail
            if masked:
                fail = jnp.where(earlier, fail, 0.0)
            for u in range(nscan):
                part = fail[u * scan:(u + 1) * scan, :]
                hi = part.astype(BF16)
                split_ref[h, 2 * u * scan:(2 * u + 1) * scan, :] = hi
                split_ref[h, (2 * u + 1) * scan:(2 * u + 2) * scan, :] = (part - hi.astype(F32)).astype(BF16)
                sum_ref[h, u] = jnp.sum(part, axis=0, keepdims=True)
        for h in range(SB_HEADS):
            after = right_ref[h]
            for u in range(nscan - 1, -1, -1):
                rows = slice(u * scan, (u + 1) * scan)
                between = _dot(tri_ref[...], split_ref[h, 2 * u * scan:(2 * u + 2) * scan, :]) + after
                w = jnp.exp2(z_ref[h, rows, :] - between)
                if masked:
                    w = jnp.where(earlier[rows, :], w, 0.0)
                w_ref[h, rows, :] = w.astype(BF16)
                after = after + sum_ref[h, u]
            right_ref[h] = after
        for h in range(SB_HEADS):
            pair = slice((h // 2) * HEAD_PAD, (h // 2 + 1) * HEAD_PAD)
            acc_ref[h] += _dot(vt_ref[0, pair, pl.ds(ks, tk)], w_ref[h])

    def body(i, carry):
        block(qi * per - 1 - i, False)
        return carry

    for d in range(per - 1, -1, -1):
        block(qi * per + d, True)
    lax.fori_loop(0, qi * per, body, 0)
    row = lax.broadcasted_iota(jnp.int32, (HEAD_PAD, tq), 0)
    for pair in range(SB_HEADS // 2):
        out_t = jnp.where(row < SB_DIM, acc_ref[2 * pair], acc_ref[2 * pair + 1])
        o_ref[0, :, pair * HEAD_PAD:(pair + 1) * HEAD_PAD] = out_t.T.astype(BF16)


def _sb_attention(qt, k, vt):
    bsz, seq, _ = k.shape
    tq, tk = min(seq, ATTN_TQ), min(seq, ATTN_TK)
    sb_w = SB_HEADS * HEAD_PAD
    scan = min(tk, SB_SCAN)
    tri = (lax.broadcasted_iota(jnp.int32, (scan, scan), 1) > lax.broadcasted_iota(jnp.int32, (scan, scan), 0)).astype(BF16)
    tri2 = jnp.concatenate([tri, tri], axis=1)
    return pl.pallas_call(
        functools.partial(_sb_kernel, tq=tq, tk=tk),
        grid=(bsz, seq // tq),
        in_specs=[pl.BlockSpec((1, sb_w, tq), lambda b, i: (b, 0, i)),
                  pl.BlockSpec((1, seq, sb_w), lambda b, i: (b, 0, 0)),
                  pl.BlockSpec((1, SB_WIDTH, seq), lambda b, i: (b, 0, 0)),
                  pl.BlockSpec((scan, 2 * scan), lambda b, i: (0, 0))],
        out_specs=pl.BlockSpec((1, tq, SB_WIDTH), lambda b, i: (b, i, 0)),
        out_shape=jax.ShapeDtypeStruct((bsz, seq, SB_WIDTH), BF16),
        scratch_shapes=[pltpu.VMEM((SB_HEADS, 1, tq), F32), pltpu.VMEM((SB_HEADS, HEAD_PAD, tq), F32),
                        pltpu.VMEM((SB_HEADS, tk, tq), F32), pltpu.VMEM((SB_HEADS, 2 * tk, tq), BF16),
                        pltpu.VMEM((SB_HEADS, tk // scan, 1, tq), F32), pltpu.VMEM((SB_HEADS, tk, tq), BF16)],
        compiler_params=_params("parallel", "arbitrary"),
        name="sb_attention",
    )(qt, k, vt, tri2)


def _ssm_kernel(u_ref, bbd_ref, lam_ref, cbd_ref, d_ref, wglu_ref, bglu_ref, y_ref, h_ref, hs_ref, *, tt, nb):
    @pl.when(pl.program_id(0) == 0)
    def _():
        h_ref[...] = jnp.zeros_like(h_ref)

    u = u_ref[...]
    half, wide = SSM_BLOCK_STATES, 2 * SSM_BLOCK_STATES
    for b in range(SSM_BLOCKS):
        hs_ref[:, b * wide:(b + 1) * wide] = _dot(u[:, b * SSM_BLOCK_CH:(b + 1) * SSM_BLOCK_CH], bbd_ref[b])
    lam_re = jnp.broadcast_to(lam_ref[0:1, :], (nb, SSM_STATES))
    lam_im = jnp.broadcast_to(lam_ref[1:2, :], (nb, SSM_STATES))

    def step(t, carry):
        r0 = pl.multiple_of(t * nb, nb)
        out = []
        for b in range(SSM_BLOCKS):
            h_re, h_im = carry[b]
            l_re, l_im = lam_re[:, b * half:(b + 1) * half], lam_im[:, b * half:(b + 1) * half]
            re_cols = slice(b * wide, b * wide + half)
            im_cols = slice(b * wide + half, (b + 1) * wide)
            n_re = l_re * h_re - l_im * h_im + hs_ref[pl.ds(r0, nb), re_cols]
            n_im = l_re * h_im + l_im * h_re + hs_ref[pl.ds(r0, nb), im_cols]
            hs_ref[pl.ds(r0, nb), re_cols] = n_re
            hs_ref[pl.ds(r0, nb), im_cols] = n_im
            out.append((n_re, n_im))
        return tuple(out)

    init = tuple((h_ref[:, b * wide:b * wide + half], h_ref[:, b * wide + half:(b + 1) * wide]) for b in range(SSM_BLOCKS))
    final = lax.fori_loop(0, tt, step, init, unroll=4)
    for b in range(SSM_BLOCKS):
        h_ref[:, b * wide:b * wide + half] = final[b][0]
        h_ref[:, b * wide + half:(b + 1) * wide] = final[b][1]

    y = jnp.concatenate([_dot(hs_ref[:, b * wide:(b + 1) * wide].astype(BF16), cbd_ref[b]) for b in range(SSM_BLOCKS)],
                        axis=1) + d_ref[...] * u.astype(F32)
    y = jax.nn.gelu(y)
    y = y * jax.nn.sigmoid(_dot(y.astype(BF16), wglu_ref[...]) + bglu_ref[...])
    y_ref[...] = y.astype(BF16)


def _ssm(u_tm, bbd, lam, cbd, d_row, wglu, bglu_row, nb):
    rows = u_tm.shape[0]
    seq = rows // nb
    tt = min(seq, 128)
    full = lambda a: pl.BlockSpec(a.shape, lambda i: (0,) * a.ndim)
    return pl.pallas_call(
        functools.partial(_ssm_kernel, tt=tt, nb=nb),
        grid=(seq // tt,),
        in_specs=[pl.BlockSpec((tt * nb, SSM_WIDTH), lambda i: (i, 0)),
                  full(bbd), full(lam), full(cbd), full(d_row), full(wglu), full(bglu_row)],
        out_specs=pl.BlockSpec((tt * nb, SSM_WIDTH), lambda i: (i, 0)),
        out_shape=jax.ShapeDtypeStruct((rows, SSM_WIDTH), BF16),
        scratch_shapes=[pltpu.VMEM((nb, 2 * SSM_STATES), F32),
                        pltpu.VMEM((tt * nb, 2 * SSM_STATES), F32)],
        compiler_params=_params("arbitrary"),
        name="ssm",
    )(u_tm, bbd, lam, cbd, d_row, wglu, bglu_row)


def _memkv_kernel(mem_ref, w_ref, k_ref, v_ref):
    kv = _dot(mem_ref[0].astype(BF16), w_ref[...])
    k_ref[0] = kv[:, :D_MODEL].astype(BF16)
    v_ref[0] = kv[:, D_MODEL:].astype(BF16)


def _memkv(mem, w_mkv):
    bsz, mlen, _ = mem.shape
    return pl.pallas_call(
        _memkv_kernel,
        grid=(bsz,),
        in_specs=[pl.BlockSpec((1, mlen, D_MODEL), lambda b: (b, 0, 0)),
                  pl.BlockSpec(w_mkv.shape, lambda b: (0, 0))],
        out_specs=[pl.BlockSpec((1, mlen, D_MODEL), lambda b: (b, 0, 0))] * 2,
        out_shape=[jax.ShapeDtypeStruct((bsz, mlen, D_MODEL), BF16)] * 2,
        compiler_params=_params("parallel"),
        name="mem_kv",
    )(mem, w_mkv)


def _mix_mem_kernel(ya_ref, ys_ref, yb_ref, x_ref, gmix_ref, wout_ref, ln1_ref,
                    mk_ref, mv_ref, wmq_ref, wmo_ref, ln2_ref, xt_ref,
                    x1_ref, xq_ref, s_ref, p_ref, o_ref, *, alpha, halves):
    gm = gmix_ref[...]
    c1, c2 = MLA_QK_W, MLA_QK_W + SSM_WIDTH
    rows = x_ref.shape[1] // halves
    parts = [slice(i * rows, (i + 1) * rows) for i in range(halves)]
    for r in parts:
        ya = (_rms_rows(ya_ref[0, r, :].astype(F32), MLA_V_W) * gm[:, 0:c1]).astype(BF16)
        ys = (_rms_rows(ys_ref[r, :].astype(F32), SSM_WIDTH) * gm[:, c1:c2]).astype(BF16)
        yb = (_rms_rows(yb_ref[0, r, :].astype(F32), SB_WIDTH) * gm[:, c2:]).astype(BF16)
        x1_ref[r, :] = _dot(ya, wout_ref[0:c1, :]) + _dot(ys, wout_ref[c1:c2, :]) + _dot(yb, wout_ref[c2:, :])
    for r in parts:
        x1 = _layer_norm_rows(alpha * x_ref[0, r, :] + x1_ref[r, :], ln1_ref[0:1, :], ln1_ref[1:2, :])
        x1_ref[r, :] = x1
        xq_ref[r, :] = x1.astype(BF16)
    for r in parts:
        xq_ref[r, :] = _dot(xq_ref[r, :], wmq_ref[...]).astype(BF16)
    for i, r in enumerate(parts):
        for h in range(MEM_HEADS):
            cols = slice(h * MEM_HEAD_DIM, (h + 1) * MEM_HEAD_DIM)
            s_ref[i * MEM_HEADS + h] = _dot_nt(xq_ref[r, cols], mk_ref[0, :, cols])
    for i in range(halves * MEM_HEADS):
        s = s_ref[i]
        p = jnp.exp(s - jnp.max(s, axis=-1, keepdims=True))
        p_ref[i] = (p / jnp.sum(p, axis=-1, keepdims=True)).astype(BF16)
    for i, r in enumerate(parts):
        for h in range(MEM_HEADS):
            cols = slice(h * MEM_HEAD_DIM, (h + 1) * MEM_HEAD_DIM)
            o_ref[r, cols] = _dot(p_ref[i * MEM_HEADS + h], mv_ref[0, :, cols]).astype(BF16)
    for r in parts:
        x2 = _layer_norm_rows(alpha * x1_ref[r, :] + _dot(o_ref[r, :], wmo_ref[...]), ln2_ref[0:1, :], ln2_ref[1:2, :])
        xt_ref[:, r] = x2.T


def _mix_mem(ya, ys_tm, yb, x, gmix, wout, ln1, mk, mv, wmq, wmo, ln2, alpha):
    bsz, seq, _ = x.shape
    ts = min(seq, 512)
    nst = seq // ts
    halves = 2
    mlen = mk.shape[1]
    tok = lambda w: pl.BlockSpec((1, ts, w), lambda b, s: (b, s, 0))
    full = lambda a: pl.BlockSpec(a.shape, lambda b, s: (0,) * a.ndim)
    return pl.pallas_call(
        functools.partial(_mix_mem_kernel, alpha=alpha, halves=halves),
        grid=(bsz, nst),
        in_specs=[tok(MLA_QK_W), pl.BlockSpec((ts, SSM_WIDTH), lambda b, s: (s, b)), tok(SB_WIDTH), tok(D_MODEL),
                  full(gmix), full(wout), full(ln1),
                  pl.BlockSpec((1, mlen, D_MODEL), lambda b, s: (b, 0, 0)),
                  pl.BlockSpec((1, mlen, D_MODEL), lambda b, s: (b, 0, 0)),
                  full(wmq), full(wmo), full(ln2)],
        out_specs=pl.BlockSpec((D_MODEL, ts), lambda b, s: (0, b * nst + s)),
        out_shape=jax.ShapeDtypeStruct((D_MODEL, bsz * seq), F32),
        scratch_shapes=[pltpu.VMEM((ts, D_MODEL), F32), pltpu.VMEM((ts, D_MODEL), BF16),
                        pltpu.VMEM((halves * MEM_HEADS, ts // halves, mlen), F32),
                        pltpu.VMEM((halves * MEM_HEADS, ts // halves, mlen), BF16),
                        pltpu.VMEM((ts, D_MODEL), BF16)],
        compiler_params=_params("parallel", "parallel"),
        name="mix_mem",
    )(ya, ys_tm, yb, x, gmix, wout, ln1, mk, mv, wmq, wmo, ln2)


_SORT16 = ((0, 1), (2, 3), (0, 2), (1, 3), (1, 2), (4, 5), (6, 7), (4, 6), (5, 7), (5, 6), (0, 4), (2, 6), (2, 4),
           (1, 5), (3, 7), (3, 5), (1, 2), (3, 4), (5, 6), (8, 9), (10, 11), (8, 10), (9, 11), (9, 10), (12, 13),
           (14, 15), (12, 14), (13, 15), (13, 14), (8, 12), (10, 14), (10, 12), (9, 13), (11, 15), (11, 13), (9, 10),
           (11, 12), (13, 14), (0, 8), (4, 12), (4, 8), (2, 10), (6, 14), (6, 10), (2, 4), (6, 8), (10, 12), (1, 9),
           (5, 13), (5, 9), (3, 11), (7, 15), (7, 11), (3, 5), (7, 9), (11, 13), (1, 2), (3, 4), (5, 6), (7, 8),
           (9, 10), (11, 12), (13, 14))


def _sort16_desc(v):
    v = list(v)
    for i, j in _SORT16:
        v[i], v[j] = jnp.maximum(v[i], v[j]), jnp.minimum(v[i], v[j])
    return v


def _merge_bitonic_desc(v):
    v = list(v)
    for j in (8, 4, 2, 1):
        for i in range(16):
            l = i ^ j
            if l > i:
                v[i], v[l] = jnp.maximum(v[i], v[l]), jnp.minimum(v[i], v[l])
    return v


def _top16_sorted(rows, sort_result=True):
    v = _sort16_desc(rows)
    for shift in (4, 2, 1):
        partner = [pltpu.roll(x, shift, axis=0) for x in v]
        v = [jnp.maximum(v[i], partner[15 - i]) for i in range(16)]
        if sort_result or shift != 1:
            v = _merge_bitonic_desc(v)
    return v


def _count_prefix(pred, v):
    p8 = pred(v[7])
    p4 = pred(jnp.where(p8, v[11], v[3]))
    p2 = pred(jnp.where(p8, jnp.where(p4, v[13], v[9]), jnp.where(p4, v[5], v[1])))
    p1 = pred(jnp.where(p8, jnp.where(p4, jnp.where(p2, v[14], v[12]), jnp.where(p2, v[10], v[8])),
                        jnp.where(p4, jnp.where(p2, v[6], v[4]), jnp.where(p2, v[2], v[0]))))
    count = (jnp.where(p8, 8.0, 0.0) + jnp.where(p4, 4.0, 0.0)) + (jnp.where(p2, 2.0, 0.0) + jnp.where(p1, 1.0, 0.0))
    return jnp.where(pred(v[15]), 16.0, count)


def _peer_route_kernel(xt_ref, wpq_ref, keys_ref, a_ref, n_ref, b_ref, r_ref, q_ref, s1_ref, s2_ref):
    q_ref[...] = _dot(wpq_ref[...], xt_ref[...].astype(BF16)).astype(BF16)
    lax.fori_loop(0, PEER_HEADS, functools.partial(_peer_route_head, q_ref, keys_ref, a_ref, n_ref, b_ref, r_ref,
                                                   s1_ref, s2_ref), 0)


def _peer_route_head(q_ref, keys_ref, a_ref, n_ref, b_ref, r_ref, s1_ref, s2_ref, h, carry):
    q0 = pl.multiple_of(h * (2 * PEER_KEY_DIM), 2 * PEER_KEY_DIM)
    s1_ref[...] = _dot(keys_ref[h, 0], q_ref[pl.ds(q0, PEER_KEY_DIM), :])
    s2_ref[...] = _dot(keys_ref[h, 1], q_ref[pl.ds(q0 + PEER_KEY_DIM, PEER_KEY_DIM), :])
    sub = lax.broadcasted_iota(jnp.int32, (8, LANE), 0)
    groups = PEER_N_KEYS // 8

    def spread(vals):
        out = vals[7]
        for j in range(6, -1, -1):
            out = jnp.where(sub == j, vals[j], out)
        return out

    def chunk(c, carry):
        ln = pl.ds(pl.multiple_of(c * LANE, LANE), LANE)
        rows1 = [s1_ref[8 * i:8 * i + 8, ln] for i in range(groups)]
        rows2 = [s2_ref[8 * i:8 * i + 8, ln] for i in range(groups)]
        v1 = _top16_sorted(rows1)
        v2 = _top16_sorted(rows2)
        v2lo, v2hi, v1hi = spread(v2[:8]), spread(v2[8:]), spread(v1[8:])
        cands = ([v1[0] + v2lo, v1[0] + v2hi] + [v1[r] + v2lo for r in range(1, 8)] + [v1hi + v2[0]]
                 + [v1[r] + v2hi for r in range(1, 7)])
        top = _top16_sorted(cands, sort_result=False)
        thr = functools.reduce(jnp.minimum, top)
        cmax = v1[0] + v2[0]
        inv_z = 1.0 / functools.reduce(jnp.add, [jnp.exp(t - cmax) for t in top])
        for i in range(groups):
            s = rows1[i]
            kept = _count_prefix(lambda t, s=s: s + t >= thr, v2)
            in_top = s >= v1[15]
            a_ref[i, h, :, ln] = jnp.where(in_top, jnp.exp(s - v1[0]), 0.0)
            n_ref[i, h, :, ln] = jnp.where(in_top, kept, 0.0)
        for i in range(groups // 2):
            bs, rs = [], []
            for s in (rows2[2 * i], rows2[2 * i + 1]):
                rank = _count_prefix(lambda t, s=s: t > s, v2)
                rs.append(rank)
                bs.append(jnp.where(rank < float(PEER_TOPK), jnp.exp(s - v2[0]) * inv_z, 0.0))
            b_ref[h, 16 * i:16 * i + 16, ln] = jnp.concatenate(bs, axis=0).astype(BF16)
            r_ref[h, 16 * i:16 * i + 16, ln] = jnp.concatenate(rs, axis=0).astype(BF16)
        return carry

    lax.fori_loop(0, s1_ref.shape[1] // LANE, chunk, 0)
    return carry


def _peer_route(xt, wpq_t, keys):
    ntok = xt.shape[1]
    tt = min(ntok, 512)
    nblk = PEER_N_KEYS // 8
    return pl.pallas_call(
        _peer_route_kernel,
        grid=(ntok // tt,),
        in_specs=[pl.BlockSpec((D_MODEL, tt), lambda i: (0, i)),
                  pl.BlockSpec(wpq_t.shape, lambda i: (0, 0)),
                  pl.BlockSpec(keys.shape, lambda i: (0, 0, 0, 0))],
        out_specs=[pl.BlockSpec((nblk, PEER_HEADS, 8, tt), lambda i: (0, 0, 0, i)),
                   pl.BlockSpec((nblk, PEER_HEADS, 8, tt), lambda i: (0, 0, 0, i)),
                   pl.BlockSpec((PEER_HEADS, PEER_N_KEYS, tt), lambda i: (0, 0, i)),
                   pl.BlockSpec((PEER_HEADS, PEER_N_KEYS, tt), lambda i: (0, 0, i))],
        out_shape=[jax.ShapeDtypeStruct((nblk, PEER_HEADS, 8, ntok), F32),
                   jax.ShapeDtypeStruct((nblk, PEER_HEADS, 8, ntok), F32),
                   jax.ShapeDtypeStruct((PEER_HEADS, PEER_N_KEYS, ntok), BF16),
                   jax.ShapeDtypeStruct((PEER_HEADS, PEER_N_KEYS, ntok), BF16)],
        scratch_shapes=[pltpu.VMEM((PEER_HEADS * 2 * PEER_KEY_DIM, tt), BF16),
                        pltpu.VMEM((PEER_N_KEYS, tt), F32), pltpu.VMEM((PEER_N_KEYS, tt), F32)],
        compiler_params=_params("parallel"),
        name="peer_route",
    )(xt, wpq_t, keys)


PEER_I1_PER_TILE = 8
PEER_TILE = PEER_I1_PER_TILE * PEER_N_KEYS


def _peer_dense_kernel(xt_ref, a_ref, n_ref, b_ref, r_ref, u_ref, vt_ref, ln_ref, o_ref,
                       acc_ref, xb_ref, gh_ref, *, alpha):
    e = pl.program_id(1)

    @pl.when(e == 0)
    def _():
        acc_ref[...] = jnp.zeros_like(acc_ref)
        xb_ref[...] = xt_ref[...].astype(BF16)

    tt = xb_ref.shape[1]
    rows = BF16_SUBLANES
    gh_ref[...] = _dot(u_ref[...], xb_ref[...]).astype(BF16)
    cw = min(tt, 2 * LANE)
    for j in range(PEER_I1_PER_TILE):
        for c in range(tt // cw):
            ln = slice(c * cw, (c + 1) * cw)
            a_rows = [jnp.broadcast_to(a_ref[0, h, j:j + 1, ln], (rows, cw)).astype(BF16) for h in range(PEER_HEADS)]
            n_rows = [jnp.broadcast_to(n_ref[0, h, j:j + 1, ln], (rows, cw)).astype(BF16) for h in range(PEER_HEADS)]
            for g in range(PEER_N_KEYS // rows):
                i2 = slice(g * rows, (g + 1) * rows)
                gate = None
                for h in range(PEER_HEADS):
                    term = jnp.where(r_ref[h, i2, ln] < n_rows[h], b_ref[h, i2, ln] * a_rows[h], jnp.zeros((), BF16))
                    gate = term if gate is None else gate + term
                e0 = j * PEER_N_KEYS + g * rows
                gh_ref[e0:e0 + rows, ln] = gate * _gelu_tanh(gh_ref[e0:e0 + rows, ln])
    acc_ref[...] += _dot(vt_ref[...], gh_ref[...])

    @pl.when(e == pl.num_programs(1) - 1)
    def _():
        y = alpha * xt_ref[...] + acc_ref[...]
        mu = jnp.mean(y, axis=0, keepdims=True)
        c = y - mu
        var = jnp.mean(c * c, axis=0, keepdims=True)
        o_ref[...] = (c * lax.rsqrt(var + NORM_EPS)).T * ln_ref[0:1, :] + ln_ref[1:2, :]


def _peer_dense(xt, a, n, b, r, u, vt, ln, alpha):
    ntok = xt.shape[1]
    tt = min(ntok, 1024)
    ntile = PEER_EXPERTS // PEER_TILE
    return pl.pallas_call(
        functools.partial(_peer_dense_kernel, alpha=alpha),
        grid=(ntok // tt, ntile),
        in_specs=[pl.BlockSpec((D_MODEL, tt), lambda i, e: (0, i)),
                  pl.BlockSpec((1, PEER_HEADS, PEER_I1_PER_TILE, tt), lambda i, e: (e, 0, 0, i)),
                  pl.BlockSpec((1, PEER_HEADS, PEER_I1_PER_TILE, tt), lambda i, e: (e, 0, 0, i)),
                  pl.BlockSpec((PEER_HEADS, PEER_N_KEYS, tt), lambda i, e: (0, 0, i)),
                  pl.BlockSpec((PEER_HEADS, PEER_N_KEYS, tt), lambda i, e: (0, 0, i)),
                  pl.BlockSpec((PEER_TILE, D_MODEL), lambda i, e: (e, 0)),
                  pl.BlockSpec((D_MODEL, PEER_TILE), lambda i, e: (0, e)),
                  pl.BlockSpec(ln.shape, lambda i, e: (0, 0))],
        out_specs=pl.BlockSpec((tt, D_MODEL), lambda i, e: (i, 0)),
        out_shape=jax.ShapeDtypeStruct((ntok, D_MODEL), F32),
        scratch_shapes=[pltpu.VMEM((D_MODEL, tt), F32), pltpu.VMEM((D_MODEL, tt), BF16),
                        pltpu.VMEM((PEER_TILE, tt), BF16)],
        compiler_params=_params("parallel", "arbitrary"),
        name="peer_dense",
    )(xt, a, n, b, r, u, vt, ln)


def _pad_heads(w, heads, width):
    rows = w.shape[0]
    w = w.reshape(rows, heads, width)
    return jnp.pad(w, ((0, 0), (0, 0), (0, HEAD_PAD - width))).reshape(rows, heads * HEAD_PAD)


def _rotate_half_cols(w, heads, width, nope):
    rows = w.shape[0]
    w = w.reshape(rows, heads, width)
    half = (width - nope) // 2
    x1, x2 = w[..., nope:nope + half], w[..., nope + half:]
    out = jnp.concatenate([jnp.zeros_like(w[..., :nope]), -x2, x1], axis=-1)
    return out.reshape(rows, heads * width)


def _pack_inproj(w_in, g_cq, g_ckv, w_uq, w_ukv):
    c = 0
    cols = {}
    for name, width in (("cq", MLA_Q_RANK), ("ckv", MLA_KV_RANK), ("kr", MLA_ROPE), ("ssm", SSM_WIDTH),
                        ("qsb", SB_WIDTH), ("ksb", SB_WIDTH), ("vsb", SB_WIDTH)):
        cols[name] = w_in[:, c:c + width]
        c += width
    rows = w_in.shape[0]
    zeros = lambda n: jnp.zeros((rows, n), F32)
    kr = cols["kr"]
    half = MLA_ROPE // 2
    kr_plain = jnp.concatenate([zeros(MLA_NOPE), kr, zeros(HEAD_PAD - MLA_NOPE - MLA_ROPE)], axis=1)
    kr_swap = jnp.concatenate([zeros(MLA_NOPE), -kr[:, half:], kr[:, :half],
                               zeros(HEAD_PAD - MLA_NOPE - MLA_ROPE)], axis=1)
    sb_scale = SB_DIM ** -0.5 * LOG2E
    win = jnp.concatenate([cols["cq"], cols["ckv"], kr_plain, kr_swap, cols["ssm"],
                           _pad_heads(cols["qsb"] * sb_scale, SB_HEADS, SB_DIM),
                           _pad_heads(cols["ksb"], SB_HEADS, SB_DIM), cols["vsb"]], axis=1).astype(BF16)

    qk_dim = MLA_NOPE + MLA_ROPE
    wq = w_uq * (g_cq[:, None] * (qk_dim ** -0.5 * LOG2E))
    wuq = jnp.concatenate([_pad_heads(wq, MLA_HEADS, qk_dim),
                           _pad_heads(_rotate_half_cols(wq, MLA_HEADS, qk_dim, MLA_NOPE), MLA_HEADS, qk_dim)],
                          axis=1).astype(BF16)
    wkv = (w_ukv * g_ckv[:, None]).reshape(MLA_KV_RANK, MLA_HEADS, MLA_NOPE + MLA_V)
    wukv = jnp.concatenate([_pad_heads(wkv[..., :MLA_NOPE].reshape(MLA_KV_RANK, -1), MLA_HEADS, MLA_NOPE),
                            _pad_heads(wkv[..., MLA_NOPE:].reshape(MLA_KV_RANK, -1), MLA_HEADS, MLA_V)],
                           axis=1).astype(BF16)
    return win, wuq, wukv


def _pad_mla_rows(w):
    cols = w.shape[1]
    head_rows = jnp.pad(w[:MLA_V_W].reshape(MLA_HEADS, MLA_V, cols), ((0, 0), (0, HEAD_PAD - MLA_V), (0, 0)))
    return jnp.concatenate([head_rows.reshape(MLA_QK_W, cols), w[MLA_V_W:]], axis=0)


def _pack_ssm(lam_re, lam_im, log_step, b_re, b_im, c_re, c_im, d_skip):
    step = jnp.exp(log_step)[:, None]
    decay = jnp.exp(lam_re * step)
    ab_re, ab_im = decay * jnp.cos(lam_im * step), decay * jnp.sin(lam_im * step)
    inv = 1.0 / (lam_re * lam_re + lam_im * lam_im)
    f_re = ((ab_re - 1.0) * lam_re + ab_im * lam_im) * inv
    f_im = (ab_im * lam_re - (ab_re - 1.0) * lam_im) * inv
    bb_re = f_re[..., None] * b_re - f_im[..., None] * b_im
    bb_im = f_re[..., None] * b_im + f_im[..., None] * b_re
    per = SSM_GROUPS // SSM_BLOCKS
    eye = jnp.eye(per, dtype=F32)
    split = lambda w: w.reshape((SSM_BLOCKS, per) + w.shape[1:])
    blk = lambda w: jnp.einsum("bgph,gk->bghkp", split(w), eye).reshape(SSM_BLOCKS, SSM_BLOCK_CH, SSM_BLOCK_STATES)
    bbd = jnp.concatenate([blk(bb_re), blk(bb_im)], axis=2).astype(BF16)
    blk_c = lambda w: jnp.einsum("bghp,gk->bgpkh", split(w), eye).reshape(SSM_BLOCKS, SSM_BLOCK_STATES, SSM_BLOCK_CH)
    cbd = jnp.concatenate([blk_c(c_re), blk_c(-c_im)], axis=1).astype(BF16)
    lam = jnp.stack([ab_re.reshape(-1), ab_im.reshape(-1)])
    return bbd, lam, cbd, d_skip.reshape(1, SSM_WIDTH)


def kernel(x, mem, positions, w_in, g_cq, g_ckv, w_uq, w_ukv, ssm_lam_re, ssm_lam_im, ssm_log_step, ssm_b_re, ssm_b_im, ssm_c_re, ssm_c_im, ssm_d, w_glu, b_glu, g_mix, w_out, ln_mix_g, ln_mix_b, w_mq, w_mkv, w_mo, ln_mem_g, ln_mem_b, w_pq, peer_sub_keys, peer_u, peer_v, ln_ffn_g, ln_ffn_b):
    bsz, seq, _ = x.shape
    depth = w_in.shape[0]
    alpha = (2 * depth) ** 0.25
    cos_t, sin_t = _rope_tables(positions)
    for l in range(depth):
        win, wuq, wukv = _pack_inproj(w_in[l], g_cq[l], g_ckv[l], w_uq[l], w_ukv[l])
        q, k, v, u_tm, q_sb, k_sb, v_sb = _inproj(x, win, wuq, wukv, cos_t, sin_t)
        y_mla = _mla_attention(q, k, v)
        y_sb = _sb_attention(q_sb, k_sb, v_sb)
        bbd, lam, cbd, d_row = _pack_ssm(ssm_lam_re[l], ssm_lam_im[l], ssm_log_step[l], ssm_b_re[l], ssm_b_im[l],
                                         ssm_c_re[l], ssm_c_im[l], ssm_d[l])
        y_ssm = _ssm(u_tm.reshape(seq * bsz, SSM_WIDTH), bbd, lam, cbd, d_row,
                     w_glu[l].astype(BF16), b_glu[l].reshape(1, SSM_WIDTH), bsz)
        mk, mv = _memkv(mem, w_mkv[l].astype(BF16))
        xt = _mix_mem(y_mla, y_ssm.reshape(seq, bsz * SSM_WIDTH), y_sb, x,
                      _pad_mla_rows(g_mix[l][:, None]).reshape(1, -1), _pad_mla_rows(w_out[l]).astype(BF16),
                      jnp.stack([ln_mix_g[l], ln_mix_b[l]]),
                      mk, mv, (w_mq[l] * MEM_HEAD_DIM ** -0.5).astype(BF16), w_mo[l].astype(BF16),
                      jnp.stack([ln_mem_g[l], ln_mem_b[l]]), alpha)
        a, n, b, r = _peer_route(xt, w_pq[l].T.astype(BF16), peer_sub_keys[l].astype(BF16))
        x = _peer_dense(xt, a, n, b, r, peer_u[l].astype(BF16), peer_v[l].T.astype(BF16),
                        jnp.stack([ln_ffn_g[l], ln_ffn_b[l]]), alpha).reshape(bsz, seq, D_MODEL)
    return x
```

```python
import functools
import math

import jax
import jax.numpy as jnp
from jax import lax
from jax.experimental import pallas as pl
from jax.experimental.pallas import tpu as pltpu

F32 = jnp.float32
BF16 = jnp.bfloat16

D_MODEL = 1024
CHUNK = 64
NORM_EPS = 1e-5
NEG_BIG = -1e30
LOG2E = math.log2(math.e)

MLA_HEADS = 6
MLA_NOPE = 64
MLA_ROPE = 32
MLA_V = 64
MLA_Q_RANK = 256
MLA_KV_RANK = 128
ROPE_THETA = 10000.0

SSM_GROUPS = 24
SSM_CH = 16
SSM_STATE = 64
SSM_WIDTH = SSM_GROUPS * SSM_CH
SSM_STATES = SSM_GROUPS * SSM_STATE
SSM_BLOCKS = 3
SSM_BLOCK_CH = SSM_WIDTH // SSM_BLOCKS
SSM_BLOCK_STATES = SSM_STATES // SSM_BLOCKS

SB_HEADS = 4
SB_DIM = 64
SB_WIDTH = SB_HEADS * SB_DIM

MEM_HEADS = 4
MEM_HEAD_DIM = D_MODEL // MEM_HEADS

PEER_HEADS = 8
PEER_N_KEYS = 128
PEER_TOPK = 16
PEER_KEY_DIM = 128
PEER_EXPERTS = PEER_N_KEYS * PEER_N_KEYS

LANE = 128
BF16_SUBLANES = 16
HEAD_PAD = 128
ATTN_TQ = 512
ATTN_TK = 256
SB_SCAN = 128

VMEM_LIMIT = 56 * 1024 * 1024

_C_CQ = 0
_C_CKV = _C_CQ + MLA_Q_RANK
_C_KRP = _C_CKV + MLA_KV_RANK
_C_KRS = _C_KRP + HEAD_PAD
_C_SSM = _C_KRS + HEAD_PAD
_C_QSB = _C_SSM + SSM_WIDTH
_C_KSB = _C_QSB + SB_HEADS * HEAD_PAD
_C_VSB = _C_KSB + SB_HEADS * HEAD_PAD
_C_END = _C_VSB + SB_WIDTH
MLA_QK_W = MLA_HEADS * HEAD_PAD
MLA_V_W = MLA_HEADS * MLA_V
MLA_V_ROWS = MLA_V + BF16_SUBLANES


def _params(*sem):
    return pltpu.CompilerParams(dimension_semantics=sem, vmem_limit_bytes=VMEM_LIMIT)


def _dot(a, b):
    return jnp.dot(a, b, preferred_element_type=F32)


def _dot_nt(a, b):
    return lax.dot_general(a, b, (((1,), (1,)), ((), ())), preferred_element_type=F32)


def _gelu_tanh(x):
    c = math.sqrt(2.0 / math.pi)
    inner = x * (c + (c * 0.044715) * (x * x))
    return (0.5 * x) * (1.0 + jnp.tanh(inner))


def _layer_norm_rows(v, g, b):
    mu = jnp.mean(v, axis=-1, keepdims=True)
    c = v - mu
    var = jnp.mean(c * c, axis=-1, keepdims=True)
    return c * lax.rsqrt(var + NORM_EPS) * g + b


def _rms_rows(v, width):
    return v * lax.rsqrt(jnp.sum(v * v, axis=-1, keepdims=True) * (1.0 / width) + NORM_EPS)


def _rope_kernel(pos_ref, freq_ref, cos_ref, sin_ref):
    ang = pos_ref[0].astype(F32) * freq_ref[...]
    lane = lax.broadcasted_iota(jnp.int32, ang.shape, 1)
    rot = (lane >= MLA_NOPE) & (lane < MLA_NOPE + MLA_ROPE)
    cos_ref[0] = jnp.where(rot, jnp.cos(ang), jnp.where(lane < MLA_NOPE, 1.0, 0.0))
    sin_ref[0] = jnp.where(rot, jnp.sin(ang), 0.0)


def _rope_tables(positions):
    bsz, seq = positions.shape
    ts = min(seq, 512)
    half = MLA_ROPE // 2
    freq = ROPE_THETA ** (-jnp.arange(half, dtype=F32) / half)
    freq_row = jnp.zeros((1, HEAD_PAD), F32).at[0, MLA_NOPE:MLA_NOPE + MLA_ROPE].set(jnp.tile(freq, 2))
    return pl.pallas_call(
        _rope_kernel,
        grid=(bsz, seq // ts),
        in_specs=[pl.BlockSpec((1, ts, 1), lambda b, s: (b, s, 0)),
                  pl.BlockSpec((1, HEAD_PAD), lambda b, s: (0, 0))],
        out_specs=[pl.BlockSpec((1, ts, HEAD_PAD), lambda b, s: (b, s, 0))] * 2,
        out_shape=[jax.ShapeDtypeStruct((bsz, seq, HEAD_PAD), F32)] * 2,
        compiler_params=_params("parallel", "parallel"),
        name="rope_tables",
    )(positions.reshape(bsz, seq, 1), freq_row)


def _inproj_kernel(x_ref, win_ref, wuq_ref, wukv_ref, cos_ref, sin_ref, ones_ref,
                   q_ref, k_ref, v_ref, u_ref, qs_ref, ks_ref, vs_ref):
    xb = x_ref[0].astype(BF16)
    acc = _dot(xb, win_ref[...])
    cos = cos_ref[0]
    sin = sin_ref[0]
    cqn = _rms_rows(acc[:, _C_CQ:_C_CKV], MLA_Q_RANK).astype(BF16)
    qq = _dot(cqn, wuq_ref[...])
    ckvn = _rms_rows(acc[:, _C_CKV:_C_KRP], MLA_KV_RANK).astype(BF16)
    kv = _dot(ckvn, wukv_ref[...])
    k_rope = acc[:, _C_KRP:_C_KRS] * cos + acc[:, _C_KRS:_C_SSM] * sin
    for h in range(MLA_HEADS):
        lo, hi = h * HEAD_PAD, (h + 1) * HEAD_PAD
        q_ref[0, lo:hi, :] = (qq[:, lo:hi] * cos + qq[:, MLA_QK_W + lo:MLA_QK_W + hi] * sin).T.astype(BF16)
        k_ref[0, :, lo:hi] = (kv[:, lo:hi] + k_rope).astype(BF16)
    v_ref[0] = (kv[:, MLA_QK_W:] + ones_ref[...]).T.astype(BF16)
    u_ref[...] = acc[:, _C_SSM:_C_QSB].astype(BF16)
    qs_ref[0] = acc[:, _C_QSB:_C_KSB].T.astype(BF16)
    ks_ref[0] = acc[:, _C_KSB:_C_VSB].astype(BF16)
    vs_ref[0] = acc[:, _C_VSB:_C_END].T.astype(BF16)


def _inproj(x, win, wuq, wukv, cos_t, sin_t):
    bsz, seq, _ = x.shape
    ts = min(seq, 512)
    tok = lambda w: pl.BlockSpec((1, ts, w), lambda b, s: (b, s, 0))
    tok_t = lambda w: pl.BlockSpec((1, w, ts), lambda b, s: (b, 0, s))
    full = lambda a: pl.BlockSpec(a.shape, lambda b, s: (0,) * a.ndim)
    sb_w = SB_HEADS * HEAD_PAD
    lane = jnp.arange(MLA_QK_W) % HEAD_PAD
    ones_row = (lane == MLA_V).astype(F32).reshape(1, MLA_QK_W)
    return pl.pallas_call(
        _inproj_kernel,
        grid=(bsz, seq // ts),
        in_specs=[tok(D_MODEL), full(win), full(wuq), full(wukv), tok(HEAD_PAD), tok(HEAD_PAD), full(ones_row)],
        out_specs=[tok_t(MLA_QK_W), tok(MLA_QK_W), tok_t(MLA_QK_W),
                   pl.BlockSpec((ts, SSM_WIDTH), lambda b, s: (s, b)),
                   tok_t(sb_w), tok(sb_w), tok_t(SB_WIDTH)],
        out_shape=[jax.ShapeDtypeStruct((bsz, MLA_QK_W, seq), BF16),
                   jax.ShapeDtypeStruct((bsz, seq, MLA_QK_W), BF16),
                   jax.ShapeDtypeStruct((bsz, MLA_QK_W, seq), BF16),
                   jax.ShapeDtypeStruct((seq, bsz * SSM_WIDTH), BF16),
                   jax.ShapeDtypeStruct((bsz, sb_w, seq), BF16),
                   jax.ShapeDtypeStruct((bsz, seq, sb_w), BF16),
                   jax.ShapeDtypeStruct((bsz, SB_WIDTH, seq), BF16)],
        compiler_params=_params("parallel", "parallel"),
        name="inproj",
    )(x, win, wuq, wukv, cos_t, sin_t, ones_row)


def _mla_kernel(q_ref, k_ref, vt_ref, o_ref, m_ref, acc_ref, s2_ref, p2_ref, *, tq, tk):
    qi = pl.program_id(1)
    per = tq // tk
    query_chunk = (qi * tq + lax.broadcasted_iota(jnp.int32, (tk, tq), 1)) // CHUNK
    key_in_tile = lax.broadcasted_iota(jnp.int32, (tk, tq), 0)
    m_ref[...] = jnp.full(m_ref.shape, NEG_BIG, F32)
    acc_ref[...] = jnp.zeros(acc_ref.shape, F32)

    def block(kb, masked, slot=0):
        ks = pl.multiple_of(kb * tk, tk)
        s_ref, p_ref = s2_ref.at[slot], p2_ref.at[slot]
        for h in range(MLA_HEADS):
            tile = slice(h * HEAD_PAD, (h + 1) * HEAD_PAD)
            s_ref[h] = _dot(k_ref[0, pl.ds(ks, tk), tile], q_ref[0, tile, :])
        if masked:
            allowed = (ks + key_in_tile) // CHUNK <= query_chunk
        alphas = []
        for h in range(MLA_HEADS):
            s = s_ref[h]
            if masked:
                s = jnp.where(allowed, s, NEG_BIG)
            m_old = m_ref[h]
            m_new = jnp.maximum(m_old, jnp.max(s, axis=0, keepdims=True))
            p_ref[h] = jnp.exp2(s - m_new).astype(BF16)
            alphas.append(jnp.exp2(m_old - m_new))
            m_ref[h] = m_new
        for h in range(MLA_HEADS):
            tile = slice(h * HEAD_PAD, (h + 1) * HEAD_PAD)
            used = slice(h * HEAD_PAD, h * HEAD_PAD + MLA_V_ROWS)
            acc_ref[h] = alphas[h] * acc_ref[h] + _dot(vt_ref[0, used, pl.ds(ks, tk)], p_ref[h])

    def body(i, carry):
        for d in range(per):
            block(i * per + d, False, d)
        return carry

    lax.fori_loop(0, qi, body, 0)
    for d in range(per):
        block(qi * per + d, True, d)
    pad = jnp.zeros((HEAD_PAD - MLA_V, tq), F32)
    for h in range(MLA_HEADS):
        acc = acc_ref[h]
        out_t = jnp.concatenate([acc[0:MLA_V, :] / acc[MLA_V:MLA_V + 1, :], pad], axis=0)
        o_ref[0, :, h * HEAD_PAD:(h + 1) * HEAD_PAD] = out_t.T.astype(BF16)


def _mla_attention(qt, k, vt):
    bsz, seq, _ = k.shape
    tq, tk = min(seq, ATTN_TQ), min(seq, ATTN_TK)
    return pl.pallas_call(
        functools.partial(_mla_kernel, tq=tq, tk=tk),
        grid=(bsz, seq // tq),
        in_specs=[pl.BlockSpec((1, MLA_QK_W, tq), lambda b, i: (b, 0, i)),
                  pl.BlockSpec((1, seq, MLA_QK_W), lambda b, i: (b, 0, 0)),
                  pl.BlockSpec((1, MLA_QK_W, seq), lambda b, i: (b, 0, 0))],
        out_specs=pl.BlockSpec((1, tq, MLA_QK_W), lambda b, i: (b, i, 0)),
        out_shape=jax.ShapeDtypeStruct((bsz, seq, MLA_QK_W), BF16),
        scratch_shapes=[pltpu.VMEM((MLA_HEADS, 1, tq), F32), pltpu.VMEM((MLA_HEADS, MLA_V_ROWS, tq), F32),
                        pltpu.VMEM((tq // tk, MLA_HEADS, tk, tq), F32), pltpu.VMEM((tq // tk, MLA_HEADS, tk, tq), BF16)],
        compiler_params=_params("parallel", "arbitrary"),
        name="mla_attention",
    )(qt, k, vt)


def _sb_kernel(q_ref, k_ref, vt_ref, tri_ref, o_ref, right_ref, acc_ref, z2_ref, split2_ref, sum2_ref, w2_ref, *, tq, tk):
    qi = pl.program_id(1)
    per = tq // tk
    scan = min(tk, SB_SCAN)
    nscan = tk // scan
    query_pos = qi * tq + lax.broadcasted_iota(jnp.int32, (tk, tq), 1)
    key_in_tile = lax.broadcasted_iota(jnp.int32, (tk, tq), 0)
    right_ref[...] = jnp.zeros(right_ref.shape, F32)
    acc_ref[...] = jnp.zeros(acc_ref.shape, F32)

    def block(kb, masked, slot=0):
        ks = pl.multiple_of(kb * tk, tk)
        z_ref, split_ref, sum_ref, w_ref = z2_ref.at[slot], split2_ref.at[slot], sum2_ref.at[slot], w2_ref.at[slot]
        for h in range(SB_HEADS):
            tile = slice(h * HEAD_PAD, (h + 1) * HEAD_PAD)
            z_ref[h] = _dot(k_ref[0, pl.ds(ks, tk), tile], q_ref[0, tile, :])
        if masked:
            earlier = ks + key_in_tile < query_pos
        for h in range(SB_HEADS):
            z = z_ref[h]
            neg_abs = pltpu.bitcast(pltpu.bitcast(z, jnp.uint32) | jnp.uint32(0x80000000), F32)
            fail = jnp.maximum(z, 0.0) + jnp.log2(1.0 + jnp.exp2(neg_abs))
            z_ref[h] = z - fail
            if masked:
                fail = jnp.where(earlier, fail, 0.0)
            for u in range(nscan):
                part = fail[u * scan:(u + 1) * scan, :]
                hi = part.astype(BF16)
                split_ref[h, 2 * u * scan:(2 * u + 1) * scan, :] = hi
                split_ref[h, (2 * u + 1) * scan:(2 * u + 2) * scan, :] = (part - hi.astype(F32)).astype(BF16)
                sum_ref[h, u] = jnp.sum(part, axis=0, keepdims=True)
        for h in range(SB_HEADS):
            after = right_ref[h]
            for u in range(nscan - 1, -1, -1):
                rows = slice(u * scan, (u + 1) * scan)
                between = _dot(tri_ref[...], split_ref[h, 2 * u * scan:(2 * u + 2) * scan, :]) + after
                w = jnp.exp2(z_ref[h, rows, :] - between)
                if masked:
                    w = jnp.where(earlier[rows, :], w, 0.0)
                w_ref[h, rows, :] = w.astype(BF16)
                after = after + sum_ref[h, u]
            right_ref[h] = after
        for h in range(SB_HEADS):
            pair = slice((h // 2) * HEAD_PAD, (h // 2 + 1) * HEAD_PAD)
            acc_ref[h] += _dot(vt_ref[0, pair, pl.ds(ks, tk)], w_ref[h])

    def body(i, carry):
        for d in range(per):
            block((qi - i) * per - 1 - d, False, d)
        return carry

    for d in range(per - 1, -1, -1):
        block(qi * per + d, True, d)
    lax.fori_loop(0, qi, body, 0)
    row = lax.broadcasted_iota(jnp.int32, (HEAD_PAD, tq), 0)
    for pair in range(SB_HEADS // 2):
        out_t = jnp.where(row < SB_DIM, acc_ref[2 * pair], acc_ref[2 * pair + 1])
        o_ref[0, :, pair * HEAD_PAD:(pair + 1) * HEAD_PAD] = out_t.T.astype(BF16)


def _sb_attention(qt, k, vt):
    bsz, seq, _ = k.shape
    tq, tk = min(seq, ATTN_TQ), min(seq, ATTN_TK)
    sb_w = SB_HEADS * HEAD_PAD
    scan = min(tk, SB_SCAN)
    tri = (lax.broadcasted_iota(jnp.int32, (scan, scan), 1) > lax.broadcasted_iota(jnp.int32, (scan, scan), 0)).astype(BF16)
    tri2 = jnp.concatenate([tri, tri], axis=1)
    return pl.pallas_call(
        functools.partial(_sb_kernel, tq=tq, tk=tk),
        grid=(bsz, seq // tq),
        in_specs=[pl.BlockSpec((1, sb_w, tq), lambda b, i: (b, 0, i)),
                  pl.BlockSpec((1, seq, sb_w), lambda b, i: (b, 0, 0)),
                  pl.BlockSpec((1, SB_WIDTH, seq), lambda b, i: (b, 0, 0)),
                  pl.BlockSpec((scan, 2 * scan), lambda b, i: (0, 0))],
        out_specs=pl.BlockSpec((1, tq, SB_WIDTH), lambda b, i: (b, i, 0)),
        out_shape=jax.ShapeDtypeStruct((bsz, seq, SB_WIDTH), BF16),
        scratch_shapes=[pltpu.VMEM((SB_HEADS, 1, tq), F32), pltpu.VMEM((SB_HEADS, HEAD_PAD, tq), F32),
                        pltpu.VMEM((tq // tk, SB_HEADS, tk, tq), F32), pltpu.VMEM((tq // tk, SB_HEADS, 2 * tk, tq), BF16),
                        pltpu.VMEM((tq // tk, SB_HEADS, tk // scan, 1, tq), F32),
                        pltpu.VMEM((tq // tk, SB_HEADS, tk, tq), BF16)],
        compiler_params=_params("parallel", "arbitrary"),
        name="sb_attention",
    )(qt, k, vt, tri2)


def _ssm_kernel(u_ref, bbd_ref, lam_ref, cbd_ref, d_ref, wglu_ref, bglu_ref, y_ref, h_ref, hs_ref, *, tt, nb):
    @pl.when(pl.program_id(0) == 0)
    def _():
        h_ref[...] = jnp.zeros_like(h_ref)

    u = u_ref[...]
    half, wide = SSM_BLOCK_STATES, 2 * SSM_BLOCK_STATES
    for b in range(SSM_BLOCKS):
        hs_ref[:, b * wide:(b + 1) * wide] = _dot(u[:, b * SSM_BLOCK_CH:(b + 1) * SSM_BLOCK_CH], bbd_ref[b])
    lam_re = jnp.broadcast_to(lam_ref[0:1, :], (nb, SSM_STATES))
    lam_im = jnp.broadcast_to(lam_ref[1:2, :], (nb, SSM_STATES))

    def step(t, carry):
        r0 = pl.multiple_of(t * nb, nb)
        out = []
        for b in range(SSM_BLOCKS):
            h_re, h_im = carry[b]
            l_re, l_im = lam_re[:, b * half:(b + 1) * half], lam_im[:, b * half:(b + 1) * half]
            re_cols = slice(b * wide, b * wide + half)
            im_cols = slice(b * wide + half, (b + 1) * wide)
            n_re = l_re * h_re - l_im * h_im + hs_ref[pl.ds(r0, nb), re_cols]
            n_im = l_re * h_im + l_im * h_re + hs_ref[pl.ds(r0, nb), im_cols]
            hs_ref[pl.ds(r0, nb), re_cols] = n_re
            hs_ref[pl.ds(r0, nb), im_cols] = n_im
            out.append((n_re, n_im))
        return tuple(out)

    init = tuple((h_ref[:, b * wide:b * wide + half], h_ref[:, b * wide + half:(b + 1) * wide]) for b in range(SSM_BLOCKS))
    final = lax.fori_loop(0, tt, step, init, unroll=4)
    for b in range(SSM_BLOCKS):
        h_ref[:, b * wide:b * wide + half] = final[b][0]
        h_ref[:, b * wide + half:(b + 1) * wide] = final[b][1]

    y = jnp.concatenate([_dot(hs_ref[:, b * wide:(b + 1) * wide].astype(BF16), cbd_ref[b]) for b in range(SSM_BLOCKS)],
                        axis=1) + d_ref[...] * u.astype(F32)
    y = jax.nn.gelu(y)
    y = y * jax.nn.sigmoid(_dot(y.astype(BF16), wglu_ref[...]) + bglu_ref[...])
    y_ref[...] = y.astype(BF16)


def _ssm(u_tm, bbd, lam, cbd, d_row, wglu, bglu_row, nb):
    rows = u_tm.shape[0]
    seq = rows // nb
    tt = min(seq, 128)
    full = lambda a: pl.BlockSpec(a.shape, lambda i: (0,) * a.ndim)
    return pl.pallas_call(
        functools.partial(_ssm_kernel, tt=tt, nb=nb),
        grid=(seq // tt,),
        in_specs=[pl.BlockSpec((tt * nb, SSM_WIDTH), lambda i: (i, 0)),
                  full(bbd), full(lam), full(cbd), full(d_row), full(wglu), full(bglu_row)],
        out_specs=pl.BlockSpec((tt * nb, SSM_WIDTH), lambda i: (i, 0)),
        out_shape=jax.ShapeDtypeStruct((rows, SSM_WIDTH), BF16),
        scratch_shapes=[pltpu.VMEM((nb, 2 * SSM_STATES), F32),
                        pltpu.VMEM((tt * nb, 2 * SSM_STATES), F32)],
        compiler_params=_params("arbitrary"),
        name="ssm",
    )(u_tm, bbd, lam, cbd, d_row, wglu, bglu_row)


def _memkv_kernel(mem_ref, w_ref, k_ref, v_ref):
    kv = _dot(mem_ref[0].astype(BF16), w_ref[...])
    k_ref[0] = kv[:, :D_MODEL].astype(BF16)
    v_ref[0] = kv[:, D_MODEL:].astype(BF16)


def _memkv(mem, w_mkv):
    bsz, mlen, _ = mem.shape
    return pl.pallas_call(
        _memkv_kernel,
        grid=(bsz,),
        in_specs=[pl.BlockSpec((1, mlen, D_MODEL), lambda b: (b, 0, 0)),
                  pl.BlockSpec(w_mkv.shape, lambda b: (0, 0))],
        out_specs=[pl.BlockSpec((1, mlen, D_MODEL), lambda b: (b, 0, 0))] * 2,
        out_shape=[jax.ShapeDtypeStruct((bsz, mlen, D_MODEL), BF16)] * 2,
        compiler_params=_params("parallel"),
        name="mem_kv",
    )(mem, w_mkv)


def _mix_mem_kernel(ya_ref, ys_ref, yb_ref, x_ref, gmix_ref, wout_ref, ln1_ref,
                    mk_ref, mv_ref, wmq_ref, wmo_ref, ln2_ref, xt_ref,
                    x1_ref, xq_ref, s_ref, p_ref, o_ref, *, alpha, halves):
    gm = gmix_ref[...]
    c1, c2 = MLA_QK_W, MLA_QK_W + SSM_WIDTH
    rows = x_ref.shape[1] // halves
    parts = [slice(i * rows, (i + 1) * rows) for i in range(halves)]
    for r in parts:
        ya = (_rms_rows(ya_ref[0, r, :].astype(F32), MLA_V_W) * gm[:, 0:c1]).astype(BF16)
        ys = (_rms_rows(ys_ref[r, :].astype(F32), SSM_WIDTH) * gm[:, c1:c2]).astype(BF16)
        yb = (_rms_rows(yb_ref[0, r, :].astype(F32), SB_WIDTH) * gm[:, c2:]).astype(BF16)
        x1_ref[r, :] = _dot(ya, wout_ref[0:c1, :]) + _dot(ys, wout_ref[c1:c2, :]) + _dot(yb, wout_ref[c2:, :])
    for r in parts:
        x1 = _layer_norm_rows(alpha * x_ref[0, r, :] + x1_ref[r, :], ln1_ref[0:1, :], ln1_ref[1:2, :])
        x1_ref[r, :] = x1
        xq_ref[r, :] = x1.astype(BF16)
    for r in parts:
        xq_ref[r, :] = _dot(xq_ref[r, :], wmq_ref[...]).astype(BF16)
    for i, r in enumerate(parts):
        for h in range(MEM_HEADS):
            cols = slice(h * MEM_HEAD_DIM, (h + 1) * MEM_HEAD_DIM)
            s_ref[i * MEM_HEADS + h] = _dot_nt(xq_ref[r, cols], mk_ref[0, :, cols])
    for i in range(halves * MEM_HEADS):
        s = s_ref[i]
        p = jnp.exp(s - jnp.max(s, axis=-1, keepdims=True))
        p_ref[i] = (p / jnp.sum(p, axis=-1, keepdims=True)).astype(BF16)
    for i, r in enumerate(parts):
        for h in range(MEM_HEADS):
            cols = slice(h * MEM_HEAD_DIM, (h + 1) * MEM_HEAD_DIM)
            o_ref[r, cols] = _dot(p_ref[i * MEM_HEADS + h], mv_ref[0, :, cols]).astype(BF16)
    for r in parts:
        x2 = _layer_norm_rows(alpha * x1_ref[r, :] + _dot(o_ref[r, :], wmo_ref[...]), ln2_ref[0:1, :], ln2_ref[1:2, :])
        xt_ref[:, r] = x2.T


def _mix_mem(ya, ys_tm, yb, x, gmix, wout, ln1, mk, mv, wmq, wmo, ln2, alpha):
    bsz, seq, _ = x.shape
    ts = min(seq, 512)
    nst = seq // ts
    halves = 2
    mlen = mk.shape[1]
    tok = lambda w: pl.BlockSpec((1, ts, w), lambda b, s: (b, s, 0))
    full = lambda a: pl.BlockSpec(a.shape, lambda b, s: (0,) * a.ndim)
    return pl.pallas_call(
        functools.partial(_mix_mem_kernel, alpha=alpha, halves=halves),
        grid=(bsz, nst),
        in_specs=[tok(MLA_QK_W), pl.BlockSpec((ts, SSM_WIDTH), lambda b, s: (s, b)), tok(SB_WIDTH), tok(D_MODEL),
                  full(gmix), full(wout), full(ln1),
                  pl.BlockSpec((1, mlen, D_MODEL), lambda b, s: (b, 0, 0)),
                  pl.BlockSpec((1, mlen, D_MODEL), lambda b, s: (b, 0, 0)),
                  full(wmq), full(wmo), full(ln2)],
        out_specs=pl.BlockSpec((D_MODEL, ts), lambda b, s: (0, b * nst + s)),
        out_shape=jax.ShapeDtypeStruct((D_MODEL, bsz * seq), F32),
        scratch_shapes=[pltpu.VMEM((ts, D_MODEL), F32), pltpu.VMEM((ts, D_MODEL), BF16),
                        pltpu.VMEM((halves * MEM_HEADS, ts // halves, mlen), F32),
                        pltpu.VMEM((halves * MEM_HEADS, ts // halves, mlen), BF16),
                        pltpu.VMEM((ts, D_MODEL), BF16)],
        compiler_params=_params("parallel", "parallel"),
        name="mix_mem",
    )(ya, ys_tm, yb, x, gmix, wout, ln1, mk, mv, wmq, wmo, ln2)


_SORT16 = ((0, 1), (2, 3), (0, 2), (1, 3), (1, 2), (4, 5), (6, 7), (4, 6), (5, 7), (5, 6), (0, 4), (2, 6), (2, 4),
           (1, 5), (3, 7), (3, 5), (1, 2), (3, 4), (5, 6), (8, 9), (10, 11), (8, 10), (9, 11), (9, 10), (12, 13),
           (14, 15), (12, 14), (13, 15), (13, 14), (8, 12), (10, 14), (10, 12), (9, 13), (11, 15), (11, 13), (9, 10),
           (11, 12), (13, 14), (0, 8), (4, 12), (4, 8), (2, 10), (6, 14), (6, 10), (2, 4), (6, 8), (10, 12), (1, 9),
           (5, 13), (5, 9), (3, 11), (7, 15), (7, 11), (3, 5), (7, 9), (11, 13), (1, 2), (3, 4), (5, 6), (7, 8),
           (9, 10), (11, 12), (13, 14))


def _sort16_desc(v):
    v = list(v)
    for i, j in _SORT16:
        v[i], v[j] = jnp.maximum(v[i], v[j]), jnp.minimum(v[i], v[j])
    return v


def _merge_bitonic_desc(v):
    v = list(v)
    for j in (8, 4, 2, 1):
        for i in range(16):
            l = i ^ j
            if l > i:
                v[i], v[l] = jnp.maximum(v[i], v[l]), jnp.minimum(v[i], v[l])
    return v


def _top16_sorted(rows, sort_result=True):
    v = _sort16_desc(rows)
    for shift in (4, 2, 1):
        partner = [pltpu.roll(x, shift, axis=0) for x in v]
        v = [jnp.maximum(v[i], partner[15 - i]) for i in range(16)]
        if sort_result or shift != 1:
            v = _merge_bitonic_desc(v)
    return v


def _count_prefix(pred, v):
    p8 = pred(v[7])
    p4 = pred(jnp.where(p8, v[11], v[3]))
    p2 = pred(jnp.where(p8, jnp.where(p4, v[13], v[9]), jnp.where(p4, v[5], v[1])))
    p1 = pred(jnp.where(p8, jnp.where(p4, jnp.where(p2, v[14], v[12]), jnp.where(p2, v[10], v[8])),
                        jnp.where(p4, jnp.where(p2, v[6], v[4]), jnp.where(p2, v[2], v[0]))))
    count = (jnp.where(p8, 8.0, 0.0) + jnp.where(p4, 4.0, 0.0)) + (jnp.where(p2, 2.0, 0.0) + jnp.where(p1, 1.0, 0.0))
    return jnp.where(pred(v[15]), 16.0, count)


def _peer_route_kernel(xt_ref, wpq_ref, keys_ref, a_ref, n_ref, b_ref, r_ref, q_ref, s1_ref, s2_ref):
    q_ref[...] = _dot(wpq_ref[...], xt_ref[...].astype(BF16)).astype(BF16)
    lax.fori_loop(0, PEER_HEADS, functools.partial(_peer_route_head, q_ref, keys_ref, a_ref, n_ref, b_ref, r_ref,
                                                   s1_ref, s2_ref), 0)


def _peer_route_head(q_ref, keys_ref, a_ref, n_ref, b_ref, r_ref, s1_ref, s2_ref, h, carry):
    q0 = pl.multiple_of(h * (2 * PEER_KEY_DIM), 2 * PEER_KEY_DIM)
    s1_ref[...] = _dot(keys_ref[h, 0], q_ref[pl.ds(q0, PEER_KEY_DIM), :])
    s2_ref[...] = _dot(keys_ref[h, 1], q_ref[pl.ds(q0 + PEER_KEY_DIM, PEER_KEY_DIM), :])
    sub = lax.broadcasted_iota(jnp.int32, (8, LANE), 0)
    groups = PEER_N_KEYS // 8

    def spread(vals):
        out = vals[7]
        for j in range(6, -1, -1):
            out = jnp.where(sub == j, vals[j], out)
        return out

    def chunk(c, carry):
        ln = pl.ds(pl.multiple_of(c * LANE, LANE), LANE)
        rows1 = [s1_ref[8 * i:8 * i + 8, ln] for i in range(groups)]
        rows2 = [s2_ref[8 * i:8 * i + 8, ln] for i in range(groups)]
        v1 = _top16_sorted(rows1)
        v2 = _top16_sorted(rows2)
        v2lo, v2hi, v1hi = spread(v2[:8]), spread(v2[8:]), spread(v1[8:])
        cands = ([v1[0] + v2lo, v1[0] + v2hi] + [v1[r] + v2lo for r in range(1, 8)] + [v1hi + v2[0]]
                 + [v1[r] + v2hi for r in range(1, 7)])
        top = _top16_sorted(cands, sort_result=False)
        thr = functools.reduce(jnp.minimum, top)
        cmax = v1[0] + v2[0]
        inv_z = 1.0 / functools.reduce(jnp.add, [jnp.exp(t - cmax) for t in top])
        for i in range(groups):
            s = rows1[i]
            kept = _count_prefix(lambda t, s=s: s + t >= thr, v2)
            in_top = s >= v1[15]
            a_ref[i, h, :, ln] = jnp.where(in_top, jnp.exp(s - v1[0]), 0.0)
            n_ref[i, h, :, ln] = jnp.where(in_top, kept, 0.0)
        for i in range(groups // 2):
            bs, rs = [], []
            for s in (rows2[2 * i], rows2[2 * i + 1]):
                rank = _count_prefix(lambda t, s=s: t > s, v2)
                rs.append(rank)
                bs.append(jnp.where(rank < float(PEER_TOPK), jnp.exp(s - v2[0]) * inv_z, 0.0))
            b_ref[h, 16 * i:16 * i + 16, ln] = jnp.concatenate(bs, axis=0).astype(BF16)
            r_ref[h, 16 * i:16 * i + 16, ln] = jnp.concatenate(rs, axis=0).astype(BF16)
        return carry

    lax.fori_loop(0, s1_ref.shape[1] // LANE, chunk, 0)
    return carry


def _peer_route(xt, wpq_t, keys):
    ntok = xt.shape[1]
    tt = min(ntok, 512)
    nblk = PEER_N_KEYS // 8
    return pl.pallas_call(
        _peer_route_kernel,
        grid=(ntok // tt,),
        in_specs=[pl.BlockSpec((D_MODEL, tt), lambda i: (0, i)),
                  pl.BlockSpec(wpq_t.shape, lambda i: (0, 0)),
                  pl.BlockSpec(keys.shape, lambda i: (0, 0, 0, 0))],
        out_specs=[pl.BlockSpec((nblk, PEER_HEADS, 8, tt), lambda i: (0, 0, 0, i)),
                   pl.BlockSpec((nblk, PEER_HEADS, 8, tt), lambda i: (0, 0, 0, i)),
                   pl.BlockSpec((PEER_HEADS, PEER_N_KEYS, tt), lambda i: (0, 0, i)),
                   pl.BlockSpec((PEER_HEADS, PEER_N_KEYS, tt), lambda i: (0, 0, i))],
        out_shape=[jax.ShapeDtypeStruct((nblk, PEER_HEADS, 8, ntok), F32),
                   jax.ShapeDtypeStruct((nblk, PEER_HEADS, 8, ntok), F32),
                   jax.ShapeDtypeStruct((PEER_HEADS, PEER_N_KEYS, ntok), BF16),
                   jax.ShapeDtypeStruct((PEER_HEADS, PEER_N_KEYS, ntok), BF16)],
        scratch_shapes=[pltpu.VMEM((PEER_HEADS * 2 * PEER_KEY_DIM, tt), BF16),
                        pltpu.VMEM((PEER_N_KEYS, tt), F32), pltpu.VMEM((PEER_N_KEYS, tt), F32)],
        compiler_params=_params("parallel"),
        name="peer_route",
    )(xt, wpq_t, keys)


PEER_I1_PER_TILE = 8
PEER_TILE = PEER_I1_PER_TILE * PEER_N_KEYS


def _peer_dense_kernel(xt_ref, a_ref, n_ref, b_ref, r_ref, u_ref, vt_ref, ln_ref, o_ref,
                       acc_ref, xb_ref, gh_ref, *, alpha):
    e = pl.program_id(1)

    @pl.when(e == 0)
    def _():
        acc_ref[...] = jnp.zeros_like(acc_ref)
        xb_ref[...] = xt_ref[...].astype(BF16)

    tt = xb_ref.shape[1]
    rows = BF16_SUBLANES
    gh_ref[...] = _dot(u_ref[...], xb_ref[...]).astype(BF16)
    cw = min(tt, 2 * LANE)
    for j in range(PEER_I1_PER_TILE):
        for c in range(tt // cw):
            ln = slice(c * cw, (c + 1) * cw)
            a_rows = [jnp.broadcast_to(a_ref[0, h, j:j + 1, ln], (rows, cw)).astype(BF16) for h in range(PEER_HEADS)]
            n_rows = [jnp.broadcast_to(n_ref[0, h, j:j + 1, ln], (rows, cw)).astype(BF16) for h in range(PEER_HEADS)]
            for g in range(PEER_N_KEYS // rows):
                i2 = slice(g * rows, (g + 1) * rows)
                gate = None
                for h in range(PEER_HEADS):
                    term = jnp.where(r_ref[h, i2, ln] < n_rows[h], b_ref[h, i2, ln] * a_rows[h], jnp.zeros((), BF16))
                    gate = term if gate is None else gate + term
                e0 = j * PEER_N_KEYS + g * rows
                gh_ref[e0:e0 + rows, ln] = gate * _gelu_tanh(gh_ref[e0:e0 + rows, ln])
    acc_ref[...] += _dot(vt_ref[...], gh_ref[...])

    @pl.when(e == pl.num_programs(1) - 1)
    def _():
        y = alpha * xt_ref[...] + acc_ref[...]
        mu = jnp.mean(y, axis=0, keepdims=True)
        c = y - mu
        var = jnp.mean(c * c, axis=0, keepdims=True)
        o_ref[...] = (c * lax.rsqrt(var + NORM_EPS)).T * ln_ref[0:1, :] + ln_ref[1:2, :]


def _peer_dense(xt, a, n, b, r, u, vt, ln, alpha):
    ntok = xt.shape[1]
    tt = min(ntok, 1024)
    ntile = PEER_EXPERTS // PEER_TILE
    return pl.pallas_call(
        functools.partial(_peer_dense_kernel, alpha=alpha),
        grid=(ntok // tt, ntile),
        in_specs=[pl.BlockSpec((D_MODEL, tt), lambda i, e: (0, i)),
                  pl.BlockSpec((1, PEER_HEADS, PEER_I1_PER_TILE, tt), lambda i, e: (e, 0, 0, i)),
                  pl.BlockSpec((1, PEER_HEADS, PEER_I1_PER_TILE, tt), lambda i, e: (e, 0, 0, i)),
                  pl.BlockSpec((PEER_HEADS, PEER_N_KEYS, tt), lambda i, e: (0, 0, i)),
                  pl.BlockSpec((PEER_HEADS, PEER_N_KEYS, tt), lambda i, e: (0, 0, i)),
                  pl.BlockSpec((PEER_TILE, D_MODEL), lambda i, e: (e, 0)),
                  pl.BlockSpec((D_MODEL, PEER_TILE), lambda i, e: (0, e)),
                  pl.BlockSpec(ln.shape, lambda i, e: (0, 0))],
        out_specs=pl.BlockSpec((tt, D_MODEL), lambda i, e: (i, 0)),
        out_shape=jax.ShapeDtypeStruct((ntok, D_MODEL), F32),
        scratch_shapes=[pltpu.VMEM((D_MODEL, tt), F32), pltpu.VMEM((D_MODEL, tt), BF16),
                        pltpu.VMEM((PEER_TILE, tt), BF16)],
        compiler_params=_params("parallel", "arbitrary"),
        name="peer_dense",
    )(xt, a, n, b, r, u, vt, ln)


def _pad_heads(w, heads, width):
    rows = w.shape[0]
    w = w.reshape(rows, heads, width)
    return jnp.pad(w, ((0, 0), (0, 0), (0, HEAD_PAD - width))).reshape(rows, heads * HEAD_PAD)


def _rotate_half_cols(w, heads, width, nope):
    rows = w.shape[0]
    w = w.reshape(rows, heads, width)
    half = (width - nope) // 2
    x1, x2 = w[..., nope:nope + half], w[..., nope + half:]
    out = jnp.concatenate([jnp.zeros_like(w[..., :nope]), -x2, x1], axis=-1)
    return out.reshape(rows, heads * width)


def _pack_inproj(w_in, g_cq, g_ckv, w_uq, w_ukv):
    c = 0
    cols = {}
    for name, width in (("cq", MLA_Q_RANK), ("ckv", MLA_KV_RANK), ("kr", MLA_ROPE), ("ssm", SSM_WIDTH),
                        ("qsb", SB_WIDTH), ("ksb", SB_WIDTH), ("vsb", SB_WIDTH)):
        cols[name] = w_in[:, c:c + width]
        c += width
    rows = w_in.shape[0]
    zeros = lambda n: jnp.zeros((rows, n), F32)
    kr = cols["kr"]
    half = MLA_ROPE // 2
    kr_plain = jnp.concatenate([zeros(MLA_NOPE), kr, zeros(HEAD_PAD - MLA_NOPE - MLA_ROPE)], axis=1)
    kr_swap = jnp.concatenate([zeros(MLA_NOPE), -kr[:, half:], kr[:, :half],
                               zeros(HEAD_PAD - MLA_NOPE - MLA_ROPE)], axis=1)
    sb_scale = SB_DIM ** -0.5 * LOG2E
    win = jnp.concatenate([cols["cq"], cols["ckv"], kr_plain, kr_swap, cols["ssm"],
                           _pad_heads(cols["qsb"] * sb_scale, SB_HEADS, SB_DIM),
                           _pad_heads(cols["ksb"], SB_HEADS, SB_DIM), cols["vsb"]], axis=1).astype(BF16)

    qk_dim = MLA_NOPE + MLA_ROPE
    wq = w_uq * (g_cq[:, None] * (qk_dim ** -0.5 * LOG2E))
    wuq = jnp.concatenate([_pad_heads(wq, MLA_HEADS, qk_dim),
                           _pad_heads(_rotate_half_cols(wq, MLA_HEADS, qk_dim, MLA_NOPE), MLA_HEADS, qk_dim)],
                          axis=1).astype(BF16)
    wkv = (w_ukv * g_ckv[:, None]).reshape(MLA_KV_RANK, MLA_HEADS, MLA_NOPE + MLA_V)
    wukv = jnp.concatenate([_pad_heads(wkv[..., :MLA_NOPE].reshape(MLA_KV_RANK, -1), MLA_HEADS, MLA_NOPE),
                            _pad_heads(wkv[..., MLA_NOPE:].reshape(MLA_KV_RANK, -1), MLA_HEADS, MLA_V)],
                           axis=1).astype(BF16)
    return win, wuq, wukv


def _pad_mla_rows(w):
    cols = w.shape[1]
    head_rows = jnp.pad(w[:MLA_V_W].reshape(MLA_HEADS, MLA_V, cols), ((0, 0), (0, HEAD_PAD - MLA_V), (0, 0)))
    return jnp.concatenate([head_rows.reshape(MLA_QK_W, cols), w[MLA_V_W:]], axis=0)


def _pack_ssm(lam_re, lam_im, log_step, b_re, b_im, c_re, c_im, d_skip):
    step = jnp.exp(log_step)[:, None]
    decay = jnp.exp(lam_re * step)
    ab_re, ab_im = decay * jnp.cos(lam_im * step), decay * jnp.sin(lam_im * step)
    inv = 1.0 / (lam_re * lam_re + lam_im * lam_im)
    f_re = ((ab_re - 1.0) * lam_re + ab_im * lam_im) * inv
    f_im = (ab_im * lam_re - (ab_re - 1.0) * lam_im) * inv
    bb_re = f_re[..., None] * b_re - f_im[..., None] * b_im
    bb_im = f_re[..., None] * b_im + f_im[..., None] * b_re
    per = SSM_GROUPS // SSM_BLOCKS
    eye = jnp.eye(per, dtype=F32)
    split = lambda w: w.reshape((SSM_BLOCKS, per) + w.shape[1:])
    blk = lambda w: jnp.einsum("bgph,gk->bghkp", split(w), eye).reshape(SSM_BLOCKS, SSM_BLOCK_CH, SSM_BLOCK_STATES)
    bbd = jnp.concatenate([blk(bb_re), blk(bb_im)], axis=2).astype(BF16)
    blk_c = lambda w: jnp.einsum("bghp,gk->bgpkh", split(w), eye).reshape(SSM_BLOCKS, SSM_BLOCK_STATES, SSM_BLOCK_CH)
    cbd = jnp.concatenate([blk_c(c_re), blk_c(-c_im)], axis=1).astype(BF16)
    lam = jnp.stack([ab_re.reshape(-1), ab_im.reshape(-1)])
    return bbd, lam, cbd, d_skip.reshape(1, SSM_WIDTH)


def kernel(x, mem, positions, w_in, g_cq, g_ckv, w_uq, w_ukv, ssm_lam_re, ssm_lam_im, ssm_log_step, ssm_b_re, ssm_b_im, ssm_c_re, ssm_c_im, ssm_d, w_glu, b_glu, g_mix, w_out, ln_mix_g, ln_mix_b, w_mq, w_mkv, w_mo, ln_mem_g, ln_mem_b, w_pq, peer_sub_keys, peer_u, peer_v, ln_ffn_g, ln_ffn_b):
    bsz, seq, _ = x.shape
    depth = w_in.shape[0]
    alpha = (2 * depth) ** 0.25
    cos_t, sin_t = _rope_tables(positions)
    for l in range(depth):
        win, wuq, wukv = _pack_inproj(w_in[l], g_cq[l], g_ckv[l], w_uq[l], w_ukv[l])
        q, k, v, u_tm, q_sb, k_sb, v_sb = _inproj(x, win, wuq, wukv, cos_t, sin_t)
        y_mla = _mla_attention(q, k, v)
        y_sb = _sb_attention(q_sb, k_sb, v_sb)
        bbd, lam, cbd, d_row = _pack_ssm(ssm_lam_re[l], ssm_lam_im[l], ssm_log_step[l], ssm_b_re[l], ssm_b_im[l],
                                         ssm_c_re[l], ssm_c_im[l], ssm_d[l])
        y_ssm = _ssm(u_tm.reshape(seq * bsz, SSM_WIDTH), bbd, lam, cbd, d_row,
                     w_glu[l].astype(BF16), b_glu[l].reshape(1, SSM_WIDTH), bsz)
        mk, mv = _memkv(mem, w_mkv[l].astype(BF16))
        xt = _mix_mem(y_mla, y_ssm.reshape(seq, bsz * SSM_WIDTH), y_sb, x,
                      _pad_mla_rows(g_mix[l][:, None]).reshape(1, -1), _pad_mla_rows(w_out[l]).astype(BF16),
                      jnp.stack([ln_mix_g[l], ln_mix_b[l]]),
                      mk, mv, (w_mq[l] * MEM_HEAD_DIM ** -0.5).astype(BF16), w_mo[l].astype(BF16),
                      jnp.stack([ln_mem_g[l], ln_mem_b[l]]), alpha)
        a, n, b, r = _peer_route(xt, w_pq[l].T.astype(BF16), peer_sub_keys[l].astype(BF16))
        x = _peer_dense(xt, a, n, b, r, peer_u[l].astype(BF16), peer_v[l].T.astype(BF16),
                        jnp.stack([ln_ffn_g[l], ln_ffn_b[l]]), alpha).reshape(bsz, seq, D_MODEL)
    return x
```

```python
import functools
import math

import jax
import jax.numpy as jnp
from jax import lax
from jax.experimental import pallas as pl
from jax.experimental.pallas import tpu as pltpu

F32 = jnp.float32
BF16 = jnp.bfloat16

D_MODEL = 1024
CHUNK = 64
NORM_EPS = 1e-5
NEG_BIG = -1e30
LOG2E = math.log2(math.e)

MLA_HEADS = 6
MLA_NOPE = 64
MLA_ROPE = 32
MLA_V = 64
MLA_Q_RANK = 256
MLA_KV_RANK = 128
ROPE_THETA = 10000.0

SSM_GROUPS = 24
SSM_CH = 16
SSM_STATE = 64
SSM_WIDTH = SSM_GROUPS * SSM_CH
SSM_STATES = SSM_GROUPS * SSM_STATE
SSM_BLOCKS = 3
SSM_BLOCK_CH = SSM_WIDTH // SSM_BLOCKS
SSM_BLOCK_STATES = SSM_STATES // SSM_BLOCKS

SB_HEADS = 4
SB_DIM = 64
SB_WIDTH = SB_HEADS * SB_DIM

MEM_HEADS = 4
MEM_HEAD_DIM = D_MODEL // MEM_HEADS

PEER_HEADS = 8
PEER_N_KEYS = 128
PEER_TOPK = 16
PEER_KEY_DIM = 128
PEER_EXPERTS = PEER_N_KEYS * PEER_N_KEYS

LANE = 128
BF16_SUBLANES = 16
HEAD_PAD = 128
ATTN_TQ = 512
ATTN_TK = 256
SB_SCAN = 128

VMEM_LIMIT = 56 * 1024 * 1024

_C_CQ = 0
_C_CKV = _C_CQ + MLA_Q_RANK
_C_KRP = _C_CKV + MLA_KV_RANK
_C_KRS = _C_KRP + HEAD_PAD
_C_SSM = _C_KRS + HEAD_PAD
_C_QSB = _C_SSM + SSM_WIDTH
_C_KSB = _C_QSB + SB_HEADS * HEAD_PAD
_C_VSB = _C_KSB + SB_HEADS * HEAD_PAD
_C_END = _C_VSB + SB_WIDTH
MLA_QK_W = MLA_HEADS * HEAD_PAD
MLA_V_W = MLA_HEADS * MLA_V
MLA_V_ROWS = MLA_V + BF16_SUBLANES


def _params(*sem):
    return pltpu.CompilerParams(dimension_semantics=sem, vmem_limit_bytes=VMEM_LIMIT)


def _dot(a, b):
    return jnp.dot(a, b, preferred_element_type=F32)


def _dot_nt(a, b):
    return lax.dot_general(a, b, (((1,), (1,)), ((), ())), preferred_element_type=F32)


def _gelu_tanh(x):
    c = math.sqrt(2.0 / math.pi)
    inner = x * (c + (c * 0.044715) * (x * x))
    return (0.5 * x) * (1.0 + jnp.tanh(inner))


def _layer_norm_rows(v, g, b):
    mu = jnp.mean(v, axis=-1, keepdims=True)
    c = v - mu
    var = jnp.mean(c * c, axis=-1, keepdims=True)
    return c * lax.rsqrt(var + NORM_EPS) * g + b


def _rms_rows(v, width):
    return v * lax.rsqrt(jnp.sum(v * v, axis=-1, keepdims=True) * (1.0 / width) + NORM_EPS)


def _rope_kernel(pos_ref, freq_ref, cos_ref, sin_ref):
    ang = pos_ref[0].astype(F32) * freq_ref[...]
    lane = lax.broadcasted_iota(jnp.int32, ang.shape, 1)
    rot = (lane >= MLA_NOPE) & (lane < MLA_NOPE + MLA_ROPE)
    cos_ref[0] = jnp.where(rot, jnp.cos(ang), jnp.where(lane < MLA_NOPE, 1.0, 0.0))
    sin_ref[0] = jnp.where(rot, jnp.sin(ang), 0.0)


def _rope_tables(positions):
    bsz, seq = positions.shape
    ts = min(seq, 512)
    half = MLA_ROPE // 2
    freq = ROPE_THETA ** (-jnp.arange(half, dtype=F32) / half)
    freq_row = jnp.zeros((1, HEAD_PAD), F32).at[0, MLA_NOPE:MLA_NOPE + MLA_ROPE].set(jnp.tile(freq, 2))
    return pl.pallas_call(
        _rope_kernel,
        grid=(bsz, seq // ts),
        in_specs=[pl.BlockSpec((1, ts, 1), lambda b, s: (b, s, 0)),
                  pl.BlockSpec((1, HEAD_PAD), lambda b, s: (0, 0))],
        out_specs=[pl.BlockSpec((1, ts, HEAD_PAD), lambda b, s: (b, s, 0))] * 2,
        out_shape=[jax.ShapeDtypeStruct((bsz, seq, HEAD_PAD), F32)] * 2,
        compiler_params=_params("parallel", "parallel"),
        name="rope_tables",
    )(positions.reshape(bsz, seq, 1), freq_row)


def _inproj_kernel(x_ref, win_ref, wuq_ref, wukv_ref, cos_ref, sin_ref, ones_ref,
                   q_ref, k_ref, v_ref, u_ref, qs_ref, ks_ref, vs_ref):
    xb = x_ref[0].astype(BF16)
    acc = _dot(xb, win_ref[...])
    cos = cos_ref[0]
    sin = sin_ref[0]
    cqn = _rms_rows(acc[:, _C_CQ:_C_CKV], MLA_Q_RANK).astype(BF16)
    qq = _dot(cqn, wuq_ref[...])
    ckvn = _rms_rows(acc[:, _C_CKV:_C_KRP], MLA_KV_RANK).astype(BF16)
    kv = _dot(ckvn, wukv_ref[...])
    k_rope = acc[:, _C_KRP:_C_KRS] * cos + acc[:, _C_KRS:_C_SSM] * sin
    for h in range(MLA_HEADS):
        lo, hi = h * HEAD_PAD, (h + 1) * HEAD_PAD
        q_ref[0, lo:hi, :] = (qq[:, lo:hi] * cos + qq[:, MLA_QK_W + lo:MLA_QK_W + hi] * sin).T.astype(BF16)
        k_ref[0, :, lo:hi] = (kv[:, lo:hi] + k_rope).astype(BF16)
    v_ref[0] = (kv[:, MLA_QK_W:] + ones_ref[...]).T.astype(BF16)
    u_ref[...] = acc[:, _C_SSM:_C_QSB].astype(BF16)
    qs_ref[0] = acc[:, _C_QSB:_C_KSB].T.astype(BF16)
    ks_ref[0] = acc[:, _C_KSB:_C_VSB].astype(BF16)
    vs_ref[0] = acc[:, _C_VSB:_C_END].T.astype(BF16)


def _inproj(x, win, wuq, wukv, cos_t, sin_t):
    bsz, seq, _ = x.shape
    ts = min(seq, 512)
    tok = lambda w: pl.BlockSpec((1, ts, w), lambda b, s: (b, s, 0))
    tok_t = lambda w: pl.BlockSpec((1, w, ts), lambda b, s: (b, 0, s))
    full = lambda a: pl.BlockSpec(a.shape, lambda b, s: (0,) * a.ndim)
    sb_w = SB_HEADS * HEAD_PAD
    lane = jnp.arange(MLA_QK_W) % HEAD_PAD
    ones_row = (lane == MLA_V).astype(F32).reshape(1, MLA_QK_W)
    return pl.pallas_call(
        _inproj_kernel,
        grid=(bsz, seq // ts),
        in_specs=[tok(D_MODEL), full(win), full(wuq), full(wukv), tok(HEAD_PAD), tok(HEAD_PAD), full(ones_row)],
        out_specs=[tok_t(MLA_QK_W), tok(MLA_QK_W), tok_t(MLA_QK_W),
                   pl.BlockSpec((ts, SSM_WIDTH), lambda b, s: (s, b)),
                   tok_t(sb_w), tok(sb_w), tok_t(SB_WIDTH)],
        out_shape=[jax.ShapeDtypeStruct((bsz, MLA_QK_W, seq), BF16),
                   jax.ShapeDtypeStruct((bsz, seq, MLA_QK_W), BF16),
                   jax.ShapeDtypeStruct((bsz, MLA_QK_W, seq), BF16),
                   jax.ShapeDtypeStruct((seq, bsz * SSM_WIDTH), BF16),
                   jax.ShapeDtypeStruct((bsz, sb_w, seq), BF16),
                   jax.ShapeDtypeStruct((bsz, seq, sb_w), BF16),
                   jax.ShapeDtypeStruct((bsz, SB_WIDTH, seq), BF16)],
        compiler_params=_params("parallel", "parallel"),
        name="inproj",
    )(x, win, wuq, wukv, cos_t, sin_t, ones_row)


def _mla_kernel(q_ref, k_ref, vt_ref, o_ref, m_ref, acc_ref, s2_ref, p2_ref, *, tq, tk):
    qi = pl.program_id(1)
    per = tq // tk
    query_chunk = (qi * tq + lax.broadcasted_iota(jnp.int32, (tk, tq), 1)) // CHUNK
    key_in_tile = lax.broadcasted_iota(jnp.int32, (tk, tq), 0)
    m_ref[...] = jnp.full(m_ref.shape, NEG_BIG, F32)
    acc_ref[...] = jnp.zeros(acc_ref.shape, F32)

    def block(kb, masked, slot=0):
        ks = pl.multiple_of(kb * tk, tk)
        s_ref, p_ref = s2_ref.at[slot], p2_ref.at[slot]
        for h in range(MLA_HEADS):
            tile = slice(h * HEAD_PAD, (h + 1) * HEAD_PAD)
            s_ref[h] = _dot(k_ref[0, pl.ds(ks, tk), tile], q_ref[0, tile, :])
        if masked:
            allowed = (ks + key_in_tile) // CHUNK <= query_chunk
        alphas = []
        for h in range(MLA_HEADS):
            s = s_ref[h]
            if masked:
                s = jnp.where(allowed, s, NEG_BIG)
            m_old = m_ref[h]
            m_new = jnp.maximum(m_old, jnp.max(s, axis=0, keepdims=True))
            p_ref[h] = jnp.exp2(s - m_new).astype(BF16)
            alphas.append(jnp.exp2(m_old - m_new))
            m_ref[h] = m_new
        for h in range(MLA_HEADS):
            tile = slice(h * HEAD_PAD, (h + 1) * HEAD_PAD)
            used = slice(h * HEAD_PAD, h * HEAD_PAD + MLA_V_ROWS)
            acc_ref[h] = alphas[h] * acc_ref[h] + _dot(vt_ref[0, used, pl.ds(ks, tk)], p_ref[h])

    def body(i, carry):
        for d in range(per):
            block(i * per + d, False, d)
        return carry

    lax.fori_loop(0, qi, body, 0)
    for d in range(per):
        block(qi * per + d, True, d)
    pad = jnp.zeros((HEAD_PAD - MLA_V, tq), F32)
    for h in range(MLA_HEADS):
        acc = acc_ref[h]
        out_t = jnp.concatenate([acc[0:MLA_V, :] / acc[MLA_V:MLA_V + 1, :], pad], axis=0)
        o_ref[0, :, h * HEAD_PAD:(h + 1) * HEAD_PAD] = out_t.T.astype(BF16)


def _mla_attention(qt, k, vt):
    bsz, seq, _ = k.shape
    tq, tk = min(seq, ATTN_TQ), min(seq, ATTN_TK)
    return pl.pallas_call(
        functools.partial(_mla_kernel, tq=tq, tk=tk),
        grid=(bsz, seq // tq),
        in_specs=[pl.BlockSpec((1, MLA_QK_W, tq), lambda b, i: (b, 0, i)),
                  pl.BlockSpec((1, seq, MLA_QK_W), lambda b, i: (b, 0, 0)),
                  pl.BlockSpec((1, MLA_QK_W, seq), lambda b, i: (b, 0, 0))],
        out_specs=pl.BlockSpec((1, tq, MLA_QK_W), lambda b, i: (b, i, 0)),
        out_shape=jax.ShapeDtypeStruct((bsz, seq, MLA_QK_W), BF16),
        scratch_shapes=[pltpu.VMEM((MLA_HEADS, 1, tq), F32), pltpu.VMEM((MLA_HEADS, MLA_V_ROWS, tq), F32),
                        pltpu.VMEM((tq // tk, MLA_HEADS, tk, tq), F32), pltpu.VMEM((tq // tk, MLA_HEADS, tk, tq), BF16)],
        compiler_params=_params("parallel", "arbitrary"),
        name="mla_attention",
    )(qt, k, vt)


def _sb_kernel(q_ref, k_ref, vt_ref, tri_ref, o_ref, right_ref, acc_ref, z2_ref, split2_ref, sum2_ref, w2_ref, *, tq, tk):
    qi = pl.program_id(1)
    per = tq // tk
    scan = min(tk, SB_SCAN)
    nscan = tk // scan
    query_pos = qi * tq + lax.broadcasted_iota(jnp.int32, (tk, tq), 1)
    key_in_tile = lax.broadcasted_iota(jnp.int32, (tk, tq), 0)
    right_ref[...] = jnp.zeros(right_ref.shape, F32)
    acc_ref[...] = jnp.zeros(acc_ref.shape, F32)

    def block(kb, masked, slot=0):
        ks = pl.multiple_of(kb * tk, tk)
        z_ref, split_ref, sum_ref, w_ref = z2_ref.at[slot], split2_ref.at[slot], sum2_ref.at[slot], w2_ref.at[slot]
        for h in range(SB_HEADS):
            tile = slice(h * HEAD_PAD, (h + 1) * HEAD_PAD)
            z_ref[h] = _dot(k_ref[0, pl.ds(ks, tk), tile], q_ref[0, tile, :])
        if masked:
            earlier = ks + key_in_tile < query_pos
        for h in range(SB_HEADS):
            z = z_ref[h]
            neg_abs = pltpu.bitcast(pltpu.bitcast(z, jnp.uint32) | jnp.uint32(0x80000000), F32)
            fail = jnp.maximum(z, 0.0) + jnp.log2(1.0 + jnp.exp2(neg_abs))
            z_ref[h] = z - fail
            if masked:
                fail = jnp.where(earlier, fail, 0.0)
            for u in range(nscan):
                part = fail[u * scan:(u + 1) * scan, :]
                hi = part.astype(BF16)
                split_ref[h, 2 * u * scan:(2 * u + 1) * scan, :] = hi
                split_ref[h, (2 * u + 1) * scan:(2 * u + 2) * scan, :] = (part - hi.astype(F32)).astype(BF16)
                sum_ref[h, u] = jnp.sum(part, axis=0, keepdims=True)
        for h in range(SB_HEADS):
            after = right_ref[h]
            for u in range(nscan - 1, -1, -1):
                rows = slice(u * scan, (u + 1) * scan)
                between = _dot(tri_ref[...], split_ref[h, 2 * u * scan:(2 * u + 2) * scan, :]) + after
                w = jnp.exp2(z_ref[h, rows, :] - between)
                if masked:
                    w = jnp.where(earlier[rows, :], w, 0.0)
                w_ref[h, rows, :] = w.astype(BF16)
                after = after + sum_ref[h, u]
            right_ref[h] = after
        for h in range(SB_HEADS):
            pair = slice((h // 2) * HEAD_PAD, (h // 2 + 1) * HEAD_PAD)
            acc_ref[h] += _dot(vt_ref[0, pair, pl.ds(ks, tk)], w_ref[h])

    def body(i, carry):
        for d in range(per):
            block((qi - i) * per - 1 - d, False, d)
        return carry

    for d in range(per - 1, -1, -1):
        block(qi * per + d, True, d)
    lax.fori_loop(0, qi, body, 0)
    row = lax.broadcasted_iota(jnp.int32, (HEAD_PAD, tq), 0)
    for pair in range(SB_HEADS // 2):
        out_t = jnp.where(row < SB_DIM, acc_ref[2 * pair], acc_ref[2 * pair + 1])
        o_ref[0, :, pair * HEAD_PAD:(pair + 1) * HEAD_PAD] = out_t.T.astype(BF16)


def _sb_attention(qt, k, vt):
    bsz, seq, _ = k.shape
    tq, tk = min(seq, ATTN_TQ), min(seq, ATTN_TK)
    sb_w = SB_HEADS * HEAD_PAD
    scan = min(tk, SB_SCAN)
    tri = (lax.broadcasted_iota(jnp.int32, (scan, scan), 1) > lax.broadcasted_iota(jnp.int32, (scan, scan), 0)).astype(BF16)
    tri2 = jnp.concatenate([tri, tri], axis=1)
    return pl.pallas_call(
        functools.partial(_sb_kernel, tq=tq, tk=tk),
        grid=(bsz, seq // tq),
        in_specs=[pl.BlockSpec((1, sb_w, tq), lambda b, i: (b, 0, i)),
                  pl.BlockSpec((1, seq, sb_w), lambda b, i: (b, 0, 0)),
                  pl.BlockSpec((1, SB_WIDTH, seq), lambda b, i: (b, 0, 0)),
                  pl.BlockSpec((scan, 2 * scan), lambda b, i: (0, 0))],
        out_specs=pl.BlockSpec((1, tq, SB_WIDTH), lambda b, i: (b, i, 0)),
        out_shape=jax.ShapeDtypeStruct((bsz, seq, SB_WIDTH), BF16),
        scratch_shapes=[pltpu.VMEM((SB_HEADS, 1, tq), F32), pltpu.VMEM((SB_HEADS, HEAD_PAD, tq), F32),
                        pltpu.VMEM((tq // tk, SB_HEADS, tk, tq), F32), pltpu.VMEM((tq // tk, SB_HEADS, 2 * tk, tq), BF16),
                        pltpu.VMEM((tq // tk, SB_HEADS, tk // scan, 1, tq), F32),
                        pltpu.VMEM((tq // tk, SB_HEADS, tk, tq), BF16)],
        compiler_params=_params("parallel", "arbitrary"),
        name="sb_attention",
    )(qt, k, vt, tri2)


def _ssm_kernel(u_ref, bbd_ref, lam_ref, cbd_ref, d_ref, wglu_ref, bglu_ref, y_ref, h_ref, hs_ref, rows_ref, *, tt, nb):
    @pl.when(pl.program_id(0) == 0)
    def _():
        h_ref[...] = jnp.zeros_like(h_ref)

    tiles = SSM_WIDTH // LANE
    for b in range(nb):
        for c in range(tiles):
            lanes = slice(b * SSM_WIDTH + c * LANE, b * SSM_WIDTH + (c + 1) * LANE)
            rows_ref[c, pl.ds(b, tt, stride=nb), :] = u_ref[:, lanes].astype(F32)
    u = jnp.concatenate([rows_ref[c] for c in range(tiles)], axis=1).astype(BF16)
    half, wide = SSM_BLOCK_STATES, 2 * SSM_BLOCK_STATES
    for b in range(SSM_BLOCKS):
        hs_ref[:, b * wide:(b + 1) * wide] = _dot(u[:, b * SSM_BLOCK_CH:(b + 1) * SSM_BLOCK_CH], bbd_ref[b])
    lam_re = jnp.broadcast_to(lam_ref[0:1, :], (nb, SSM_STATES))
    lam_im = jnp.broadcast_to(lam_ref[1:2, :], (nb, SSM_STATES))

    def step(t, carry):
        r0 = pl.multiple_of(t * nb, nb)
        out = []
        for b in range(SSM_BLOCKS):
            h_re, h_im = carry[b]
            l_re, l_im = lam_re[:, b * half:(b + 1) * half], lam_im[:, b * half:(b + 1) * half]
            re_cols = slice(b * wide, b * wide + half)
            im_cols = slice(b * wide + half, (b + 1) * wide)
            n_re = l_re * h_re - l_im * h_im + hs_ref[pl.ds(r0, nb), re_cols]
            n_im = l_re * h_im + l_im * h_re + hs_ref[pl.ds(r0, nb), im_cols]
            hs_ref[pl.ds(r0, nb), re_cols] = n_re
            hs_ref[pl.ds(r0, nb), im_cols] = n_im
            out.append((n_re, n_im))
        return tuple(out)

    init = tuple((h_ref[:, b * wide:b * wide + half], h_ref[:, b * wide + half:(b + 1) * wide]) for b in range(SSM_BLOCKS))
    final = lax.fori_loop(0, tt, step, init, unroll=4)
    for b in range(SSM_BLOCKS):
        h_ref[:, b * wide:b * wide + half] = final[b][0]
        h_ref[:, b * wide + half:(b + 1) * wide] = final[b][1]

    y = jnp.concatenate([_dot(hs_ref[:, b * wide:(b + 1) * wide].astype(BF16), cbd_ref[b]) for b in range(SSM_BLOCKS)],
                        axis=1) + d_ref[...] * u.astype(F32)
    y = jax.nn.gelu(y)
    y = y * jax.nn.sigmoid(_dot(y.astype(BF16), wglu_ref[...]) + bglu_ref[...])
    for c in range(tiles):
        rows_ref[c] = y[:, c * LANE:(c + 1) * LANE]
    for b in range(nb):
        for c in range(tiles):
            lanes = slice(b * SSM_WIDTH + c * LANE, b * SSM_WIDTH + (c + 1) * LANE)
            y_ref[:, lanes] = rows_ref[c, pl.ds(b, tt, stride=nb), :].astype(BF16)


def _ssm(u_tm, bbd, lam, cbd, d_row, wglu, bglu_row, nb):
    seq = u_tm.shape[0]
    tt = min(seq, 128)
    full = lambda a: pl.BlockSpec(a.shape, lambda i: (0,) * a.ndim)
    return pl.pallas_call(
        functools.partial(_ssm_kernel, tt=tt, nb=nb),
        grid=(seq // tt,),
        in_specs=[pl.BlockSpec((tt, nb * SSM_WIDTH), lambda i: (i, 0)),
                  full(bbd), full(lam), full(cbd), full(d_row), full(wglu), full(bglu_row)],
        out_specs=pl.BlockSpec((tt, nb * SSM_WIDTH), lambda i: (i, 0)),
        out_shape=jax.ShapeDtypeStruct((seq, nb * SSM_WIDTH), BF16),
        scratch_shapes=[pltpu.VMEM((nb, 2 * SSM_STATES), F32),
                        pltpu.VMEM((tt * nb, 2 * SSM_STATES), F32),
                        pltpu.VMEM((SSM_WIDTH // LANE, tt * nb, LANE), F32)],
        compiler_params=_params("arbitrary"),
        name="ssm",
    )(u_tm, bbd, lam, cbd, d_row, wglu, bglu_row)


def _memkv_kernel(mem_ref, w_ref, k_ref, v_ref):
    kv = _dot(mem_ref[0].astype(BF16), w_ref[...])
    k_ref[0] = kv[:, :D_MODEL].astype(BF16)
    v_ref[0] = kv[:, D_MODEL:].astype(BF16)


def _memkv(mem, w_mkv):
    bsz, mlen, _ = mem.shape
    return pl.pallas_call(
        _memkv_kernel,
        grid=(bsz,),
        in_specs=[pl.BlockSpec((1, mlen, D_MODEL), lambda b: (b, 0, 0)),
                  pl.BlockSpec(w_mkv.shape, lambda b: (0, 0))],
        out_specs=[pl.BlockSpec((1, mlen, D_MODEL), lambda b: (b, 0, 0))] * 2,
        out_shape=[jax.ShapeDtypeStruct((bsz, mlen, D_MODEL), BF16)] * 2,
        compiler_params=_params("parallel"),
        name="mem_kv",
    )(mem, w_mkv)


def _mix_mem_kernel(ya_ref, ys_ref, yb_ref, x_ref, gmix_ref, wout_ref, ln1_ref,
                    mk_ref, mv_ref, wmq_ref, wmo_ref, ln2_ref, xt_ref,
                    x1_ref, xq_ref, s_ref, p_ref, o_ref, *, alpha, halves):
    gm = gmix_ref[...]
    c1, c2 = MLA_QK_W, MLA_QK_W + SSM_WIDTH
    rows = x_ref.shape[1] // halves
    parts = [slice(i * rows, (i + 1) * rows) for i in range(halves)]
    for r in parts:
        ya = (_rms_rows(ya_ref[0, r, :].astype(F32), MLA_V_W) * gm[:, 0:c1]).astype(BF16)
        ys = (_rms_rows(ys_ref[r, :].astype(F32), SSM_WIDTH) * gm[:, c1:c2]).astype(BF16)
        yb = (_rms_rows(yb_ref[0, r, :].astype(F32), SB_WIDTH) * gm[:, c2:]).astype(BF16)
        x1_ref[r, :] = _dot(ya, wout_ref[0:c1, :]) + _dot(ys, wout_ref[c1:c2, :]) + _dot(yb, wout_ref[c2:, :])
    for r in parts:
        x1 = _layer_norm_rows(alpha * x_ref[0, r, :] + x1_ref[r, :], ln1_ref[0:1, :], ln1_ref[1:2, :])
        x1_ref[r, :] = x1
        xq_ref[r, :] = x1.astype(BF16)
    for r in parts:
        xq_ref[r, :] = _dot(xq_ref[r, :], wmq_ref[...]).astype(BF16)
    for i, r in enumerate(parts):
        for h in range(MEM_HEADS):
            cols = slice(h * MEM_HEAD_DIM, (h + 1) * MEM_HEAD_DIM)
            s_ref[i * MEM_HEADS + h] = _dot_nt(xq_ref[r, cols], mk_ref[0, :, cols])
    for i in range(halves * MEM_HEADS):
        s = s_ref[i]
        p = jnp.exp(s - jnp.max(s, axis=-1, keepdims=True))
        p_ref[i] = (p / jnp.sum(p, axis=-1, keepdims=True)).astype(BF16)
    for i, r in enumerate(parts):
        for h in range(MEM_HEADS):
            cols = slice(h * MEM_HEAD_DIM, (h + 1) * MEM_HEAD_DIM)
            o_ref[r, cols] = _dot(p_ref[i * MEM_HEADS + h], mv_ref[0, :, cols]).astype(BF16)
    for r in parts:
        x2 = _layer_norm_rows(alpha * x1_ref[r, :] + _dot(o_ref[r, :], wmo_ref[...]), ln2_ref[0:1, :], ln2_ref[1:2, :])
        xt_ref[:, r] = x2.T


def _mix_mem(ya, ys_tm, yb, x, gmix, wout, ln1, mk, mv, wmq, wmo, ln2, alpha):
    bsz, seq, _ = x.shape
    ts = min(seq, 512)
    nst = seq // ts
    halves = 2
    mlen = mk.shape[1]
    tok = lambda w: pl.BlockSpec((1, ts, w), lambda b, s: (b, s, 0))
    full = lambda a: pl.BlockSpec(a.shape, lambda b, s: (0,) * a.ndim)
    return pl.pallas_call(
        functools.partial(_mix_mem_kernel, alpha=alpha, halves=halves),
        grid=(bsz, nst),
        in_specs=[tok(MLA_QK_W), pl.BlockSpec((ts, SSM_WIDTH), lambda b, s: (s, b)), tok(SB_WIDTH), tok(D_MODEL),
                  full(gmix), full(wout), full(ln1),
                  pl.BlockSpec((1, mlen, D_MODEL), lambda b, s: (b, 0, 0)),
                  pl.BlockSpec((1, mlen, D_MODEL), lambda b, s: (b, 0, 0)),
                  full(wmq), full(wmo), full(ln2)],
        out_specs=pl.BlockSpec((D_MODEL, ts), lambda b, s: (0, b * nst + s)),
        out_shape=jax.ShapeDtypeStruct((D_MODEL, bsz * seq), F32),
        scratch_shapes=[pltpu.VMEM((ts, D_MODEL), F32), pltpu.VMEM((ts, D_MODEL), BF16),
                        pltpu.VMEM((halves * MEM_HEADS, ts // halves, mlen), F32),
                        pltpu.VMEM((halves * MEM_HEADS, ts // halves, mlen), BF16),
                        pltpu.VMEM((ts, D_MODEL), BF16)],
        compiler_params=_params("parallel", "parallel"),
        name="mix_mem",
    )(ya, ys_tm, yb, x, gmix, wout, ln1, mk, mv, wmq, wmo, ln2)


_SORT16 = ((0, 1), (2, 3), (0, 2), (1, 3), (1, 2), (4, 5), (6, 7), (4, 6), (5, 7), (5, 6), (0, 4), (2, 6), (2, 4),
           (1, 5), (3, 7), (3, 5), (1, 2), (3, 4), (5, 6), (8, 9), (10, 11), (8, 10), (9, 11), (9, 10), (12, 13),
           (14, 15), (12, 14), (13, 15), (13, 14), (8, 12), (10, 14), (10, 12), (9, 13), (11, 15), (11, 13), (9, 10),
           (11, 12), (13, 14), (0, 8), (4, 12), (4, 8), (2, 10), (6, 14), (6, 10), (2, 4), (6, 8), (10, 12), (1, 9),
           (5, 13), (5, 9), (3, 11), (7, 15), (7, 11), (3, 5), (7, 9), (11, 13), (1, 2), (3, 4), (5, 6), (7, 8),
           (9, 10), (11, 12), (13, 14))


def _sort16_desc(v):
    v = list(v)
    for i, j in _SORT16:
        v[i], v[j] = jnp.maximum(v[i], v[j]), jnp.minimum(v[i], v[j])
    return v


def _merge_bitonic_desc(v):
    v = list(v)
    for j in (8, 4, 2, 1):
        for i in range(16):
            l = i ^ j
            if l > i:
                v[i], v[l] = jnp.maximum(v[i], v[l]), jnp.minimum(v[i], v[l])
    return v


def _top16_sorted(rows, sort_result=True):
    v = _sort16_desc(rows)
    for shift in (4, 2, 1):
        partner = [pltpu.roll(x, shift, axis=0) for x in v]
        v = [jnp.maximum(v[i], partner[15 - i]) for i in range(16)]
        if sort_result or shift != 1:
            v = _merge_bitonic_desc(v)
    return v


def _count_prefix(pred, v):
    p8 = pred(v[7])
    p4 = pred(jnp.where(p8, v[11], v[3]))
    p2 = pred(jnp.where(p8, jnp.where(p4, v[13], v[9]), jnp.where(p4, v[5], v[1])))
    p1 = pred(jnp.where(p8, jnp.where(p4, jnp.where(p2, v[14], v[12]), jnp.where(p2, v[10], v[8])),
                        jnp.where(p4, jnp.where(p2, v[6], v[4]), jnp.where(p2, v[2], v[0]))))
    count = (jnp.where(p8, 8.0, 0.0) + jnp.where(p4, 4.0, 0.0)) + (jnp.where(p2, 2.0, 0.0) + jnp.where(p1, 1.0, 0.0))
    return jnp.where(pred(v[15]), 16.0, count)


def _peer_route_kernel(xt_ref, wpq_ref, keys_ref, a_ref, n_ref, b_ref, r_ref, q_ref, s1_ref, s2_ref):
    q_ref[...] = _dot(wpq_ref[...], xt_ref[...].astype(BF16)).astype(BF16)
    lax.fori_loop(0, PEER_HEADS, functools.partial(_peer_route_head, q_ref, keys_ref, a_ref, n_ref, b_ref, r_ref,
                                                   s1_ref, s2_ref), 0)


def _peer_route_head(q_ref, keys_ref, a_ref, n_ref, b_ref, r_ref, s1_ref, s2_ref, h, carry):
    q0 = pl.multiple_of(h * (2 * PEER_KEY_DIM), 2 * PEER_KEY_DIM)
    s1_ref[...] = _dot(keys_ref[h, 0], q_ref[pl.ds(q0, PEER_KEY_DIM), :])
    s2_ref[...] = _dot(keys_ref[h, 1], q_ref[pl.ds(q0 + PEER_KEY_DIM, PEER_KEY_DIM), :])
    sub = lax.broadcasted_iota(jnp.int32, (8, LANE), 0)
    groups = PEER_N_KEYS // 8

    def spread(vals):
        out = vals[7]
        for j in range(6, -1, -1):
            out = jnp.where(sub == j, vals[j], out)
        return out

    def chunk(c, carry):
        ln = pl.ds(pl.multiple_of(c * LANE, LANE), LANE)
        rows1 = [s1_ref[8 * i:8 * i + 8, ln] for i in range(groups)]
        rows2 = [s2_ref[8 * i:8 * i + 8, ln] for i in range(groups)]
        v1 = _top16_sorted(rows1)
        v2 = _top16_sorted(rows2)
        v2lo, v2hi, v1hi = spread(v2[:8]), spread(v2[8:]), spread(v1[8:])
        cands = ([v1[0] + v2lo, v1[0] + v2hi] + [v1[r] + v2lo for r in range(1, 8)] + [v1hi + v2[0]]
                 + [v1[r] + v2hi for r in range(1, 7)])
        top = _top16_sorted(cands, sort_result=False)
        thr = functools.reduce(jnp.minimum, top)
        cmax = v1[0] + v2[0]
        inv_z = 1.0 / functools.reduce(jnp.add, [jnp.exp(t - cmax) for t in top])
        for i in range(groups):
            s = rows1[i]
            kept = _count_prefix(lambda t, s=s: s + t >= thr, v2)
            in_top = s >= v1[15]
            a_ref[i, h, :, ln] = jnp.where(in_top, jnp.exp(s - v1[0]), 0.0)
            n_ref[i, h, :, ln] = jnp.where(in_top, kept, 0.0)
        for i in range(groups // 2):
            bs, rs = [], []
            for s in (rows2[2 * i], rows2[2 * i + 1]):
                rank = _count_prefix(lambda t, s=s: t > s, v2)
                rs.append(rank)
                bs.append(jnp.where(rank < float(PEER_TOPK), jnp.exp(s - v2[0]) * inv_z, 0.0))
            b_ref[h, 16 * i:16 * i + 16, ln] = jnp.concatenate(bs, axis=0).astype(BF16)
            r_ref[h, 16 * i:16 * i + 16, ln] = jnp.concatenate(rs, axis=0).astype(BF16)
        return carry

    lax.fori_loop(0, s1_ref.shape[1] // LANE, chunk, 0)
    return carry


def _peer_route(xt, wpq_t, keys):
    ntok = xt.shape[1]
    tt = min(ntok, 512)
    nblk = PEER_N_KEYS // 8
    return pl.pallas_call(
        _peer_route_kernel,
        grid=(ntok // tt,),
        in_specs=[pl.BlockSpec((D_MODEL, tt), lambda i: (0, i)),
                  pl.BlockSpec(wpq_t.shape, lambda i: (0, 0)),
                  pl.BlockSpec(keys.shape, lambda i: (0, 0, 0, 0))],
        out_specs=[pl.BlockSpec((nblk, PEER_HEADS, 8, tt), lambda i: (0, 0, 0, i)),
                   pl.BlockSpec((nblk, PEER_HEADS, 8, tt), lambda i: (0, 0, 0, i)),
                   pl.BlockSpec((PEER_HEADS, PEER_N_KEYS, tt), lambda i: (0, 0, i)),
                   pl.BlockSpec((PEER_HEADS, PEER_N_KEYS, tt), lambda i: (0, 0, i))],
        out_shape=[jax.ShapeDtypeStruct((nblk, PEER_HEADS, 8, ntok), F32),
                   jax.ShapeDtypeStruct((nblk, PEER_HEADS, 8, ntok), F32),
                   jax.ShapeDtypeStruct((PEER_HEADS, PEER_N_KEYS, ntok), BF16),
                   jax.ShapeDtypeStruct((PEER_HEADS, PEER_N_KEYS, ntok), BF16)],
        scratch_shapes=[pltpu.VMEM((PEER_HEADS * 2 * PEER_KEY_DIM, tt), BF16),
                        pltpu.VMEM((PEER_N_KEYS, tt), F32), pltpu.VMEM((PEER_N_KEYS, tt), F32)],
        compiler_params=_params("parallel"),
        name="peer_route",
    )(xt, wpq_t, keys)


PEER_I1_PER_TILE = 8
PEER_TILE = PEER_I1_PER_TILE * PEER_N_KEYS


def _peer_dense_kernel(xt_ref, a_ref, n_ref, b_ref, r_ref, u_ref, vt_ref, ln_ref, o_ref,
                       acc_ref, xb_ref, gh_ref, *, alpha):
    e = pl.program_id(1)

    @pl.when(e == 0)
    def _():
        acc_ref[...] = jnp.zeros_like(acc_ref)
        xb_ref[...] = xt_ref[...].astype(BF16)

    tt = xb_ref.shape[1]
    rows = BF16_SUBLANES
    gh_ref[...] = _dot(u_ref[...], xb_ref[...]).astype(BF16)
    cw = min(tt, 2 * LANE)
    for j in range(PEER_I1_PER_TILE):
        for c in range(tt // cw):
            ln = slice(c * cw, (c + 1) * cw)
            a_rows = [jnp.broadcast_to(a_ref[0, h, j:j + 1, ln], (rows, cw)).astype(BF16) for h in range(PEER_HEADS)]
            n_rows = [jnp.broadcast_to(n_ref[0, h, j:j + 1, ln], (rows, cw)).astype(BF16) for h in range(PEER_HEADS)]
            for g in range(PEER_N_KEYS // rows):
                i2 = slice(g * rows, (g + 1) * rows)
                gate = None
                for h in range(PEER_HEADS):
                    term = jnp.where(r_ref[h, i2, ln] < n_rows[h], b_ref[h, i2, ln] * a_rows[h], jnp.zeros((), BF16))
                    gate = term if gate is None else gate + term
                e0 = j * PEER_N_KEYS + g * rows
                gh_ref[e0:e0 + rows, ln] = gate * _gelu_tanh(gh_ref[e0:e0 + rows, ln])
    acc_ref[...] += _dot(vt_ref[...], gh_ref[...])

    @pl.when(e == pl.num_programs(1) - 1)
    def _():
        y = alpha * xt_ref[...] + acc_ref[...]
        mu = jnp.mean(y, axis=0, keepdims=True)
        c = y - mu
        var = jnp.mean(c * c, axis=0, keepdims=True)
        o_ref[...] = (c * lax.rsqrt(var + NORM_EPS)).T * ln_ref[0:1, :] + ln_ref[1:2, :]


def _peer_dense(xt, a, n, b, r, u, vt, ln, alpha):
    ntok = xt.shape[1]
    tt = min(ntok, 1024)
    ntile = PEER_EXPERTS // PEER_TILE
    return pl.pallas_call(
        functools.partial(_peer_dense_kernel, alpha=alpha),
        grid=(ntok // tt, ntile),
        in_specs=[pl.BlockSpec((D_MODEL, tt), lambda i, e: (0, i)),
                  pl.BlockSpec((1, PEER_HEADS, PEER_I1_PER_TILE, tt), lambda i, e: (e, 0, 0, i)),
                  pl.BlockSpec((1, PEER_HEADS, PEER_I1_PER_TILE, tt), lambda i, e: (e, 0, 0, i)),
                  pl.BlockSpec((PEER_HEADS, PEER_N_KEYS, tt), lambda i, e: (0, 0, i)),
                  pl.BlockSpec((PEER_HEADS, PEER_N_KEYS, tt), lambda i, e: (0, 0, i)),
                  pl.BlockSpec((PEER_TILE, D_MODEL), lambda i, e: (e, 0)),
                  pl.BlockSpec((D_MODEL, PEER_TILE), lambda i, e: (0, e)),
                  pl.BlockSpec(ln.shape, lambda i, e: (0, 0))],
        out_specs=pl.BlockSpec((tt, D_MODEL), lambda i, e: (i, 0)),
        out_shape=jax.ShapeDtypeStruct((ntok, D_MODEL), F32),
        scratch_shapes=[pltpu.VMEM((D_MODEL, tt), F32), pltpu.VMEM((D_MODEL, tt), BF16),
                        pltpu.VMEM((PEER_TILE, tt), BF16)],
        compiler_params=_params("parallel", "arbitrary"),
        name="peer_dense",
    )(xt, a, n, b, r, u, vt, ln)


def _pad_heads(w, heads, width):
    rows = w.shape[0]
    w = w.reshape(rows, heads, width)
    return jnp.pad(w, ((0, 0), (0, 0), (0, HEAD_PAD - width))).reshape(rows, heads * HEAD_PAD)


def _rotate_half_cols(w, heads, width, nope):
    rows = w.shape[0]
    w = w.reshape(rows, heads, width)
    half = (width - nope) // 2
    x1, x2 = w[..., nope:nope + half], w[..., nope + half:]
    out = jnp.concatenate([jnp.zeros_like(w[..., :nope]), -x2, x1], axis=-1)
    return out.reshape(rows, heads * width)


def _pack_inproj(w_in, g_cq, g_ckv, w_uq, w_ukv):
    c = 0
    cols = {}
    for name, width in (("cq", MLA_Q_RANK), ("ckv", MLA_KV_RANK), ("kr", MLA_ROPE), ("ssm", SSM_WIDTH),
                        ("qsb", SB_WIDTH), ("ksb", SB_WIDTH), ("vsb", SB_WIDTH)):
        cols[name] = w_in[:, c:c + width]
        c += width
    rows = w_in.shape[0]
    zeros = lambda n: jnp.zeros((rows, n), F32)
    kr = cols["kr"]
    half = MLA_ROPE // 2
    kr_plain = jnp.concatenate([zeros(MLA_NOPE), kr, zeros(HEAD_PAD - MLA_NOPE - MLA_ROPE)], axis=1)
    kr_swap = jnp.concatenate([zeros(MLA_NOPE), -kr[:, half:], kr[:, :half],
                               zeros(HEAD_PAD - MLA_NOPE - MLA_ROPE)], axis=1)
    sb_scale = SB_DIM ** -0.5 * LOG2E
    win = jnp.concatenate([cols["cq"], cols["ckv"], kr_plain, kr_swap, cols["ssm"],
                           _pad_heads(cols["qsb"] * sb_scale, SB_HEADS, SB_DIM),
                           _pad_heads(cols["ksb"], SB_HEADS, SB_DIM), cols["vsb"]], axis=1).astype(BF16)

    qk_dim = MLA_NOPE + MLA_ROPE
    wq = w_uq * (g_cq[:, None] * (qk_dim ** -0.5 * LOG2E))
    wuq = jnp.concatenate([_pad_heads(wq, MLA_HEADS, qk_dim),
                           _pad_heads(_rotate_half_cols(wq, MLA_HEADS, qk_dim, MLA_NOPE), MLA_HEADS, qk_dim)],
                          axis=1).astype(BF16)
    wkv = (w_ukv * g_ckv[:, None]).reshape(MLA_KV_RANK, MLA_HEADS, MLA_NOPE + MLA_V)
    wukv = jnp.concatenate([_pad_heads(wkv[..., :MLA_NOPE].reshape(MLA_KV_RANK, -1), MLA_HEADS, MLA_NOPE),
                            _pad_heads(wkv[..., MLA_NOPE:].reshape(MLA_KV_RANK, -1), MLA_HEADS, MLA_V)],
                           axis=1).astype(BF16)
    return win, wuq, wukv


def _pad_mla_rows(w):
    cols = w.shape[1]
    head_rows = jnp.pad(w[:MLA_V_W].reshape(MLA_HEADS, MLA_V, cols), ((0, 0), (0, HEAD_PAD - MLA_V), (0, 0)))
    return jnp.concatenate([head_rows.reshape(MLA_QK_W, cols), w[MLA_V_W:]], axis=0)


def _pack_ssm(lam_re, lam_im, log_step, b_re, b_im, c_re, c_im, d_skip):
    step = jnp.exp(log_step)[:, None]
    decay = jnp.exp(lam_re * step)
    ab_re, ab_im = decay * jnp.cos(lam_im * step), decay * jnp.sin(lam_im * step)
    inv = 1.0 / (lam_re * lam_re + lam_im * lam_im)
    f_re = ((ab_re - 1.0) * lam_re + ab_im * lam_im) * inv
    f_im = (ab_im * lam_re - (ab_re - 1.0) * lam_im) * inv
    bb_re = f_re[..., None] * b_re - f_im[..., None] * b_im
    bb_im = f_re[..., None] * b_im + f_im[..., None] * b_re
    per = SSM_GROUPS // SSM_BLOCKS
    eye = jnp.eye(per, dtype=F32)
    split = lambda w: w.reshape((SSM_BLOCKS, per) + w.shape[1:])
    blk = lambda w: jnp.einsum("bgph,gk->bghkp", split(w), eye).reshape(SSM_BLOCKS, SSM_BLOCK_CH, SSM_BLOCK_STATES)
    bbd = jnp.concatenate([blk(bb_re), blk(bb_im)], axis=2).astype(BF16)
    blk_c = lambda w: jnp.einsum("bghp,gk->bgpkh", split(w), eye).reshape(SSM_BLOCKS, SSM_BLOCK_STATES, SSM_BLOCK_CH)
    cbd = jnp.concatenate([blk_c(c_re), blk_c(-c_im)], axis=1).astype(BF16)
    lam = jnp.stack([ab_re.reshape(-1), ab_im.reshape(-1)])
    return bbd, lam, cbd, d_skip.reshape(1, SSM_WIDTH)


def kernel(x, mem, positions, w_in, g_cq, g_ckv, w_uq, w_ukv, ssm_lam_re, ssm_lam_im, ssm_log_step, ssm_b_re, ssm_b_im, ssm_c_re, ssm_c_im, ssm_d, w_glu, b_glu, g_mix, w_out, ln_mix_g, ln_mix_b, w_mq, w_mkv, w_mo, ln_mem_g, ln_mem_b, w_pq, peer_sub_keys, peer_u, peer_v, ln_ffn_g, ln_ffn_b):
    bsz, seq, _ = x.shape
    depth = w_in.shape[0]
    alpha = (2 * depth) ** 0.25
    cos_t, sin_t = _rope_tables(positions)
    for l in range(depth):
        win, wuq, wukv = _pack_inproj(w_in[l], g_cq[l], g_ckv[l], w_uq[l], w_ukv[l])
        q, k, v, u_tm, q_sb, k_sb, v_sb = _inproj(x, win, wuq, wukv, cos_t, sin_t)
        y_mla = _mla_attention(q, k, v)
        y_sb = _sb_attention(q_sb, k_sb, v_sb)
        bbd, lam, cbd, d_row = _pack_ssm(ssm_lam_re[l], ssm_lam_im[l], ssm_log_step[l], ssm_b_re[l], ssm_b_im[l],
                                         ssm_c_re[l], ssm_c_im[l], ssm_d[l])
        y_ssm = _ssm(u_tm, bbd, lam, cbd, d_row,
                     w_glu[l].astype(BF16), b_glu[l].reshape(1, SSM_WIDTH), bsz)
        mk, mv = _memkv(mem, w_mkv[l].astype(BF16))
        xt = _mix_mem(y_mla, y_ssm, y_sb, x,
                      _pad_mla_rows(g_mix[l][:, None]).reshape(1, -1), _pad_mla_rows(w_out[l]).astype(BF16),
                      jnp.stack([ln_mix_g[l], ln_mix_b[l]]),
                      mk, mv, (w_mq[l] * MEM_HEAD_DIM ** -0.5).astype(BF16), w_mo[l].astype(BF16),
                      jnp.stack([ln_mem_g[l], ln_mem_b[l]]), alpha)
        a, n, b, r = _peer_route(xt, w_pq[l].T.astype(BF16), peer_sub_keys[l].astype(BF16))
        x = _peer_dense(xt, a, n, b, r, peer_u[l].astype(BF16), peer_v[l].T.astype(BF16),
                        jnp.stack([ln_ffn_g[l], ln_ffn_b[l]]), alpha).reshape(bsz, seq, D_MODEL)
    return x
```

```python
import functools
import math

import jax
import jax.numpy as jnp
from jax import lax
from jax.experimental import pallas as pl
from jax.experimental.pallas import tpu as pltpu

F32 = jnp.float32
BF16 = jnp.bfloat16

D_MODEL = 1024
CHUNK = 64
NORM_EPS = 1e-5
NEG_BIG = -1e30
LOG2E = math.log2(math.e)

MLA_HEADS = 6
MLA_NOPE = 64
MLA_ROPE = 32
MLA_V = 64
MLA_Q_RANK = 256
MLA_KV_RANK = 128
ROPE_THETA = 10000.0

SSM_GROUPS = 24
SSM_CH = 16
SSM_STATE = 64
SSM_WIDTH = SSM_GROUPS * SSM_CH
SSM_STATES = SSM_GROUPS * SSM_STATE
SSM_BLOCKS = 3
SSM_BLOCK_CH = SSM_WIDTH // SSM_BLOCKS
SSM_BLOCK_STATES = SSM_STATES // SSM_BLOCKS

SB_HEADS = 4
SB_DIM = 64
SB_WIDTH = SB_HEADS * SB_DIM

MEM_HEADS = 4
MEM_HEAD_DIM = D_MODEL // MEM_HEADS

PEER_HEADS = 8
PEER_N_KEYS = 128
PEER_TOPK = 16
PEER_KEY_DIM = 128
PEER_EXPERTS = PEER_N_KEYS * PEER_N_KEYS

LANE = 128
BF16_SUBLANES = 16
HEAD_PAD = 128
ATTN_TQ = 512
ATTN_TK = 256
SB_SCAN = 128

VMEM_LIMIT = 56 * 1024 * 1024

_C_CQ = 0
_C_CKV = _C_CQ + MLA_Q_RANK
_C_KRP = _C_CKV + MLA_KV_RANK
_C_KRS = _C_KRP + HEAD_PAD
_C_SSM = _C_KRS + HEAD_PAD
_C_QSB = _C_SSM + SSM_WIDTH
_C_KSB = _C_QSB + SB_HEADS * HEAD_PAD
_C_VSB = _C_KSB + SB_HEADS * HEAD_PAD
_C_END = _C_VSB + SB_WIDTH
MLA_QK_W = MLA_HEADS * HEAD_PAD
MLA_V_W = MLA_HEADS * MLA_V
MLA_V_ROWS = MLA_V + BF16_SUBLANES


def _params(*sem):
    return pltpu.CompilerParams(dimension_semantics=sem, vmem_limit_bytes=VMEM_LIMIT)


def _dot(a, b):
    return jnp.dot(a, b, preferred_element_type=F32)


def _dot_nt(a, b):
    return lax.dot_general(a, b, (((1,), (1,)), ((), ())), preferred_element_type=F32)


def _gelu_tanh(x):
    c = math.sqrt(2.0 / math.pi)
    inner = x * (c + (c * 0.044715) * (x * x))
    return (0.5 * x) * (1.0 + jnp.tanh(inner))


def _layer_norm_rows(v, g, b):
    mu = jnp.mean(v, axis=-1, keepdims=True)
    c = v - mu
    var = jnp.mean(c * c, axis=-1, keepdims=True)
    return c * lax.rsqrt(var + NORM_EPS) * g + b


def _rms_rows(v, width):
    return v * lax.rsqrt(jnp.sum(v * v, axis=-1, keepdims=True) * (1.0 / width) + NORM_EPS)


def _rope_kernel(pos_ref, freq_ref, cos_ref, sin_ref):
    ang = pos_ref[0].astype(F32) * freq_ref[...]
    lane = lax.broadcasted_iota(jnp.int32, ang.shape, 1)
    rot = (lane >= MLA_NOPE) & (lane < MLA_NOPE + MLA_ROPE)
    cos_ref[0] = jnp.where(rot, jnp.cos(ang), jnp.where(lane < MLA_NOPE, 1.0, 0.0))
    sin_ref[0] = jnp.where(rot, jnp.sin(ang), 0.0)


def _rope_tables(positions):
    bsz, seq = positions.shape
    ts = min(seq, 512)
    half = MLA_ROPE // 2
    freq = ROPE_THETA ** (-jnp.arange(half, dtype=F32) / half)
    freq_row = jnp.zeros((1, HEAD_PAD), F32).at[0, MLA_NOPE:MLA_NOPE + MLA_ROPE].set(jnp.tile(freq, 2))
    return pl.pallas_call(
        _rope_kernel,
        grid=(bsz, seq // ts),
        in_specs=[pl.BlockSpec((1, ts, 1), lambda b, s: (b, s, 0)),
                  pl.BlockSpec((1, HEAD_PAD), lambda b, s: (0, 0))],
        out_specs=[pl.BlockSpec((1, ts, HEAD_PAD), lambda b, s: (b, s, 0))] * 2,
        out_shape=[jax.ShapeDtypeStruct((bsz, seq, HEAD_PAD), F32)] * 2,
        compiler_params=_params("parallel", "parallel"),
        name="rope_tables",
    )(positions.reshape(bsz, seq, 1), freq_row)


def _inproj_kernel(x_ref, win_ref, wuq_ref, wukv_ref, cos_ref, sin_ref, ones_ref,
                   q_ref, k_ref, v_ref, u_ref, qs_ref, ks_ref, vs_ref):
    xb = x_ref[0].astype(BF16)
    acc = _dot(xb, win_ref[...])
    cos = cos_ref[0]
    sin = sin_ref[0]
    cqn = _rms_rows(acc[:, _C_CQ:_C_CKV], MLA_Q_RANK).astype(BF16)
    qq = _dot(cqn, wuq_ref[...])
    ckvn = _rms_rows(acc[:, _C_CKV:_C_KRP], MLA_KV_RANK).astype(BF16)
    kv = _dot(ckvn, wukv_ref[...])
    k_rope = acc[:, _C_KRP:_C_KRS] * cos + acc[:, _C_KRS:_C_SSM] * sin
    for h in range(MLA_HEADS):
        lo, hi = h * HEAD_PAD, (h + 1) * HEAD_PAD
        q_ref[0, lo:hi, :] = (qq[:, lo:hi] * cos + qq[:, MLA_QK_W + lo:MLA_QK_W + hi] * sin).T.astype(BF16)
        k_ref[0, :, lo:hi] = (kv[:, lo:hi] + k_rope).astype(BF16)
    v_ref[0] = (kv[:, MLA_QK_W:] + ones_ref[...]).T.astype(BF16)
    u_ref[...] = acc[:, _C_SSM:_C_QSB].astype(BF16)
    qs_ref[0] = acc[:, _C_QSB:_C_KSB].T.astype(BF16)
    ks_ref[0] = acc[:, _C_KSB:_C_VSB].astype(BF16)
    vs_ref[0] = acc[:, _C_VSB:_C_END].T.astype(BF16)


def _inproj(x, win, wuq, wukv, cos_t, sin_t):
    bsz, seq, _ = x.shape
    ts = min(seq, 512)
    tok = lambda w: pl.BlockSpec((1, ts, w), lambda b, s: (b, s, 0))
    tok_t = lambda w: pl.BlockSpec((1, w, ts), lambda b, s: (b, 0, s))
    full = lambda a: pl.BlockSpec(a.shape, lambda b, s: (0,) * a.ndim)
    sb_w = SB_HEADS * HEAD_PAD
    lane = jnp.arange(MLA_QK_W) % HEAD_PAD
    ones_row = (lane == MLA_V).astype(F32).reshape(1, MLA_QK_W)
    return pl.pallas_call(
        _inproj_kernel,
        grid=(bsz, seq // ts),
        in_specs=[tok(D_MODEL), full(win), full(wuq), full(wukv), tok(HEAD_PAD), tok(HEAD_PAD), full(ones_row)],
        out_specs=[tok_t(MLA_QK_W), tok(MLA_QK_W), tok_t(MLA_QK_W),
                   pl.BlockSpec((ts, SSM_WIDTH), lambda b, s: (s, b)),
                   tok_t(sb_w), tok(sb_w), tok_t(SB_WIDTH)],
        out_shape=[jax.ShapeDtypeStruct((bsz, MLA_QK_W, seq), BF16),
                   jax.ShapeDtypeStruct((bsz, seq, MLA_QK_W), BF16),
                   jax.ShapeDtypeStruct((bsz, MLA_QK_W, seq), BF16),
                   jax.ShapeDtypeStruct((seq, bsz * SSM_WIDTH), BF16),
                   jax.ShapeDtypeStruct((bsz, sb_w, seq), BF16),
                   jax.ShapeDtypeStruct((bsz, seq, sb_w), BF16),
                   jax.ShapeDtypeStruct((bsz, SB_WIDTH, seq), BF16)],
        compiler_params=_params("parallel", "parallel"),
        name="inproj",
    )(x, win, wuq, wukv, cos_t, sin_t, ones_row)


def _mla_kernel(q_ref, k_ref, vt_ref, o_ref, m_ref, acc_ref, s2_ref, p2_ref, *, tq, tk):
    qi = pl.program_id(1)
    per = tq // tk
    query_chunk = (qi * tq + lax.broadcasted_iota(jnp.int32, (tk, tq), 1)) // CHUNK
    key_in_tile = lax.broadcasted_iota(jnp.int32, (tk, tq), 0)
    m_ref[...] = jnp.full(m_ref.shape, NEG_BIG, F32)
    acc_ref[...] = jnp.zeros(acc_ref.shape, F32)

    def block(kb, masked, slot=0):
        ks = pl.multiple_of(kb * tk, tk)
        s_ref, p_ref = s2_ref.at[slot], p2_ref.at[slot]
        for h in range(MLA_HEADS):
            tile = slice(h * HEAD_PAD, (h + 1) * HEAD_PAD)
            s_ref[h] = _dot(k_ref[0, pl.ds(ks, tk), tile], q_ref[0, tile, :])
        if masked:
            allowed = (ks + key_in_tile) // CHUNK <= query_chunk
        alphas = []
        for h in range(MLA_HEADS):
            s = s_ref[h]
            if masked:
                s = jnp.where(allowed, s, NEG_BIG)
            m_old = m_ref[h]
            m_new = jnp.maximum(m_old, jnp.max(s, axis=0, keepdims=True))
            p_ref[h] = jnp.exp2(s - m_new).astype(BF16)
            alphas.append(jnp.exp2(m_old - m_new))
            m_ref[h] = m_new
        for h in range(MLA_HEADS):
            tile = slice(h * HEAD_PAD, (h + 1) * HEAD_PAD)
            used = slice(h * HEAD_PAD, h * HEAD_PAD + MLA_V_ROWS)
            acc_ref[h] = alphas[h] * acc_ref[h] + _dot(vt_ref[0, used, pl.ds(ks, tk)], p_ref[h])

    def body(i, carry):
        for d in range(per):
            block(i * per + d, False, d)
        return carry

    lax.fori_loop(0, qi, body, 0)
    for d in range(per):
        block(qi * per + d, True, d)
    pad = jnp.zeros((HEAD_PAD - MLA_V, tq), F32)
    for h in range(MLA_HEADS):
        acc = acc_ref[h]
        out_t = jnp.concatenate([acc[0:MLA_V, :] / acc[MLA_V:MLA_V + 1, :], pad], axis=0)
        o_ref[0, :, h * HEAD_PAD:(h + 1) * HEAD_PAD] = out_t.T.astype(BF16)


def _mla_attention(qt, k, vt):
    bsz, seq, _ = k.shape
    tq, tk = min(seq, ATTN_TQ), min(seq, ATTN_TK)
    return pl.pallas_call(
        functools.partial(_mla_kernel, tq=tq, tk=tk),
        grid=(bsz, seq // tq),
        in_specs=[pl.BlockSpec((1, MLA_QK_W, tq), lambda b, i: (b, 0, i)),
                  pl.BlockSpec((1, seq, MLA_QK_W), lambda b, i: (b, 0, 0)),
                  pl.BlockSpec((1, MLA_QK_W, seq), lambda b, i: (b, 0, 0))],
        out_specs=pl.BlockSpec((1, tq, MLA_QK_W), lambda b, i: (b, i, 0)),
        out_shape=jax.ShapeDtypeStruct((bsz, seq, MLA_QK_W), BF16),
        scratch_shapes=[pltpu.VMEM((MLA_HEADS, 1, tq), F32), pltpu.VMEM((MLA_HEADS, MLA_V_ROWS, tq), F32),
                        pltpu.VMEM((tq // tk, MLA_HEADS, tk, tq), F32), pltpu.VMEM((tq // tk, MLA_HEADS, tk, tq), BF16)],
        compiler_params=_params("parallel", "arbitrary"),
        name="mla_attention",
    )(qt, k, vt)


def _sb_kernel(q_ref, k_ref, vt_ref, tri_ref, o_ref, right_ref, acc_ref, z2_ref, split2_ref, sum2_ref, w2_ref, *, tq, tk):
    qi = pl.program_id(1)
    per = tq // tk
    scan = min(tk, SB_SCAN)
    nscan = tk // scan
    query_pos = qi * tq + lax.broadcasted_iota(jnp.int32, (tk, tq), 1)
    key_in_tile = lax.broadcasted_iota(jnp.int32, (tk, tq), 0)
    right_ref[...] = jnp.zeros(right_ref.shape, F32)
    acc_ref[...] = jnp.zeros(acc_ref.shape, F32)

    def block(kb, masked, slot=0):
        ks = pl.multiple_of(kb * tk, tk)
        z_ref, split_ref, sum_ref, w_ref = z2_ref.at[slot], split2_ref.at[slot], sum2_ref.at[slot], w2_ref.at[slot]
        for h in range(SB_HEADS):
            tile = slice(h * HEAD_PAD, (h + 1) * HEAD_PAD)
            z_ref[h] = _dot(k_ref[0, pl.ds(ks, tk), tile], q_ref[0, tile, :])
        if masked:
            earlier = ks + key_in_tile < query_pos
        for h in range(SB_HEADS):
            z = z_ref[h]
            neg_abs = pltpu.bitcast(pltpu.bitcast(z, jnp.uint32) | jnp.uint32(0x80000000), F32)
            fail = jnp.maximum(z, 0.0) + jnp.log2(1.0 + jnp.exp2(neg_abs))
            z_ref[h] = z - fail
            if masked:
                fail = jnp.where(earlier, fail, 0.0)
            for u in range(nscan):
                part = fail[u * scan:(u + 1) * scan, :]
                hi = part.astype(BF16)
                split_ref[h, 2 * u * scan:(2 * u + 1) * scan, :] = hi
                split_ref[h, (2 * u + 1) * scan:(2 * u + 2) * scan, :] = (part - hi.astype(F32)).astype(BF16)
                sum_ref[h, u] = jnp.sum(part, axis=0, keepdims=True)
        for h in range(SB_HEADS):
            after = right_ref[h]
            for u in range(nscan - 1, -1, -1):
                rows = slice(u * scan, (u + 1) * scan)
                between = _dot(tri_ref[...], split_ref[h, 2 * u * scan:(2 * u + 2) * scan, :]) + after
                w = jnp.exp2(z_ref[h, rows, :] - between)
                if masked:
                    w = jnp.where(earlier[rows, :], w, 0.0)
                w_ref[h, rows, :] = w.astype(BF16)
                after = after + sum_ref[h, u]
            right_ref[h] = after
        for h in range(SB_HEADS):
            pair = slice((h // 2) * HEAD_PAD, (h // 2 + 1) * HEAD_PAD)
            acc_ref[h] += _dot(vt_ref[0, pair, pl.ds(ks, tk)], w_ref[h])

    def body(i, carry):
        for d in range(per):
            block((qi - i) * per - 1 - d, False, d)
        return carry

    for d in range(per - 1, -1, -1):
        block(qi * per + d, True, d)
    lax.fori_loop(0, qi, body, 0)
    row = lax.broadcasted_iota(jnp.int32, (HEAD_PAD, tq), 0)
    for pair in range(SB_HEADS // 2):
        out_t = jnp.where(row < SB_DIM, acc_ref[2 * pair], acc_ref[2 * pair + 1])
        o_ref[0, :, pair * HEAD_PAD:(pair + 1) * HEAD_PAD] = out_t.T.astype(BF16)


def _sb_attention(qt, k, vt):
    bsz, seq, _ = k.shape
    tq, tk = min(seq, ATTN_TQ), min(seq, ATTN_TK)
    sb_w = SB_HEADS * HEAD_PAD
    scan = min(tk, SB_SCAN)
    tri = (lax.broadcasted_iota(jnp.int32, (scan, scan), 1) > lax.broadcasted_iota(jnp.int32, (scan, scan), 0)).astype(BF16)
    tri2 = jnp.concatenate([tri, tri], axis=1)
    return pl.pallas_call(
        functools.partial(_sb_kernel, tq=tq, tk=tk),
        grid=(bsz, seq // tq),
        in_specs=[pl.BlockSpec((1, sb_w, tq), lambda b, i: (b, 0, i)),
                  pl.BlockSpec((1, seq, sb_w), lambda b, i: (b, 0, 0)),
                  pl.BlockSpec((1, SB_WIDTH, seq), lambda b, i: (b, 0, 0)),
                  pl.BlockSpec((scan, 2 * scan), lambda b, i: (0, 0))],
        out_specs=pl.BlockSpec((1, tq, SB_WIDTH), lambda b, i: (b, i, 0)),
        out_shape=jax.ShapeDtypeStruct((bsz, seq, SB_WIDTH), BF16),
        scratch_shapes=[pltpu.VMEM((SB_HEADS, 1, tq), F32), pltpu.VMEM((SB_HEADS, HEAD_PAD, tq), F32),
                        pltpu.VMEM((tq // tk, SB_HEADS, tk, tq), F32), pltpu.VMEM((tq // tk, SB_HEADS, 2 * tk, tq), BF16),
                        pltpu.VMEM((tq // tk, SB_HEADS, tk // scan, 1, tq), F32),
                        pltpu.VMEM((tq // tk, SB_HEADS, tk, tq), BF16)],
        compiler_params=_params("parallel", "arbitrary"),
        name="sb_attention",
    )(qt, k, vt, tri2)


def _ssm_kernel(u_ref, bbd_ref, lam_ref, cbd_ref, d_ref, wglu_ref, bglu_ref, y_ref, h_ref, hs_ref, rows_ref, *, tt, nb):
    @pl.when(pl.program_id(0) == 0)
    def _():
        h_ref[...] = jnp.zeros_like(h_ref)

    tiles = SSM_WIDTH // LANE
    for b in range(nb):
        for c in range(tiles):
            lanes = slice(b * SSM_WIDTH + c * LANE, b * SSM_WIDTH + (c + 1) * LANE)
            rows_ref[c, pl.ds(b, tt, stride=nb), :] = u_ref[:, lanes].astype(F32)
    u = jnp.concatenate([rows_ref[c] for c in range(tiles)], axis=1).astype(BF16)
    half, wide = SSM_BLOCK_STATES, 2 * SSM_BLOCK_STATES
    for b in range(SSM_BLOCKS):
        hs_ref[:, b * wide:(b + 1) * wide] = _dot(u[:, b * SSM_BLOCK_CH:(b + 1) * SSM_BLOCK_CH], bbd_ref[b])
    lam_re = jnp.broadcast_to(lam_ref[0:1, :], (nb, SSM_STATES))
    lam_im = jnp.broadcast_to(lam_ref[1:2, :], (nb, SSM_STATES))

    def step(t, carry):
        r0 = pl.multiple_of(t * nb, nb)
        out = []
        for b in range(SSM_BLOCKS):
            h_re, h_im = carry[b]
            l_re, l_im = lam_re[:, b * half:(b + 1) * half], lam_im[:, b * half:(b + 1) * half]
            re_cols = slice(b * wide, b * wide + half)
            im_cols = slice(b * wide + half, (b + 1) * wide)
            n_re = l_re * h_re - l_im * h_im + hs_ref[pl.ds(r0, nb), re_cols]
            n_im = l_re * h_im + l_im * h_re + hs_ref[pl.ds(r0, nb), im_cols]
            hs_ref[pl.ds(r0, nb), re_cols] = n_re
            hs_ref[pl.ds(r0, nb), im_cols] = n_im
            out.append((n_re, n_im))
        return tuple(out)

    init = tuple((h_ref[:, b * wide:b * wide + half], h_ref[:, b * wide + half:(b + 1) * wide]) for b in range(SSM_BLOCKS))
    final = lax.fori_loop(0, tt, step, init, unroll=4)
    for b in range(SSM_BLOCKS):
        h_ref[:, b * wide:b * wide + half] = final[b][0]
        h_ref[:, b * wide + half:(b + 1) * wide] = final[b][1]

    y = jnp.concatenate([_dot(hs_ref[:, b * wide:(b + 1) * wide].astype(BF16), cbd_ref[b]) for b in range(SSM_BLOCKS)],
                        axis=1) + d_ref[...] * u.astype(F32)
    y = jax.nn.gelu(y)
    y = y * jax.nn.sigmoid(_dot(y.astype(BF16), wglu_ref[...]) + bglu_ref[...])
    for c in range(tiles):
        rows_ref[c] = y[:, c * LANE:(c + 1) * LANE]
    for b in range(nb):
        for c in range(tiles):
            lanes = slice(b * SSM_WIDTH + c * LANE, b * SSM_WIDTH + (c + 1) * LANE)
            y_ref[:, lanes] = rows_ref[c, pl.ds(b, tt, stride=nb), :].astype(BF16)


def _ssm(u_tm, bbd, lam, cbd, d_row, wglu, bglu_row, nb):
    seq = u_tm.shape[0]
    tt = min(seq, 128)
    full = lambda a: pl.BlockSpec(a.shape, lambda i: (0,) * a.ndim)
    return pl.pallas_call(
        functools.partial(_ssm_kernel, tt=tt, nb=nb),
        grid=(seq // tt,),
        in_specs=[pl.BlockSpec((tt, nb * SSM_WIDTH), lambda i: (i, 0)),
                  full(bbd), full(lam), full(cbd), full(d_row), full(wglu), full(bglu_row)],
        out_specs=pl.BlockSpec((tt, nb * SSM_WIDTH), lambda i: (i, 0)),
        out_shape=jax.ShapeDtypeStruct((seq, nb * SSM_WIDTH), BF16),
        scratch_shapes=[pltpu.VMEM((nb, 2 * SSM_STATES), F32),
                        pltpu.VMEM((tt * nb, 2 * SSM_STATES), F32),
                        pltpu.VMEM((SSM_WIDTH // LANE, tt * nb, LANE), F32)],
        compiler_params=_params("arbitrary"),
        name="ssm",
    )(u_tm, bbd, lam, cbd, d_row, wglu, bglu_row)


def _memkv_kernel(mem_ref, w_ref, k_ref, v_ref):
    kv = _dot(mem_ref[0].astype(BF16), w_ref[...])
    k_ref[0] = kv[:, :D_MODEL].astype(BF16)
    v_ref[0] = kv[:, D_MODEL:].astype(BF16)


def _memkv(mem, w_mkv):
    bsz, mlen, _ = mem.shape
    return pl.pallas_call(
        _memkv_kernel,
        grid=(bsz,),
        in_specs=[pl.BlockSpec((1, mlen, D_MODEL), lambda b: (b, 0, 0)),
                  pl.BlockSpec(w_mkv.shape, lambda b: (0, 0))],
        out_specs=[pl.BlockSpec((1, mlen, D_MODEL), lambda b: (b, 0, 0))] * 2,
        out_shape=[jax.ShapeDtypeStruct((bsz, mlen, D_MODEL), BF16)] * 2,
        compiler_params=_params("parallel"),
        name="mem_kv",
    )(mem, w_mkv)


def _mix_mem_kernel(ya_ref, ys_ref, yb_ref, x_ref, gmix_ref, wout_ref, ln1_ref,
                    mk_ref, mv_ref, wmq_ref, wmo_ref, ln2_ref, xt_ref,
                    x1_ref, xq_ref, s_ref, p_ref, o_ref, *, alpha, halves):
    gm = gmix_ref[...]
    c1, c2 = MLA_QK_W, MLA_QK_W + SSM_WIDTH
    rows = x_ref.shape[1] // halves
    parts = [slice(i * rows, (i + 1) * rows) for i in range(halves)]
    for r in parts:
        ya = (_rms_rows(ya_ref[0, r, :].astype(F32), MLA_V_W) * gm[:, 0:c1]).astype(BF16)
        ys = (_rms_rows(ys_ref[r, :].astype(F32), SSM_WIDTH) * gm[:, c1:c2]).astype(BF16)
        yb = (_rms_rows(yb_ref[0, r, :].astype(F32), SB_WIDTH) * gm[:, c2:]).astype(BF16)
        x1_ref[r, :] = _dot(ya, wout_ref[0:c1, :]) + _dot(ys, wout_ref[c1:c2, :]) + _dot(yb, wout_ref[c2:, :])
    for r in parts:
        x1 = _layer_norm_rows(alpha * x_ref[0, r, :] + x1_ref[r, :], ln1_ref[0:1, :], ln1_ref[1:2, :])
        x1_ref[r, :] = x1
        xq_ref[r, :] = x1.astype(BF16)
    for r in parts:
        xq_ref[r, :] = _dot(xq_ref[r, :], wmq_ref[...]).astype(BF16)
    for i, r in enumerate(parts):
        for h in range(MEM_HEADS):
            cols = slice(h * MEM_HEAD_DIM, (h + 1) * MEM_HEAD_DIM)
            s_ref[i * MEM_HEADS + h] = _dot_nt(xq_ref[r, cols], mk_ref[0, :, cols])
    for i in range(halves * MEM_HEADS):
        s = s_ref[i]
        p = jnp.exp(s - jnp.max(s, axis=-1, keepdims=True))
        p_ref[i] = (p / jnp.sum(p, axis=-1, keepdims=True)).astype(BF16)
    for i, r in enumerate(parts):
        for h in range(MEM_HEADS):
            cols = slice(h * MEM_HEAD_DIM, (h + 1) * MEM_HEAD_DIM)
            o_ref[r, cols] = _dot(p_ref[i * MEM_HEADS + h], mv_ref[0, :, cols]).astype(BF16)
    for r in parts:
        x2 = _layer_norm_rows(alpha * x1_ref[r, :] + _dot(o_ref[r, :], wmo_ref[...]), ln2_ref[0:1, :], ln2_ref[1:2, :])
        xt_ref[:, r] = x2.T


def _mix_mem(ya, ys_tm, yb, x, gmix, wout, ln1, mk, mv, wmq, wmo, ln2, alpha):
    bsz, seq, _ = x.shape
    ts = min(seq, 512)
    nst = seq // ts
    halves = 2
    mlen = mk.shape[1]
    tok = lambda w: pl.BlockSpec((1, ts, w), lambda b, s: (b, s, 0))
    full = lambda a: pl.BlockSpec(a.shape, lambda b, s: (0,) * a.ndim)
    return pl.pallas_call(
        functools.partial(_mix_mem_kernel, alpha=alpha, halves=halves),
        grid=(bsz, nst),
        in_specs=[tok(MLA_QK_W), pl.BlockSpec((ts, SSM_WIDTH), lambda b, s: (s, b)), tok(SB_WIDTH), tok(D_MODEL),
                  full(gmix), full(wout), full(ln1),
                  pl.BlockSpec((1, mlen, D_MODEL), lambda b, s: (b, 0, 0)),
                  pl.BlockSpec((1, mlen, D_MODEL), lambda b, s: (b, 0, 0)),
                  full(wmq), full(wmo), full(ln2)],
        out_specs=pl.BlockSpec((D_MODEL, ts), lambda b, s: (0, b * nst + s)),
        out_shape=jax.ShapeDtypeStruct((D_MODEL, bsz * seq), F32),
        scratch_shapes=[pltpu.VMEM((ts, D_MODEL), F32), pltpu.VMEM((ts, D_MODEL), BF16),
                        pltpu.VMEM((halves * MEM_HEADS, ts // halves, mlen), F32),
                        pltpu.VMEM((halves * MEM_HEADS, ts // halves, mlen), BF16),
                        pltpu.VMEM((ts, D_MODEL), BF16)],
        compiler_params=_params("parallel", "parallel"),
        name="mix_mem",
    )(ya, ys_tm, yb, x, gmix, wout, ln1, mk, mv, wmq, wmo, ln2)


_SORT16 = ((0, 1), (2, 3), (0, 2), (1, 3), (1, 2), (4, 5), (6, 7), (4, 6), (5, 7), (5, 6), (0, 4), (2, 6), (2, 4),
           (1, 5), (3, 7), (3, 5), (1, 2), (3, 4), (5, 6), (8, 9), (10, 11), (8, 10), (9, 11), (9, 10), (12, 13),
           (14, 15), (12, 14), (13, 15), (13, 14), (8, 12), (10, 14), (10, 12), (9, 13), (11, 15), (11, 13), (9, 10),
           (11, 12), (13, 14), (0, 8), (4, 12), (4, 8), (2, 10), (6, 14), (6, 10), (2, 4), (6, 8), (10, 12), (1, 9),
           (5, 13), (5, 9), (3, 11), (7, 15), (7, 11), (3, 5), (7, 9), (11, 13), (1, 2), (3, 4), (5, 6), (7, 8),
           (9, 10), (11, 12), (13, 14))


def _sort16_desc(v):
    v = list(v)
    for i, j in _SORT16:
        v[i], v[j] = jnp.maximum(v[i], v[j]), jnp.minimum(v[i], v[j])
    return v


def _merge_bitonic_desc(v):
    v = list(v)
    for j in (8, 4, 2, 1):
        for i in range(16):
            l = i ^ j
            if l > i:
                v[i], v[l] = jnp.maximum(v[i], v[l]), jnp.minimum(v[i], v[l])
    return v


def _top16_sorted(rows, sort_result=True):
    v = _sort16_desc(rows)
    for shift in (4, 2, 1):
        partner = [pltpu.roll(x, shift, axis=0) for x in v]
        v = [jnp.maximum(v[i], partner[15 - i]) for i in range(16)]
        if sort_result or shift != 1:
            v = _merge_bitonic_desc(v)
    return v


def _count_prefix(pred, v):
    p8 = pred(v[7])
    p4 = pred(jnp.where(p8, v[11], v[3]))
    p2 = pred(jnp.where(p8, jnp.where(p4, v[13], v[9]), jnp.where(p4, v[5], v[1])))
    p1 = pred(jnp.where(p8, jnp.where(p4, jnp.where(p2, v[14], v[12]), jnp.where(p2, v[10], v[8])),
                        jnp.where(p4, jnp.where(p2, v[6], v[4]), jnp.where(p2, v[2], v[0]))))
    count = (jnp.where(p8, 8.0, 0.0) + jnp.where(p4, 4.0, 0.0)) + (jnp.where(p2, 2.0, 0.0) + jnp.where(p1, 1.0, 0.0))
    return jnp.where(pred(v[15]), 16.0, count)


def _peer_route_kernel(xt_ref, wpq_ref, keys_ref, a_ref, n_ref, b_ref, r_ref, q_ref, s1_ref, s2_ref):
    q_ref[...] = _dot(wpq_ref[...], xt_ref[...].astype(BF16)).astype(BF16)
    lax.fori_loop(0, PEER_HEADS, functools.partial(_peer_route_head, q_ref, keys_ref, a_ref, n_ref, b_ref, r_ref,
                                                   s1_ref, s2_ref), 0)


def _peer_route_head(q_ref, keys_ref, a_ref, n_ref, b_ref, r_ref, s1_ref, s2_ref, h, carry):
    q0 = pl.multiple_of(h * (2 * PEER_KEY_DIM), 2 * PEER_KEY_DIM)
    s1_ref[...] = _dot(keys_ref[h, 0], q_ref[pl.ds(q0, PEER_KEY_DIM), :])
    s2_ref[...] = _dot(keys_ref[h, 1], q_ref[pl.ds(q0 + PEER_KEY_DIM, PEER_KEY_DIM), :])
    sub = lax.broadcasted_iota(jnp.int32, (8, LANE), 0)
    groups = PEER_N_KEYS // 8

    def spread(vals):
        out = vals[7]
        for j in range(6, -1, -1):
            out = jnp.where(sub == j, vals[j], out)
        return out

    def chunk(c, carry):
        ln = pl.ds(pl.multiple_of(c * LANE, LANE), LANE)
        rows1 = [s1_ref[8 * i:8 * i + 8, ln] for i in range(groups)]
        rows2 = [s2_ref[8 * i:8 * i + 8, ln] for i in range(groups)]
        v1 = _top16_sorted(rows1)
        v2 = _top16_sorted(rows2)
        v2lo, v2hi, v1hi = spread(v2[:8]), spread(v2[8:]), spread(v1[8:])
        cands = ([v1[0] + v2lo, v1[0] + v2hi] + [v1[r] + v2lo for r in range(1, 8)] + [v1hi + v2[0]]
                 + [v1[r] + v2hi for r in range(1, 7)])
        top = _top16_sorted(cands, sort_result=False)
        thr = functools.reduce(jnp.minimum, top)
        cmax = v1[0] + v2[0]
        inv_z = 1.0 / functools.reduce(jnp.add, [jnp.exp(t - cmax) for t in top])
        for i in range(groups):
            s = rows1[i]
            kept = _count_prefix(lambda t, s=s: s + t >= thr, v2)
            in_top = s >= v1[15]
            a_ref[i, h, :, ln] = jnp.where(in_top, jnp.exp(s - v1[0]), 0.0)
            n_ref[i, h, :, ln] = jnp.where(in_top, kept, 0.0)
        for i in range(groups // 2):
            bs, rs = [], []
            for s in (rows2[2 * i], rows2[2 * i + 1]):
                rank = _count_prefix(lambda t, s=s: t > s, v2)
                rs.append(rank)
                bs.append(jnp.where(rank < float(PEER_TOPK), jnp.exp(s - v2[0]) * inv_z, 0.0))
            b_ref[h, 16 * i:16 * i + 16, ln] = jnp.concatenate(bs, axis=0).astype(BF16)
            r_ref[h, 16 * i:16 * i + 16, ln] = jnp.concatenate(rs, axis=0).astype(BF16)
        return carry

    lax.fori_loop(0, s1_ref.shape[1] // LANE, chunk, 0)
    return carry


def _peer_route(xt, wpq_t, keys):
    ntok = xt.shape[1]
    tt = min(ntok, 512)
    nblk = PEER_N_KEYS // 8
    return pl.pallas_call(
        _peer_route_kernel,
        grid=(ntok // tt,),
        in_specs=[pl.BlockSpec((D_MODEL, tt), lambda i: (0, i)),
                  pl.BlockSpec(wpq_t.shape, lambda i: (0, 0)),
                  pl.BlockSpec(keys.shape, lambda i: (0, 0, 0, 0))],
        out_specs=[pl.BlockSpec((nblk, PEER_HEADS, 8, tt), lambda i: (0, 0, 0, i)),
                   pl.BlockSpec((nblk, PEER_HEADS, 8, tt), lambda i: (0, 0, 0, i)),
                   pl.BlockSpec((PEER_HEADS, PEER_N_KEYS, tt), lambda i: (0, 0, i)),
                   pl.BlockSpec((PEER_HEADS, PEER_N_KEYS, tt), lambda i: (0, 0, i))],
        out_shape=[jax.ShapeDtypeStruct((nblk, PEER_HEADS, 8, ntok), F32),
                   jax.ShapeDtypeStruct((nblk, PEER_HEADS, 8, ntok), F32),
                   jax.ShapeDtypeStruct((PEER_HEADS, PEER_N_KEYS, ntok), BF16),
                   jax.ShapeDtypeStruct((PEER_HEADS, PEER_N_KEYS, ntok), BF16)],
        scratch_shapes=[pltpu.VMEM((PEER_HEADS * 2 * PEER_KEY_DIM, tt), BF16),
                        pltpu.VMEM((PEER_N_KEYS, tt), F32), pltpu.VMEM((PEER_N_KEYS, tt), F32)],
        compiler_params=_params("parallel"),
        name="peer_route",
    )(xt, wpq_t, keys)


PEER_I1_PER_TILE = 8
PEER_TILE = PEER_I1_PER_TILE * PEER_N_KEYS


def _peer_dense_kernel(xt_ref, a_ref, n_ref, b_ref, r_ref, u_ref, vt_ref, ln_ref, o_ref,
                       acc_ref, xb_ref, gh_ref, *, alpha):
    e = pl.program_id(1)

    @pl.when(e == 0)
    def _():
        acc_ref[...] = jnp.zeros_like(acc_ref)
        xb_ref[...] = xt_ref[...].astype(BF16)

    tt = xb_ref.shape[1]
    rows = BF16_SUBLANES
    gh_ref[...] = _dot(u_ref[...], xb_ref[...]).astype(BF16)
    cw = min(tt, 2 * LANE)
    for j in range(PEER_I1_PER_TILE):
        for c in range(tt // cw):
            ln = slice(c * cw, (c + 1) * cw)
            a_rows = [jnp.broadcast_to(a_ref[0, h, j:j + 1, ln], (rows, cw)).astype(BF16) for h in range(PEER_HEADS)]
            n_rows = [jnp.broadcast_to(n_ref[0, h, j:j + 1, ln], (rows, cw)).astype(BF16) for h in range(PEER_HEADS)]
            for g in range(PEER_N_KEYS // rows):
                i2 = slice(g * rows, (g + 1) * rows)
                gate = None
                for h in range(PEER_HEADS):
                    term = jnp.where(r_ref[h, i2, ln] < n_rows[h], b_ref[h, i2, ln] * a_rows[h], jnp.zeros((), BF16))
                    gate = term if gate is None else gate + term
                e0 = j * PEER_N_KEYS + g * rows
                gh_ref[e0:e0 + rows, ln] = gate * _gelu_tanh(gh_ref[e0:e0 + rows, ln])
    acc_ref[...] += _dot(vt_ref[...], gh_ref[...])

    @pl.when(e == pl.num_programs(1) - 1)
    def _():
        y = alpha * xt_ref[...] + acc_ref[...]
        mu = jnp.mean(y, axis=0, keepdims=True)
        c = y - mu
        var = jnp.mean(c * c, axis=0, keepdims=True)
        o_ref[...] = (c * lax.rsqrt(var + NORM_EPS)).T * ln_ref[0:1, :] + ln_ref[1:2, :]


def _peer_dense(xt, a, n, b, r, u_all, vt_all, layer, ln, alpha):
    ntok = xt.shape[1]
    tt = min(ntok, 1024)
    ntile = PEER_EXPERTS // PEER_TILE
    return pl.pallas_call(
        functools.partial(_peer_dense_kernel, alpha=alpha),
        grid=(ntok // tt, ntile),
        in_specs=[pl.BlockSpec((D_MODEL, tt), lambda i, e: (0, i)),
                  pl.BlockSpec((1, PEER_HEADS, PEER_I1_PER_TILE, tt), lambda i, e: (e, 0, 0, i)),
                  pl.BlockSpec((1, PEER_HEADS, PEER_I1_PER_TILE, tt), lambda i, e: (e, 0, 0, i)),
                  pl.BlockSpec((PEER_HEADS, PEER_N_KEYS, tt), lambda i, e: (0, 0, i)),
                  pl.BlockSpec((PEER_HEADS, PEER_N_KEYS, tt), lambda i, e: (0, 0, i)),
                  pl.BlockSpec((None, PEER_TILE, D_MODEL), lambda i, e: (layer, e, 0)),
                  pl.BlockSpec((None, D_MODEL, PEER_TILE), lambda i, e: (layer, 0, e)),
                  pl.BlockSpec(ln.shape, lambda i, e: (0, 0))],
        out_specs=pl.BlockSpec((tt, D_MODEL), lambda i, e: (i, 0)),
        out_shape=jax.ShapeDtypeStruct((ntok, D_MODEL), F32),
        scratch_shapes=[pltpu.VMEM((D_MODEL, tt), F32), pltpu.VMEM((D_MODEL, tt), BF16),
                        pltpu.VMEM((PEER_TILE, tt), BF16)],
        compiler_params=_params("parallel", "arbitrary"),
        name="peer_dense",
    )(xt, a, n, b, r, u_all, vt_all, ln)


def _pad_heads(w, heads, width):
    rows = w.shape[0]
    w = w.reshape(rows, heads, width)
    return jnp.pad(w, ((0, 0), (0, 0), (0, HEAD_PAD - width))).reshape(rows, heads * HEAD_PAD)


def _rotate_half_cols(w, heads, width, nope):
    rows = w.shape[0]
    w = w.reshape(rows, heads, width)
    half = (width - nope) // 2
    x1, x2 = w[..., nope:nope + half], w[..., nope + half:]
    out = jnp.concatenate([jnp.zeros_like(w[..., :nope]), -x2, x1], axis=-1)
    return out.reshape(rows, heads * width)


def _pack_inproj(w_in, g_cq, g_ckv, w_uq, w_ukv):
    c = 0
    cols = {}
    for name, width in (("cq", MLA_Q_RANK), ("ckv", MLA_KV_RANK), ("kr", MLA_ROPE), ("ssm", SSM_WIDTH),
                        ("qsb", SB_WIDTH), ("ksb", SB_WIDTH), ("vsb", SB_WIDTH)):
        cols[name] = w_in[:, c:c + width]
        c += width
    rows = w_in.shape[0]
    zeros = lambda n: jnp.zeros((rows, n), F32)
    kr = cols["kr"]
    half = MLA_ROPE // 2
    kr_plain = jnp.concatenate([zeros(MLA_NOPE), kr, zeros(HEAD_PAD - MLA_NOPE - MLA_ROPE)], axis=1)
    kr_swap = jnp.concatenate([zeros(MLA_NOPE), -kr[:, half:], kr[:, :half],
                               zeros(HEAD_PAD - MLA_NOPE - MLA_ROPE)], axis=1)
    sb_scale = SB_DIM ** -0.5 * LOG2E
    win = jnp.concatenate([cols["cq"], cols["ckv"], kr_plain, kr_swap, cols["ssm"],
                           _pad_heads(cols["qsb"] * sb_scale, SB_HEADS, SB_DIM),
                           _pad_heads(cols["ksb"], SB_HEADS, SB_DIM), cols["vsb"]], axis=1).astype(BF16)

    qk_dim = MLA_NOPE + MLA_ROPE
    wq = w_uq * (g_cq[:, None] * (qk_dim ** -0.5 * LOG2E))
    wuq = jnp.concatenate([_pad_heads(wq, MLA_HEADS, qk_dim),
                           _pad_heads(_rotate_half_cols(wq, MLA_HEADS, qk_dim, MLA_NOPE), MLA_HEADS, qk_dim)],
                          axis=1).astype(BF16)
    wkv = (w_ukv * g_ckv[:, None]).reshape(MLA_KV_RANK, MLA_HEADS, MLA_NOPE + MLA_V)
    wukv = jnp.concatenate([_pad_heads(wkv[..., :MLA_NOPE].reshape(MLA_KV_RANK, -1), MLA_HEADS, MLA_NOPE),
                            _pad_heads(wkv[..., MLA_NOPE:].reshape(MLA_KV_RANK, -1), MLA_HEADS, MLA_V)],
                           axis=1).astype(BF16)
    return win, wuq, wukv


def _pad_mla_rows(w):
    cols = w.shape[1]
    head_rows = jnp.pad(w[:MLA_V_W].reshape(MLA_HEADS, MLA_V, cols), ((0, 0), (0, HEAD_PAD - MLA_V), (0, 0)))
    return jnp.concatenate([head_rows.reshape(MLA_QK_W, cols), w[MLA_V_W:]], axis=0)


def _pack_ssm(lam_re, lam_im, log_step, b_re, b_im, c_re, c_im, d_skip):
    step = jnp.exp(log_step)[:, None]
    decay = jnp.exp(lam_re * step)
    ab_re, ab_im = decay * jnp.cos(lam_im * step), decay * jnp.sin(lam_im * step)
    inv = 1.0 / (lam_re * lam_re + lam_im * lam_im)
    f_re = ((ab_re - 1.0) * lam_re + ab_im * lam_im) * inv
    f_im = (ab_im * lam_re - (ab_re - 1.0) * lam_im) * inv
    bb_re = f_re[..., None] * b_re - f_im[..., None] * b_im
    bb_im = f_re[..., None] * b_im + f_im[..., None] * b_re
    per = SSM_GROUPS // SSM_BLOCKS
    eye = jnp.eye(per, dtype=F32)
    split = lambda w: w.reshape((SSM_BLOCKS, per) + w.shape[1:])
    blk = lambda w: jnp.einsum("bgph,gk->bghkp", split(w), eye).reshape(SSM_BLOCKS, SSM_BLOCK_CH, SSM_BLOCK_STATES)
    bbd = jnp.concatenate([blk(bb_re), blk(bb_im)], axis=2).astype(BF16)
    blk_c = lambda w: jnp.einsum("bghp,gk->bgpkh", split(w), eye).reshape(SSM_BLOCKS, SSM_BLOCK_STATES, SSM_BLOCK_CH)
    cbd = jnp.concatenate([blk_c(c_re), blk_c(-c_im)], axis=1).astype(BF16)
    lam = jnp.stack([ab_re.reshape(-1), ab_im.reshape(-1)])
    return bbd, lam, cbd, d_skip.reshape(1, SSM_WIDTH)


def kernel(x, mem, positions, w_in, g_cq, g_ckv, w_uq, w_ukv, ssm_lam_re, ssm_lam_im, ssm_log_step, ssm_b_re, ssm_b_im, ssm_c_re, ssm_c_im, ssm_d, w_glu, b_glu, g_mix, w_out, ln_mix_g, ln_mix_b, w_mq, w_mkv, w_mo, ln_mem_g, ln_mem_b, w_pq, peer_sub_keys, peer_u, peer_v, ln_ffn_g, ln_ffn_b):
    bsz, seq, _ = x.shape
    depth = w_in.shape[0]
    alpha = (2 * depth) ** 0.25
    cos_t, sin_t = _rope_tables(positions)
    u_all = peer_u.astype(BF16)
    vt_all = jnp.swapaxes(peer_v, 1, 2).astype(BF16)
    for l in range(depth):
        win, wuq, wukv = _pack_inproj(w_in[l], g_cq[l], g_ckv[l], w_uq[l], w_ukv[l])
        q, k, v, u_tm, q_sb, k_sb, v_sb = _inproj(x, win, wuq, wukv, cos_t, sin_t)
        y_mla = _mla_attention(q, k, v)
        y_sb = _sb_attention(q_sb, k_sb, v_sb)
        bbd, lam, cbd, d_row = _pack_ssm(ssm_lam_re[l], ssm_lam_im[l], ssm_log_step[l], ssm_b_re[l], ssm_b_im[l],
                                         ssm_c_re[l], ssm_c_im[l], ssm_d[l])
        y_ssm = _ssm(u_tm, bbd, lam, cbd, d_row,
                     w_glu[l].astype(BF16), b_glu[l].reshape(1, SSM_WIDTH), bsz)
        mk, mv = _memkv(mem, w_mkv[l].astype(BF16))
        xt = _mix_mem(y_mla, y_ssm, y_sb, x,
                      _pad_mla_rows(g_mix[l][:, None]).reshape(1, -1), _pad_mla_rows(w_out[l]).astype(BF16),
                      jnp.stack([ln_mix_g[l], ln_mix_b[l]]),
                      mk, mv, (w_mq[l] * MEM_HEAD_DIM ** -0.5).astype(BF16), w_mo[l].astype(BF16),
                      jnp.stack([ln_mem_g[l], ln_mem_b[l]]), alpha)
        a, n, b, r = _peer_route(xt, w_pq[l].T.astype(BF16), peer_sub_keys[l].astype(BF16))
        x = _peer_dense(xt, a, n, b, r, u_all, vt_all, l,
                        jnp.stack([ln_ffn_g[l], ln_ffn_b[l]]), alpha).reshape(bsz, seq, D_MODEL)
    return x
```

```python
import functools
import math

import jax
import jax.numpy as jnp
from jax import lax
from jax.experimental import pallas as pl
from jax.experimental.pallas import tpu as pltpu

F32 = jnp.float32
BF16 = jnp.bfloat16

D_MODEL = 1024
CHUNK = 64
NORM_EPS = 1e-5
NEG_BIG = -1e30
LOG2E = math.log2(math.e)

MLA_HEADS = 6
MLA_NOPE = 64
MLA_ROPE = 32
MLA_V = 64
MLA_Q_RANK = 256
MLA_KV_RANK = 128
ROPE_THETA = 10000.0

SSM_GROUPS = 24
SSM_CH = 16
SSM_STATE = 64
SSM_WIDTH = SSM_GROUPS * SSM_CH
SSM_STATES = SSM_GROUPS * SSM_STATE
SSM_BLOCKS = 3
SSM_BLOCK_CH = SSM_WIDTH // SSM_BLOCKS
SSM_BLOCK_STATES = SSM_STATES // SSM_BLOCKS

SB_HEADS = 4
SB_DIM = 64
SB_WIDTH = SB_HEADS * SB_DIM

MEM_HEADS = 4
MEM_HEAD_DIM = D_MODEL // MEM_HEADS

PEER_HEADS = 8
PEER_N_KEYS = 128
PEER_TOPK = 16
PEER_KEY_DIM = 128
PEER_EXPERTS = PEER_N_KEYS * PEER_N_KEYS

LANE = 128
BF16_SUBLANES = 16
HEAD_PAD = 128
ATTN_TQ = 512
ATTN_TK = 256
SB_SCAN = 128

VMEM_LIMIT = 56 * 1024 * 1024

_C_CQ = 0
_C_CKV = _C_CQ + MLA_Q_RANK
_C_KRP = _C_CKV + MLA_KV_RANK
_C_KRS = _C_KRP + HEAD_PAD
_C_SSM = _C_KRS + HEAD_PAD
_C_QSB = _C_SSM + SSM_WIDTH
_C_KSB = _C_QSB + SB_HEADS * HEAD_PAD
_C_VSB = _C_KSB + SB_HEADS * HEAD_PAD
_C_END = _C_VSB + SB_WIDTH
MLA_QK_W = MLA_HEADS * HEAD_PAD
MLA_V_W = MLA_HEADS * MLA_V
MLA_V_ROWS = MLA_V + BF16_SUBLANES


def _params(*sem):
    return pltpu.CompilerParams(dimension_semantics=sem, vmem_limit_bytes=VMEM_LIMIT)


def _dot(a, b):
    return jnp.dot(a, b, preferred_element_type=F32)


def _dot_nt(a, b):
    return lax.dot_general(a, b, (((1,), (1,)), ((), ())), preferred_element_type=F32)


def _gelu_tanh(x):
    c = math.sqrt(2.0 / math.pi)
    inner = x * (c + (c * 0.044715) * (x * x))
    return (0.5 * x) * (1.0 + jnp.tanh(inner))


def _layer_norm_rows(v, g, b):
    mu = jnp.mean(v, axis=-1, keepdims=True)
    c = v - mu
    var = jnp.mean(c * c, axis=-1, keepdims=True)
    return c * lax.rsqrt(var + NORM_EPS) * g + b


def _rms_rows(v, width):
    return v * lax.rsqrt(jnp.sum(v * v, axis=-1, keepdims=True) * (1.0 / width) + NORM_EPS)


def _rope_kernel(pos_ref, freq_ref, cos_ref, sin_ref):
    ang = pos_ref[0].astype(F32) * freq_ref[...]
    lane = lax.broadcasted_iota(jnp.int32, ang.shape, 1)
    rot = (lane >= MLA_NOPE) & (lane < MLA_NOPE + MLA_ROPE)
    cos_ref[0] = jnp.where(rot, jnp.cos(ang), jnp.where(lane < MLA_NOPE, 1.0, 0.0))
    sin_ref[0] = jnp.where(rot, jnp.sin(ang), 0.0)


def _rope_tables(positions):
    bsz, seq = positions.shape
    ts = min(seq, 512)
    half = MLA_ROPE // 2
    freq = ROPE_THETA ** (-jnp.arange(half, dtype=F32) / half)
    freq_row = jnp.zeros((1, HEAD_PAD), F32).at[0, MLA_NOPE:MLA_NOPE + MLA_ROPE].set(jnp.tile(freq, 2))
    return pl.pallas_call(
        _rope_kernel,
        grid=(bsz, seq // ts),
        in_specs=[pl.BlockSpec((1, ts, 1), lambda b, s: (b, s, 0)),
                  pl.BlockSpec((1, HEAD_PAD), lambda b, s: (0, 0))],
        out_specs=[pl.BlockSpec((1, ts, HEAD_PAD), lambda b, s: (b, s, 0))] * 2,
        out_shape=[jax.ShapeDtypeStruct((bsz, seq, HEAD_PAD), F32)] * 2,
        compiler_params=_params("parallel", "parallel"),
        name="rope_tables",
    )(positions.reshape(bsz, seq, 1), freq_row)


def _inproj_kernel(x_ref, win_ref, wuq_ref, wukv_ref, cos_ref, sin_ref, ones_ref,
                   q_ref, k_ref, v_ref, u_ref, qs_ref, ks_ref, vs_ref):
    xb = x_ref[0].astype(BF16)
    acc = _dot(xb, win_ref[...])
    cos = cos_ref[0]
    sin = sin_ref[0]
    cqn = _rms_rows(acc[:, _C_CQ:_C_CKV], MLA_Q_RANK).astype(BF16)
    qq = _dot(cqn, wuq_ref[...])
    ckvn = _rms_rows(acc[:, _C_CKV:_C_KRP], MLA_KV_RANK).astype(BF16)
    kv = _dot(ckvn, wukv_ref[...])
    k_rope = acc[:, _C_KRP:_C_KRS] * cos + acc[:, _C_KRS:_C_SSM] * sin
    for h in range(MLA_HEADS):
        lo, hi = h * HEAD_PAD, (h + 1) * HEAD_PAD
        q_ref[0, lo:hi, :] = (qq[:, lo:hi] * cos + qq[:, MLA_QK_W + lo:MLA_QK_W + hi] * sin).T.astype(BF16)
        k_ref[0, :, lo:hi] = (kv[:, lo:hi] + k_rope).astype(BF16)
    v_ref[0] = (kv[:, MLA_QK_W:] + ones_ref[...]).T.astype(BF16)
    u_ref[...] = acc[:, _C_SSM:_C_QSB].astype(BF16)
    qs_ref[0] = acc[:, _C_QSB:_C_KSB].T.astype(BF16)
    ks_ref[0] = acc[:, _C_KSB:_C_VSB].astype(BF16)
    vs_ref[0] = acc[:, _C_VSB:_C_END].T.astype(BF16)


def _inproj(x, win, wuq, wukv, cos_t, sin_t):
    bsz, seq, _ = x.shape
    ts = min(seq, 512)
    tok = lambda w: pl.BlockSpec((1, ts, w), lambda b, s: (b, s, 0))
    tok_t = lambda w: pl.BlockSpec((1, w, ts), lambda b, s: (b, 0, s))
    full = lambda a: pl.BlockSpec(a.shape, lambda b, s: (0,) * a.ndim)
    sb_w = SB_HEADS * HEAD_PAD
    lane = jnp.arange(MLA_QK_W) % HEAD_PAD
    ones_row = (lane == MLA_V).astype(F32).reshape(1, MLA_QK_W)
    return pl.pallas_call(
        _inproj_kernel,
        grid=(bsz, seq // ts),
        in_specs=[tok(D_MODEL), full(win), full(wuq), full(wukv), tok(HEAD_PAD), tok(HEAD_PAD), full(ones_row)],
        out_specs=[tok_t(MLA_QK_W), tok(MLA_QK_W), tok_t(MLA_QK_W),
                   pl.BlockSpec((ts, SSM_WIDTH), lambda b, s: (s, b)),
                   tok_t(sb_w), tok(sb_w), tok_t(SB_WIDTH)],
        out_shape=[jax.ShapeDtypeStruct((bsz, MLA_QK_W, seq), BF16),
                   jax.ShapeDtypeStruct((bsz, seq, MLA_QK_W), BF16),
                   jax.ShapeDtypeStruct((bsz, MLA_QK_W, seq), BF16),
                   jax.ShapeDtypeStruct((seq, bsz * SSM_WIDTH), BF16),
                   jax.ShapeDtypeStruct((bsz, sb_w, seq), BF16),
                   jax.ShapeDtypeStruct((bsz, seq, sb_w), BF16),
                   jax.ShapeDtypeStruct((bsz, SB_WIDTH, seq), BF16)],
        compiler_params=_params("parallel", "parallel"),
        name="inproj",
    )(x, win, wuq, wukv, cos_t, sin_t, ones_row)


def _mla_kernel(q_ref, k_ref, vt_ref, o_ref, m_ref, acc_ref, s2_ref, p2_ref, *, tq, tk):
    qi = pl.program_id(1)
    per = tq // tk
    query_chunk = (qi * tq + lax.broadcasted_iota(jnp.int32, (tk, tq), 1)) // CHUNK
    key_in_tile = lax.broadcasted_iota(jnp.int32, (tk, tq), 0)
    m_ref[...] = jnp.full(m_ref.shape, NEG_BIG, F32)
    acc_ref[...] = jnp.zeros(acc_ref.shape, F32)

    def block(kb, masked, slot=0):
        ks = pl.multiple_of(kb * tk, tk)
        s_ref, p_ref = s2_ref.at[slot], p2_ref.at[slot]
        for h in range(MLA_HEADS):
            tile = slice(h * HEAD_PAD, (h + 1) * HEAD_PAD)
            s_ref[h] = _dot(k_ref[0, pl.ds(ks, tk), tile], q_ref[0, tile, :])
        if masked:
            allowed = (ks + key_in_tile) // CHUNK <= query_chunk
        alphas = []
        for h in range(MLA_HEADS):
            s = s_ref[h]
            if masked:
                s = jnp.where(allowed, s, NEG_BIG)
            m_old = m_ref[h]
            m_new = jnp.maximum(m_old, jnp.max(s, axis=0, keepdims=True))
            p_ref[h] = jnp.exp2(s - m_new).astype(BF16)
            alphas.append(jnp.exp2(m_old - m_new))
            m_ref[h] = m_new
        for h in range(MLA_HEADS):
            tile = slice(h * HEAD_PAD, (h + 1) * HEAD_PAD)
            used = slice(h * HEAD_PAD, h * HEAD_PAD + MLA_V_ROWS)
            acc_ref[h] = alphas[h] * acc_ref[h] + _dot(vt_ref[0, used, pl.ds(ks, tk)], p_ref[h])

    def body(i, carry):
        for d in range(per):
            block(i * per + d, False, d)
        return carry

    lax.fori_loop(0, qi, body, 0)
    for d in range(per):
        block(qi * per + d, True, d)
    pad = jnp.zeros((HEAD_PAD - MLA_V, tq), F32)
    for h in range(MLA_HEADS):
        acc = acc_ref[h]
        out_t = jnp.concatenate([acc[0:MLA_V, :] / acc[MLA_V:MLA_V + 1, :], pad], axis=0)
        o_ref[0, :, h * HEAD_PAD:(h + 1) * HEAD_PAD] = out_t.T.astype(BF16)


def _mla_attention(qt, k, vt):
    bsz, seq, _ = k.shape
    tq, tk = min(seq, ATTN_TQ), min(seq, ATTN_TK)
    return pl.pallas_call(
        functools.partial(_mla_kernel, tq=tq, tk=tk),
        grid=(bsz, seq // tq),
        in_specs=[pl.BlockSpec((1, MLA_QK_W, tq), lambda b, i: (b, 0, i)),
                  pl.BlockSpec((1, seq, MLA_QK_W), lambda b, i: (b, 0, 0)),
                  pl.BlockSpec((1, MLA_QK_W, seq), lambda b, i: (b, 0, 0))],
        out_specs=pl.BlockSpec((1, tq, MLA_QK_W), lambda b, i: (b, i, 0)),
        out_shape=jax.ShapeDtypeStruct((bsz, seq, MLA_QK_W), BF16),
        scratch_shapes=[pltpu.VMEM((MLA_HEADS, 1, tq), F32), pltpu.VMEM((MLA_HEADS, MLA_V_ROWS, tq), F32),
                        pltpu.VMEM((tq // tk, MLA_HEADS, tk, tq), F32), pltpu.VMEM((tq // tk, MLA_HEADS, tk, tq), BF16)],
        compiler_params=_params("parallel", "arbitrary"),
        name="mla_attention",
    )(qt, k, vt)


def _sb_kernel(q_ref, k_ref, vt_ref, tri_ref, o_ref, right_ref, acc_ref, z2_ref, split2_ref, sum2_ref, w2_ref, *, tq, tk):
    qi = pl.program_id(1)
    per = tq // tk
    scan = min(tk, SB_SCAN)
    nscan = tk // scan
    query_pos = qi * tq + lax.broadcasted_iota(jnp.int32, (tk, tq), 1)
    key_in_tile = lax.broadcasted_iota(jnp.int32, (tk, tq), 0)
    right_ref[...] = jnp.zeros(right_ref.shape, F32)
    acc_ref[...] = jnp.zeros(acc_ref.shape, F32)

    def block(kb, masked, slot=0):
        ks = pl.multiple_of(kb * tk, tk)
        z_ref, split_ref, sum_ref, w_ref = z2_ref.at[slot], split2_ref.at[slot], sum2_ref.at[slot], w2_ref.at[slot]
        for h in range(SB_HEADS):
            tile = slice(h * HEAD_PAD, (h + 1) * HEAD_PAD)
            z_ref[h] = _dot(k_ref[0, pl.ds(ks, tk), tile], q_ref[0, tile, :])
        if masked:
            earlier = ks + key_in_tile < query_pos
        for h in range(SB_HEADS):
            z = z_ref[h]
            neg_abs = pltpu.bitcast(pltpu.bitcast(z, jnp.uint32) | jnp.uint32(0x80000000), F32)
            fail = jnp.maximum(z, 0.0) + jnp.log2(1.0 + jnp.exp2(neg_abs))
            z_ref[h] = z - fail
            if masked:
                fail = jnp.where(earlier, fail, 0.0)
            for u in range(nscan):
                part = fail[u * scan:(u + 1) * scan, :]
                hi = part.astype(BF16)
                split_ref[h, 2 * u * scan:(2 * u + 1) * scan, :] = hi
                split_ref[h, (2 * u + 1) * scan:(2 * u + 2) * scan, :] = (part - hi.astype(F32)).astype(BF16)
                sum_ref[h, u] = jnp.sum(part, axis=0, keepdims=True)
        for h in range(SB_HEADS):
            after = right_ref[h]
            for u in range(nscan - 1, -1, -1):
                rows = slice(u * scan, (u + 1) * scan)
                between = _dot(tri_ref[...], split_ref[h, 2 * u * scan:(2 * u + 2) * scan, :]) + after
                w = jnp.exp2(z_ref[h, rows, :] - between)
                if masked:
                    w = jnp.where(earlier[rows, :], w, 0.0)
                w_ref[h, rows, :] = w.astype(BF16)
                after = after + sum_ref[h, u]
            right_ref[h] = after
        for h in range(SB_HEADS):
            pair = slice((h // 2) * HEAD_PAD, (h // 2 + 1) * HEAD_PAD)
            acc_ref[h] += _dot(vt_ref[0, pair, pl.ds(ks, tk)], w_ref[h])

    def body(i, carry):
        for d in range(per):
            block((qi - i) * per - 1 - d, False, d)
        return carry

    for d in range(per - 1, -1, -1):
        block(qi * per + d, True, d)
    lax.fori_loop(0, qi, body, 0)
    row = lax.broadcasted_iota(jnp.int32, (HEAD_PAD, tq), 0)
    for pair in range(SB_HEADS // 2):
        out_t = jnp.where(row < SB_DIM, acc_ref[2 * pair], acc_ref[2 * pair + 1])
        o_ref[0, :, pair * HEAD_PAD:(pair + 1) * HEAD_PAD] = out_t.T.astype(BF16)


def _sb_attention(qt, k, vt):
    bsz, seq, _ = k.shape
    tq, tk = min(seq, ATTN_TQ), min(seq, ATTN_TK)
    sb_w = SB_HEADS * HEAD_PAD
    scan = min(tk, SB_SCAN)
    tri = (lax.broadcasted_iota(jnp.int32, (scan, scan), 1) > lax.broadcasted_iota(jnp.int32, (scan, scan), 0)).astype(BF16)
    tri2 = jnp.concatenate([tri, tri], axis=1)
    return pl.pallas_call(
        functools.partial(_sb_kernel, tq=tq, tk=tk),
        grid=(bsz, seq // tq),
        in_specs=[pl.BlockSpec((1, sb_w, tq), lambda b, i: (b, 0, i)),
                  pl.BlockSpec((1, seq, sb_w), lambda b, i: (b, 0, 0)),
                  pl.BlockSpec((1, SB_WIDTH, seq), lambda b, i: (b, 0, 0)),
                  pl.BlockSpec((scan, 2 * scan), lambda b, i: (0, 0))],
        out_specs=pl.BlockSpec((1, tq, SB_WIDTH), lambda b, i: (b, i, 0)),
        out_shape=jax.ShapeDtypeStruct((bsz, seq, SB_WIDTH), BF16),
        scratch_shapes=[pltpu.VMEM((SB_HEADS, 1, tq), F32), pltpu.VMEM((SB_HEADS, HEAD_PAD, tq), F32),
                        pltpu.VMEM((tq // tk, SB_HEADS, tk, tq), F32), pltpu.VMEM((tq // tk, SB_HEADS, 2 * tk, tq), BF16),
                        pltpu.VMEM((tq // tk, SB_HEADS, tk // scan, 1, tq), F32),
                        pltpu.VMEM((tq // tk, SB_HEADS, tk, tq), BF16)],
        compiler_params=_params("parallel", "arbitrary"),
        name="sb_attention",
    )(qt, k, vt, tri2)


def _ssm_kernel(u_ref, bbd_ref, lam_ref, cbd_ref, d_ref, wglu_ref, bglu_ref, y_ref, h_ref, hs_ref, rows_ref, *, tt, nb):
    @pl.when(pl.program_id(0) == 0)
    def _():
        h_ref[...] = jnp.zeros_like(h_ref)

    tiles = SSM_WIDTH // LANE
    for b in range(nb):
        for c in range(tiles):
            lanes = slice(b * SSM_WIDTH + c * LANE, b * SSM_WIDTH + (c + 1) * LANE)
            rows_ref[c, pl.ds(b, tt, stride=nb), :] = u_ref[:, lanes].astype(F32)
    u = jnp.concatenate([rows_ref[c] for c in range(tiles)], axis=1).astype(BF16)
    half, wide = SSM_BLOCK_STATES, 2 * SSM_BLOCK_STATES
    for b in range(SSM_BLOCKS):
        hs_ref[:, b * wide:(b + 1) * wide] = _dot(u[:, b * SSM_BLOCK_CH:(b + 1) * SSM_BLOCK_CH], bbd_ref[b])
    lam_re = jnp.broadcast_to(lam_ref[0:1, :], (nb, SSM_STATES))
    lam_im = jnp.broadcast_to(lam_ref[1:2, :], (nb, SSM_STATES))

    def step(t, carry):
        r0 = pl.multiple_of(t * nb, nb)
        out = []
        for b in range(SSM_BLOCKS):
            h_re, h_im = carry[b]
            l_re, l_im = lam_re[:, b * half:(b + 1) * half], lam_im[:, b * half:(b + 1) * half]
            re_cols = slice(b * wide, b * wide + half)
            im_cols = slice(b * wide + half, (b + 1) * wide)
            n_re = l_re * h_re - l_im * h_im + hs_ref[pl.ds(r0, nb), re_cols]
            n_im = l_re * h_im + l_im * h_re + hs_ref[pl.ds(r0, nb), im_cols]
            hs_ref[pl.ds(r0, nb), re_cols] = n_re
            hs_ref[pl.ds(r0, nb), im_cols] = n_im
            out.append((n_re, n_im))
        return tuple(out)

    init = tuple((h_ref[:, b * wide:b * wide + half], h_ref[:, b * wide + half:(b + 1) * wide]) for b in range(SSM_BLOCKS))
    final = lax.fori_loop(0, tt, step, init, unroll=4)
    for b in range(SSM_BLOCKS):
        h_ref[:, b * wide:b * wide + half] = final[b][0]
        h_ref[:, b * wide + half:(b + 1) * wide] = final[b][1]

    y = jnp.concatenate([_dot(hs_ref[:, b * wide:(b + 1) * wide].astype(BF16), cbd_ref[b]) for b in range(SSM_BLOCKS)],
                        axis=1) + d_ref[...] * u.astype(F32)
    y = jax.nn.gelu(y)
    y = y * jax.nn.sigmoid(_dot(y.astype(BF16), wglu_ref[...]) + bglu_ref[...])
    for c in range(tiles):
        rows_ref[c] = y[:, c * LANE:(c + 1) * LANE]
    for b in range(nb):
        for c in range(tiles):
            lanes = slice(b * SSM_WIDTH + c * LANE, b * SSM_WIDTH + (c + 1) * LANE)
            y_ref[:, lanes] = rows_ref[c, pl.ds(b, tt, stride=nb), :].astype(BF16)


def _ssm(u_tm, bbd, lam, cbd, d_row, wglu, bglu_row, nb):
    seq = u_tm.shape[0]
    tt = min(seq, 128)
    full = lambda a: pl.BlockSpec(a.shape, lambda i: (0,) * a.ndim)
    return pl.pallas_call(
        functools.partial(_ssm_kernel, tt=tt, nb=nb),
        grid=(seq // tt,),
        in_specs=[pl.BlockSpec((tt, nb * SSM_WIDTH), lambda i: (i, 0)),
                  full(bbd), full(lam), full(cbd), full(d_row), full(wglu), full(bglu_row)],
        out_specs=pl.BlockSpec((tt, nb * SSM_WIDTH), lambda i: (i, 0)),
        out_shape=jax.ShapeDtypeStruct((seq, nb * SSM_WIDTH), BF16),
        scratch_shapes=[pltpu.VMEM((nb, 2 * SSM_STATES), F32),
                        pltpu.VMEM((tt * nb, 2 * SSM_STATES), F32),
                        pltpu.VMEM((SSM_WIDTH // LANE, tt * nb, LANE), F32)],
        compiler_params=_params("arbitrary"),
        name="ssm",
    )(u_tm, bbd, lam, cbd, d_row, wglu, bglu_row)


def _memkv_kernel(mem_ref, w_ref, k_ref, v_ref):
    kv = _dot(mem_ref[0].astype(BF16), w_ref[...])
    k_ref[0] = kv[:, :D_MODEL].astype(BF16)
    v_ref[0] = kv[:, D_MODEL:].astype(BF16)


def _memkv(mem, w_mkv):
    bsz, mlen, _ = mem.shape
    return pl.pallas_call(
        _memkv_kernel,
        grid=(bsz,),
        in_specs=[pl.BlockSpec((1, mlen, D_MODEL), lambda b: (b, 0, 0)),
                  pl.BlockSpec(w_mkv.shape, lambda b: (0, 0))],
        out_specs=[pl.BlockSpec((1, mlen, D_MODEL), lambda b: (b, 0, 0))] * 2,
        out_shape=[jax.ShapeDtypeStruct((bsz, mlen, D_MODEL), BF16)] * 2,
        compiler_params=_params("parallel"),
        name="mem_kv",
    )(mem, w_mkv)


def _mix_mem_kernel(ya_ref, ys_ref, yb_ref, x_ref, gmix_ref, wout_ref, ln1_ref,
                    mk_ref, mv_ref, wmq_ref, wmo_ref, ln2_ref, xt_ref,
                    x1_ref, xq_ref, s_ref, p_ref, o_ref, *, alpha, halves):
    gm = gmix_ref[...]
    c1, c2 = MLA_QK_W, MLA_QK_W + SSM_WIDTH
    rows = x_ref.shape[1] // halves
    parts = [slice(i * rows, (i + 1) * rows) for i in range(halves)]
    for r in parts:
        ya = (_rms_rows(ya_ref[0, r, :].astype(F32), MLA_V_W) * gm[:, 0:c1]).astype(BF16)
        ys = (_rms_rows(ys_ref[r, :].astype(F32), SSM_WIDTH) * gm[:, c1:c2]).astype(BF16)
        yb = (_rms_rows(yb_ref[0, r, :].astype(F32), SB_WIDTH) * gm[:, c2:]).astype(BF16)
        x1_ref[r, :] = _dot(ya, wout_ref[0:c1, :]) + _dot(ys, wout_ref[c1:c2, :]) + _dot(yb, wout_ref[c2:, :])
    for r in parts:
        x1 = _layer_norm_rows(alpha * x_ref[0, r, :] + x1_ref[r, :], ln1_ref[0:1, :], ln1_ref[1:2, :])
        x1_ref[r, :] = x1
        xq_ref[r, :] = x1.astype(BF16)
    for r in parts:
        xq_ref[r, :] = _dot(xq_ref[r, :], wmq_ref[...]).astype(BF16)
    for i, r in enumerate(parts):
        for h in range(MEM_HEADS):
            cols = slice(h * MEM_HEAD_DIM, (h + 1) * MEM_HEAD_DIM)
            s_ref[i * MEM_HEADS + h] = _dot_nt(xq_ref[r, cols], mk_ref[0, :, cols])
    for i in range(halves * MEM_HEADS):
        s = s_ref[i]
        p = jnp.exp(s - jnp.max(s, axis=-1, keepdims=True))
        p_ref[i] = (p / jnp.sum(p, axis=-1, keepdims=True)).astype(BF16)
    for i, r in enumerate(parts):
        for h in range(MEM_HEADS):
            cols = slice(h * MEM_HEAD_DIM, (h + 1) * MEM_HEAD_DIM)
            o_ref[r, cols] = _dot(p_ref[i * MEM_HEADS + h], mv_ref[0, :, cols]).astype(BF16)
    for r in parts:
        x2 = _layer_norm_rows(alpha * x1_ref[r, :] + _dot(o_ref[r, :], wmo_ref[...]), ln2_ref[0:1, :], ln2_ref[1:2, :])
        xt_ref[:, r] = x2.T


def _mix_mem(ya, ys_tm, yb, x, gmix, wout, ln1, mk, mv, wmq, wmo, ln2, alpha):
    bsz, seq, _ = x.shape
    ts = min(seq, 512)
    nst = seq // ts
    halves = 2
    mlen = mk.shape[1]
    tok = lambda w: pl.BlockSpec((1, ts, w), lambda b, s: (b, s, 0))
    full = lambda a: pl.BlockSpec(a.shape, lambda b, s: (0,) * a.ndim)
    return pl.pallas_call(
        functools.partial(_mix_mem_kernel, alpha=alpha, halves=halves),
        grid=(bsz, nst),
        in_specs=[tok(MLA_QK_W), pl.BlockSpec((ts, SSM_WIDTH), lambda b, s: (s, b)), tok(SB_WIDTH), tok(D_MODEL),
                  full(gmix), full(wout), full(ln1),
                  pl.BlockSpec((1, mlen, D_MODEL), lambda b, s: (b, 0, 0)),
                  pl.BlockSpec((1, mlen, D_MODEL), lambda b, s: (b, 0, 0)),
                  full(wmq), full(wmo), full(ln2)],
        out_specs=pl.BlockSpec((D_MODEL, ts), lambda b, s: (0, b * nst + s)),
        out_shape=jax.ShapeDtypeStruct((D_MODEL, bsz * seq), F32),
        scratch_shapes=[pltpu.VMEM((ts, D_MODEL), F32), pltpu.VMEM((ts, D_MODEL), BF16),
                        pltpu.VMEM((halves * MEM_HEADS, ts // halves, mlen), F32),
                        pltpu.VMEM((halves * MEM_HEADS, ts // halves, mlen), BF16),
                        pltpu.VMEM((ts, D_MODEL), BF16)],
        compiler_params=_params("parallel", "parallel"),
        name="mix_mem",
    )(ya, ys_tm, yb, x, gmix, wout, ln1, mk, mv, wmq, wmo, ln2)


_SORT16 = ((0, 1), (2, 3), (0, 2), (1, 3), (1, 2), (4, 5), (6, 7), (4, 6), (5, 7), (5, 6), (0, 4), (2, 6), (2, 4),
           (1, 5), (3, 7), (3, 5), (1, 2), (3, 4), (5, 6), (8, 9), (10, 11), (8, 10), (9, 11), (9, 10), (12, 13),
           (14, 15), (12, 14), (13, 15), (13, 14), (8, 12), (10, 14), (10, 12), (9, 13), (11, 15), (11, 13), (9, 10),
           (11, 12), (13, 14), (0, 8), (4, 12), (4, 8), (2, 10), (6, 14), (6, 10), (2, 4), (6, 8), (10, 12), (1, 9),
           (5, 13), (5, 9), (3, 11), (7, 15), (7, 11), (3, 5), (7, 9), (11, 13), (1, 2), (3, 4), (5, 6), (7, 8),
           (9, 10), (11, 12), (13, 14))


def _sort16_desc(v):
    v = list(v)
    for i, j in _SORT16:
        v[i], v[j] = jnp.maximum(v[i], v[j]), jnp.minimum(v[i], v[j])
    return v


def _merge_bitonic_desc(v):
    v = list(v)
    for j in (8, 4, 2, 1):
        for i in range(16):
            l = i ^ j
            if l > i:
                v[i], v[l] = jnp.maximum(v[i], v[l]), jnp.minimum(v[i], v[l])
    return v


def _top16_sorted(rows, sort_result=True):
    v = _sort16_desc(rows)
    for shift in (4, 2, 1):
        partner = [pltpu.roll(x, shift, axis=0) for x in v]
        v = [jnp.maximum(v[i], partner[15 - i]) for i in range(16)]
        if sort_result or shift != 1:
            v = _merge_bitonic_desc(v)
    return v


def _count_prefix(pred, v):
    p8 = pred(v[7])
    p4 = pred(jnp.where(p8, v[11], v[3]))
    p2 = pred(jnp.where(p8, jnp.where(p4, v[13], v[9]), jnp.where(p4, v[5], v[1])))
    p1 = pred(jnp.where(p8, jnp.where(p4, jnp.where(p2, v[14], v[12]), jnp.where(p2, v[10], v[8])),
                        jnp.where(p4, jnp.where(p2, v[6], v[4]), jnp.where(p2, v[2], v[0]))))
    count = (jnp.where(p8, 8.0, 0.0) + jnp.where(p4, 4.0, 0.0)) + (jnp.where(p2, 2.0, 0.0) + jnp.where(p1, 1.0, 0.0))
    return jnp.where(pred(v[15]), 16.0, count)


def _peer_route_kernel(xt_ref, wpq_ref, keys_ref, a_ref, n_ref, b_ref, r_ref, q_ref, s1_ref, s2_ref):
    q_ref[...] = _dot(wpq_ref[...], xt_ref[...].astype(BF16)).astype(BF16)
    lax.fori_loop(0, PEER_HEADS, functools.partial(_peer_route_head, q_ref, keys_ref, a_ref, n_ref, b_ref, r_ref,
                                                   s1_ref, s2_ref), 0)


def _peer_route_head(q_ref, keys_ref, a_ref, n_ref, b_ref, r_ref, s1_ref, s2_ref, h, carry):
    q0 = pl.multiple_of(h * (2 * PEER_KEY_DIM), 2 * PEER_KEY_DIM)
    s1_ref[...] = _dot(keys_ref[h, 0], q_ref[pl.ds(q0, PEER_KEY_DIM), :])
    s2_ref[...] = _dot(keys_ref[h, 1], q_ref[pl.ds(q0 + PEER_KEY_DIM, PEER_KEY_DIM), :])
    sub = lax.broadcasted_iota(jnp.int32, (8, LANE), 0)
    groups = PEER_N_KEYS // 8

    def spread(vals):
        out = vals[7]
        for j in range(6, -1, -1):
            out = jnp.where(sub == j, vals[j], out)
        return out

    def chunk(c, carry):
        ln = pl.ds(pl.multiple_of(c * LANE, LANE), LANE)
        rows1 = [s1_ref[8 * i:8 * i + 8, ln] for i in range(groups)]
        rows2 = [s2_ref[8 * i:8 * i + 8, ln] for i in range(groups)]
        v1 = _top16_sorted(rows1)
        v2 = _top16_sorted(rows2)
        v2lo, v2hi, v1hi = spread(v2[:8]), spread(v2[8:]), spread(v1[8:])
        cands = ([v1[0] + v2lo, v1[0] + v2hi] + [v1[r] + v2lo for r in range(1, 8)] + [v1hi + v2[0]]
                 + [v1[r] + v2hi for r in range(1, 7)])
        top = _top16_sorted(cands, sort_result=False)
        thr = functools.reduce(jnp.minimum, top)
        cmax = v1[0] + v2[0]
        inv_z = 1.0 / functools.reduce(jnp.add, [jnp.exp(t - cmax) for t in top])
        for i in range(groups):
            s = rows1[i]
            kept = _count_prefix(lambda t, s=s: s + t >= thr, v2)
            in_top = s >= v1[15]
            a_ref[i, h, :, ln] = jnp.where(in_top, jnp.exp(s - v1[0]), 0.0)
            n_ref[i, h, :, ln] = jnp.where(in_top, kept, 0.0)
        for i in range(groups // 2):
            bs, rs = [], []
            for s in (rows2[2 * i], rows2[2 * i + 1]):
                rank = _count_prefix(lambda t, s=s: t > s, v2)
                rs.append(rank)
                bs.append(jnp.where(rank < float(PEER_TOPK), jnp.exp(s - v2[0]) * inv_z, 0.0))
            b_ref[h, 16 * i:16 * i + 16, ln] = jnp.concatenate(bs, axis=0).astype(BF16)
            r_ref[h, 16 * i:16 * i + 16, ln] = jnp.concatenate(rs, axis=0).astype(BF16)
        return carry

    lax.fori_loop(0, s1_ref.shape[1] // LANE, chunk, 0)
    return carry


def _peer_route(xt, wpq_t, keys):
    ntok = xt.shape[1]
    tt = min(ntok, 512)
    nblk = PEER_N_KEYS // 8
    return pl.pallas_call(
        _peer_route_kernel,
        grid=(ntok // tt,),
        in_specs=[pl.BlockSpec((D_MODEL, tt), lambda i: (0, i)),
                  pl.BlockSpec(wpq_t.shape, lambda i: (0, 0)),
                  pl.BlockSpec(keys.shape, lambda i: (0, 0, 0, 0))],
        out_specs=[pl.BlockSpec((nblk, PEER_HEADS, 8, tt), lambda i: (0, 0, 0, i)),
                   pl.BlockSpec((nblk, PEER_HEADS, 8, tt), lambda i: (0, 0, 0, i)),
                   pl.BlockSpec((PEER_HEADS, PEER_N_KEYS, tt), lambda i: (0, 0, i)),
                   pl.BlockSpec((PEER_HEADS, PEER_N_KEYS, tt), lambda i: (0, 0, i))],
        out_shape=[jax.ShapeDtypeStruct((nblk, PEER_HEADS, 8, ntok), F32),
                   jax.ShapeDtypeStruct((nblk, PEER_HEADS, 8, ntok), F32),
                   jax.ShapeDtypeStruct((PEER_HEADS, PEER_N_KEYS, ntok), BF16),
                   jax.ShapeDtypeStruct((PEER_HEADS, PEER_N_KEYS, ntok), BF16)],
        scratch_shapes=[pltpu.VMEM((PEER_HEADS * 2 * PEER_KEY_DIM, tt), BF16),
                        pltpu.VMEM((PEER_N_KEYS, tt), F32), pltpu.VMEM((PEER_N_KEYS, tt), F32)],
        compiler_params=_params("parallel"),
        name="peer_route",
    )(xt, wpq_t, keys)


PEER_I1_PER_TILE = 8
PEER_TILE = PEER_I1_PER_TILE * PEER_N_KEYS


def _peer_dense_kernel(xt_ref, a_ref, n_ref, b_ref, r_ref, u_ref, vt_ref, ln_ref, o_ref,
                       acc_ref, xb_ref, gh_ref, *, alpha):
    e = pl.program_id(1)

    @pl.when(e == 0)
    def _():
        acc_ref[...] = jnp.zeros_like(acc_ref)
        xb_ref[...] = xt_ref[...].astype(BF16)

    tt = xb_ref.shape[1]
    rows = BF16_SUBLANES
    cw = min(tt, 2 * LANE)
    hw = max(tt // 2, cw)
    for half in range(tt // hw):
        hl = slice(half * hw, (half + 1) * hw)
        gh_ref[:, hl] = _dot(u_ref[...], xb_ref[:, hl]).astype(BF16)
        for j in range(PEER_I1_PER_TILE):
            for c in range(half * (hw // cw), (half + 1) * (hw // cw)):
                ln = slice(c * cw, (c + 1) * cw)
                a_rows = [jnp.broadcast_to(a_ref[0, h, j:j + 1, ln], (rows, cw)).astype(BF16) for h in range(PEER_HEADS)]
                n_rows = [jnp.broadcast_to(n_ref[0, h, j:j + 1, ln], (rows, cw)).astype(BF16) for h in range(PEER_HEADS)]
                for g in range(PEER_N_KEYS // rows):
                    i2 = slice(g * rows, (g + 1) * rows)
                    gate = None
                    for h in range(PEER_HEADS):
                        term = jnp.where(r_ref[h, i2, ln] < n_rows[h], b_ref[h, i2, ln] * a_rows[h], jnp.zeros((), BF16))
                        gate = term if gate is None else gate + term
                    e0 = j * PEER_N_KEYS + g * rows
                    gh_ref[e0:e0 + rows, ln] = gate * _gelu_tanh(gh_ref[e0:e0 + rows, ln])
        acc_ref[:, hl] += _dot(vt_ref[...], gh_ref[:, hl])

    @pl.when(e == pl.num_programs(1) - 1)
    def _():
        y = alpha * xt_ref[...] + acc_ref[...]
        mu = jnp.mean(y, axis=0, keepdims=True)
        c = y - mu
        var = jnp.mean(c * c, axis=0, keepdims=True)
        o_ref[...] = (c * lax.rsqrt(var + NORM_EPS)).T * ln_ref[0:1, :] + ln_ref[1:2, :]


def _peer_dense(xt, a, n, b, r, u_all, vt_all, layer, ln, alpha):
    ntok = xt.shape[1]
    tt = min(ntok, 1024)
    ntile = PEER_EXPERTS // PEER_TILE
    return pl.pallas_call(
        functools.partial(_peer_dense_kernel, alpha=alpha),
        grid=(ntok // tt, ntile),
        in_specs=[pl.BlockSpec((D_MODEL, tt), lambda i, e: (0, i)),
                  pl.BlockSpec((1, PEER_HEADS, PEER_I1_PER_TILE, tt), lambda i, e: (e, 0, 0, i)),
                  pl.BlockSpec((1, PEER_HEADS, PEER_I1_PER_TILE, tt), lambda i, e: (e, 0, 0, i)),
                  pl.BlockSpec((PEER_HEADS, PEER_N_KEYS, tt), lambda i, e: (0, 0, i)),
                  pl.BlockSpec((PEER_HEADS, PEER_N_KEYS, tt), lambda i, e: (0, 0, i)),
                  pl.BlockSpec((None, PEER_TILE, D_MODEL), lambda i, e: (layer, e, 0)),
                  pl.BlockSpec((None, D_MODEL, PEER_TILE), lambda i, e: (layer, 0, e)),
                  pl.BlockSpec(ln.shape, lambda i, e: (0, 0))],
        out_specs=pl.BlockSpec((tt, D_MODEL), lambda i, e: (i, 0)),
        out_shape=jax.ShapeDtypeStruct((ntok, D_MODEL), F32),
        scratch_shapes=[pltpu.VMEM((D_MODEL, tt), F32), pltpu.VMEM((D_MODEL, tt), BF16),
                        pltpu.VMEM((PEER_TILE, tt), BF16)],
        compiler_params=_params("parallel", "arbitrary"),
        name="peer_dense",
    )(xt, a, n, b, r, u_all, vt_all, ln)


def _pad_heads(w, heads, width):
    rows = w.shape[0]
    w = w.reshape(rows, heads, width)
    return jnp.pad(w, ((0, 0), (0, 0), (0, HEAD_PAD - width))).reshape(rows, heads * HEAD_PAD)


def _rotate_half_cols(w, heads, width, nope):
    rows = w.shape[0]
    w = w.reshape(rows, heads, width)
    half = (width - nope) // 2
    x1, x2 = w[..., nope:nope + half], w[..., nope + half:]
    out = jnp.concatenate([jnp.zeros_like(w[..., :nope]), -x2, x1], axis=-1)
    return out.reshape(rows, heads * width)


def _pack_inproj(w_in, g_cq, g_ckv, w_uq, w_ukv):
    c = 0
    cols = {}
    for name, width in (("cq", MLA_Q_RANK), ("ckv", MLA_KV_RANK), ("kr", MLA_ROPE), ("ssm", SSM_WIDTH),
                        ("qsb", SB_WIDTH), ("ksb", SB_WIDTH), ("vsb", SB_WIDTH)):
        cols[name] = w_in[:, c:c + width]
        c += width
    rows = w_in.shape[0]
    zeros = lambda n: jnp.zeros((rows, n), F32)
    kr = cols["kr"]
    half = MLA_ROPE // 2
    kr_plain = jnp.concatenate([zeros(MLA_NOPE), kr, zeros(HEAD_PAD - MLA_NOPE - MLA_ROPE)], axis=1)
    kr_swap = jnp.concatenate([zeros(MLA_NOPE), -kr[:, half:], kr[:, :half],
                               zeros(HEAD_PAD - MLA_NOPE - MLA_ROPE)], axis=1)
    sb_scale = SB_DIM ** -0.5 * LOG2E
    win = jnp.concatenate([cols["cq"], cols["ckv"], kr_plain, kr_swap, cols["ssm"],
                           _pad_heads(cols["qsb"] * sb_scale, SB_HEADS, SB_DIM),
                           _pad_heads(cols["ksb"], SB_HEADS, SB_DIM), cols["vsb"]], axis=1).astype(BF16)

    qk_dim = MLA_NOPE + MLA_ROPE
    wq = w_uq * (g_cq[:, None] * (qk_dim ** -0.5 * LOG2E))
    wuq = jnp.concatenate([_pad_heads(wq, MLA_HEADS, qk_dim),
                           _pad_heads(_rotate_half_cols(wq, MLA_HEADS, qk_dim, MLA_NOPE), MLA_HEADS, qk_dim)],
                          axis=1).astype(BF16)
    wkv = (w_ukv * g_ckv[:, None]).reshape(MLA_KV_RANK, MLA_HEADS, MLA_NOPE + MLA_V)
    wukv = jnp.concatenate([_pad_heads(wkv[..., :MLA_NOPE].reshape(MLA_KV_RANK, -1), MLA_HEADS, MLA_NOPE),
                            _pad_heads(wkv[..., MLA_NOPE:].reshape(MLA_KV_RANK, -1), MLA_HEADS, MLA_V)],
                           axis=1).astype(BF16)
    return win, wuq, wukv


def _pad_mla_rows(w):
    cols = w.shape[1]
    head_rows = jnp.pad(w[:MLA_V_W].reshape(MLA_HEADS, MLA_V, cols), ((0, 0), (0, HEAD_PAD - MLA_V), (0, 0)))
    return jnp.concatenate([head_rows.reshape(MLA_QK_W, cols), w[MLA_V_W:]], axis=0)


def _pack_ssm(lam_re, lam_im, log_step, b_re, b_im, c_re, c_im, d_skip):
    step = jnp.exp(log_step)[:, None]
    decay = jnp.exp(lam_re * step)
    ab_re, ab_im = decay * jnp.cos(lam_im * step), decay * jnp.sin(lam_im * step)
    inv = 1.0 / (lam_re * lam_re + lam_im * lam_im)
    f_re = ((ab_re - 1.0) * lam_re + ab_im * lam_im) * inv
    f_im = (ab_im * lam_re - (ab_re - 1.0) * lam_im) * inv
    bb_re = f_re[..., None] * b_re - f_im[..., None] * b_im
    bb_im = f_re[..., None] * b_im + f_im[..., None] * b_re
    per = SSM_GROUPS // SSM_BLOCKS
    eye = jnp.eye(per, dtype=F32)
    split = lambda w: w.reshape((SSM_BLOCKS, per) + w.shape[1:])
    blk = lambda w: jnp.einsum("bgph,gk->bghkp", split(w), eye).reshape(SSM_BLOCKS, SSM_BLOCK_CH, SSM_BLOCK_STATES)
    bbd = jnp.concatenate([blk(bb_re), blk(bb_im)], axis=2).astype(BF16)
    blk_c = lambda w: jnp.einsum("bghp,gk->bgpkh", split(w), eye).reshape(SSM_BLOCKS, SSM_BLOCK_STATES, SSM_BLOCK_CH)
    cbd = jnp.concatenate([blk_c(c_re), blk_c(-c_im)], axis=1).astype(BF16)
    lam = jnp.stack([ab_re.reshape(-1), ab_im.reshape(-1)])
    return bbd, lam, cbd, d_skip.reshape(1, SSM_WIDTH)


def kernel(x, mem, positions, w_in, g_cq, g_ckv, w_uq, w_ukv, ssm_lam_re, ssm_lam_im, ssm_log_step, ssm_b_re, ssm_b_im, ssm_c_re, ssm_c_im, ssm_d, w_glu, b_glu, g_mix, w_out, ln_mix_g, ln_mix_b, w_mq, w_mkv, w_mo, ln_mem_g, ln_mem_b, w_pq, peer_sub_keys, peer_u, peer_v, ln_ffn_g, ln_ffn_b):
    bsz, seq, _ = x.shape
    depth = w_in.shape[0]
    alpha = (2 * depth) ** 0.25
    cos_t, sin_t = _rope_tables(positions)
    u_all = peer_u.astype(BF16)
    vt_all = jnp.swapaxes(peer_v, 1, 2).astype(BF16)
    for l in range(depth):
        win, wuq, wukv = _pack_inproj(w_in[l], g_cq[l], g_ckv[l], w_uq[l], w_ukv[l])
        q, k, v, u_tm, q_sb, k_sb, v_sb = _inproj(x, win, wuq, wukv, cos_t, sin_t)
        y_mla = _mla_attention(q, k, v)
        y_sb = _sb_attention(q_sb, k_sb, v_sb)
        bbd, lam, cbd, d_row = _pack_ssm(ssm_lam_re[l], ssm_lam_im[l], ssm_log_step[l], ssm_b_re[l], ssm_b_im[l],
                                         ssm_c_re[l], ssm_c_im[l], ssm_d[l])
        y_ssm = _ssm(u_tm, bbd, lam, cbd, d_row,
                     w_glu[l].astype(BF16), b_glu[l].reshape(1, SSM_WIDTH), bsz)
        mk, mv = _memkv(mem, w_mkv[l].astype(BF16))
        xt = _mix_mem(y_mla, y_ssm, y_sb, x,
                      _pad_mla_rows(g_mix[l][:, None]).reshape(1, -1), _pad_mla_rows(w_out[l]).astype(BF16),
                      jnp.stack([ln_mix_g[l], ln_mix_b[l]]),
                      mk, mv, (w_mq[l] * MEM_HEAD_DIM ** -0.5).astype(BF16), w_mo[l].astype(BF16),
                      jnp.stack([ln_mem_g[l], ln_mem_b[l]]), alpha)
        a, n, b, r = _peer_route(xt, w_pq[l].T.astype(BF16), peer_sub_keys[l].astype(BF16))
        x = _peer_dense(xt, a, n, b, r, u_all, vt_all, l,
                        jnp.stack([ln_ffn_g[l], ln_ffn_b[l]]), alpha).reshape(bsz, seq, D_MODEL)
    return x
```
